```python
import math
import jax, jax.numpy as jnp
from jax import lax
import numpy as np

D_MODEL = 1024
BATCH = 4
SEQ = 8192
DEPTH = 2

D_MIX = D_MODEL
W_CONF = D_MIX // 4
W_POOL = D_MIX // 4
W_SCONV = D_MIX // 4
W_ATTN = D_MIX - W_CONF - W_POOL - W_SCONV
CONF_WIDTH = 31
POOL_WINDOWS = (2, 4, 8, 16)
POOL_GROUP = W_POOL // len(POOL_WINDOWS)
SCONV_WIDTH = 3
ATTN_HEAD_DIM = 64
ATTN_HEADS = W_ATTN // ATTN_HEAD_DIM
ATTN_BLOCK = 128
SPLIT_SIZES = (W_CONF, W_CONF, W_POOL, W_SCONV, W_SCONV, W_SCONV, W_ATTN, W_ATTN, W_ATTN)
D_IN = sum(SPLIT_SIZES)
PEER_HEADS = 8
PEER_NKEYS = 128
PEER_EXPERTS = PEER_NKEYS * PEER_NKEYS
PEER_DKEY = 256
PEER_TOPK = 16
PEER_CHUNK = 128
DEEPNORM_ALPHA = (2 * DEPTH) ** 0.25
DEEPNORM_BETA = (8 * DEPTH) ** -0.25
LN_EPS = 1e-5

kernel_name = "hybrid_headgroup_peer_deepnorm"


def layer_norm(x, g, b):
    xf = x.astype(jnp.float32)
    mu = jnp.mean(xf, axis=-1, keepdims=True)
    xc = xf - mu
    var = jnp.mean(xc * xc, axis=-1, keepdims=True)
    return (xc * lax.rsqrt(var + LN_EPS) * g + b).astype(x.dtype)


def depthwise_causal_conv(x, w):
    k_width, ch = w.shape
    return lax.conv_general_dilated(
        x, w[:, None, :].astype(x.dtype), window_strides=(1,), padding=[(k_width - 1, 0)],
        dimension_numbers=("NWC", "WIO", "NWC"), feature_group_count=ch)


def conformer_conv(a_val, a_gate, conv_w, conv_b, norm_g, norm_b):
    u = a_val * jax.nn.sigmoid(a_gate)
    h = depthwise_causal_conv(u, conv_w) + conv_b
    return jax.nn.silu(layer_norm(h, norm_g, norm_b))


def multiscale_pool(p, pool_w, pool_scale):
    bn, s, _ = p.shape
    groups = p.reshape(bn, s, len(POOL_WINDOWS), POOL_GROUP)
    t = jnp.arange(s)
    outs = []
    for g, win in enumerate(POOL_WINDOWS):
        xg = groups[:, :, g, :].astype(jnp.float32)
        cs = jnp.cumsum(xg, axis=1)
        prev = jnp.pad(cs, ((0, 0), (win, 0), (0, 0)))[:, :s]
        count = jnp.minimum(t + 1, win).astype(jnp.float32)
        outs.append((cs - prev) / count[None, :, None] - xg)
    pooled = jnp.stack(outs, axis=2).astype(p.dtype)
    mixed = jnp.einsum("bsgc,gce->bsge", pooled, pool_w)
    return mixed.reshape(bn, s, W_POOL) * pool_scale


def stick_breaking_attention(q, k, v):
    bn, s, _ = q.shape
    nb = s // ATTN_BLOCK

    def heads(a):
        return a.reshape(bn, s, ATTN_HEADS, ATTN_HEAD_DIM).transpose(0, 2, 1, 3)

    q, k, v = heads(q), heads(k), heads(v)
    q_blocks = q.reshape(bn, ATTN_HEADS, nb, ATTN_BLOCK, ATTN_HEAD_DIM).transpose(2, 0, 1, 3, 4)
    k_pos = jnp.arange(s)
    scale = ATTN_HEAD_DIM ** -0.5

    def block(args):
        q_blk, blk = args
        z = jnp.einsum("bhqd,bhkd->bhqk", q_blk, k).astype(jnp.float32) * scale
        q_pos = blk * ATTN_BLOCK + jnp.arange(ATTN_BLOCK)
        causal = k_pos[None, :] < q_pos[:, None]
        log_stay = jnp.where(causal, jax.nn.log_sigmoid(-z), 0.0)
        cum = jnp.cumsum(log_stay, axis=-1)
        log_w = jax.nn.log_sigmoid(z) + (cum[..., -1:] - cum)
        w = jnp.where(causal, jnp.exp(log_w), 0.0)
        return jnp.einsum("bhqk,bhkd->bhqd", w.astype(v.dtype), v)

    out = lax.map(block, (q_blocks, jnp.arange(nb)))
    return out.transpose(1, 0, 3, 2, 4).reshape(bn, s, W_ATTN)


def token_mix(x, w_in, conv_a_w, conv_a_b, norm_a_g, norm_a_b, pool_w, pool_scale, conv_c_w, w_out):
    proj = jnp.einsum("bsd,de->bse", x, w_in)
    split_points = [int(i) for i in np.cumsum(SPLIT_SIZES)[:-1]]
    a_val, a_gate, p_in, c_h, c_gate_b, c_gate_c, q, k, v = jnp.split(proj, split_points, axis=-1)
    y_a = conformer_conv(a_val, a_gate, conv_a_w, conv_a_b, norm_a_g, norm_a_b)
    y_b = multiscale_pool(p_in, pool_w, pool_scale)
    y_c = c_gate_b * depthwise_causal_conv(c_gate_c * c_h, conv_c_w)
    y_d = stick_breaking_attention(q, k, v)
    y = jnp.concatenate([y_a, y_b, y_c, y_d], axis=-1)
    return jnp.einsum("bse,ed->bsd", y, w_out)


def peer_ffn(x, wq, sub_keys, u_tab, v_tab):
    bn, s, d = x.shape
    t = bn * s
    xf = x.reshape(t, d)
    q = (xf @ wq).reshape(t, PEER_HEADS, 2, PEER_DKEY // 2)
    scores = jnp.einsum("thpc,hpnc->thpn", q, sub_keys).astype(jnp.float32)
    s_top, i_top = lax.top_k(scores, PEER_TOPK)
    cand_s = (s_top[:, :, 0, :, None] + s_top[:, :, 1, None, :]).reshape(t, PEER_HEADS, PEER_TOPK * PEER_TOPK)
    cand_i = (i_top[:, :, 0, :, None] * PEER_NKEYS + i_top[:, :, 1, None, :]).reshape(t, PEER_HEADS, PEER_TOPK * PEER_TOPK)
    best_s, best_pos = lax.top_k(cand_s, PEER_TOPK)
    expert_idx = jnp.take_along_axis(cand_i, best_pos, axis=-1)
    gates = jax.nn.softmax(best_s, axis=-1)
    n_chunks = t // PEER_CHUNK

    def chunk(args):
        xc, ec, gc = args
        u = jnp.take(u_tab, ec, axis=0)
        act = jax.nn.gelu(jnp.einsum("cd,chkd->chk", xc, u).astype(jnp.float32)) * gc
        vv = jnp.take(v_tab, ec, axis=0)
        return jnp.einsum("chk,chkd->cd", act.astype(vv.dtype), vv)

    out = lax.map(chunk, (xf.reshape(n_chunks, PEER_CHUNK, d),
                          expert_idx.reshape(n_chunks, PEER_CHUNK, PEER_HEADS, PEER_TOPK),
                          gates.reshape(n_chunks, PEER_CHUNK, PEER_HEADS, PEER_TOPK)))
    return out.reshape(bn, s, d).astype(x.dtype)


def setup_inputs(seed: int = 0) -> dict:
    key = jax.random.key(seed)
    ks = jax.random.split(key, 20)
    L = DEPTH

    def nrm(k, shape, scale):
        return jax.random.normal(k, shape, jnp.float32) * scale

    return {
        "x": nrm(ks[0], (BATCH, SEQ, D_MODEL), 1.0),
        "w_in": nrm(ks[1], (L, D_MODEL, D_IN), D_MODEL ** -0.5),
        "conv_a_w": nrm(ks[2], (L, CONF_WIDTH, W_CONF), CONF_WIDTH ** -0.5),
        "conv_a_b": nrm(ks[3], (L, W_CONF), 0.02),
        "norm_a_g": 1.0 + nrm(ks[4], (L, W_CONF), 0.02),
        "norm_a_b": nrm(ks[5], (L, W_CONF), 0.02),
        "pool_w": nrm(ks[6], (L, len(POOL_WINDOWS), POOL_GROUP, POOL_GROUP), POOL_GROUP ** -0.5),
        "pool_scale": 1.0 + nrm(ks[7], (L, W_POOL), 0.02),
        "conv_c_w": nrm(ks[8], (L, SCONV_WIDTH, W_SCONV), SCONV_WIDTH ** -0.5),
        "w_out": nrm(ks[9], (L, D_MIX, D_MODEL), D_MIX ** -0.5 * DEEPNORM_BETA),
        "ln1_g": 1.0 + nrm(ks[10], (L, D_MODEL), 0.02),
        "ln1_b": nrm(ks[11], (L, D_MODEL), 0.02),
        "peer_wq": nrm(ks[12], (L, D_MODEL, PEER_HEADS * PEER_DKEY), D_MODEL ** -0.5),
        "peer_keys": nrm(ks[13], (L, PEER_HEADS, 2, PEER_NKEYS, PEER_DKEY // 2), (PEER_DKEY // 2) ** -0.5),
        "peer_u": nrm(ks[14], (L, PEER_EXPERTS, D_MODEL), D_MODEL ** -0.5),
        "peer_v": nrm(ks[15], (L, PEER_EXPERTS, D_MODEL), PEER_HEADS ** -0.5 * DEEPNORM_BETA),
        "ln2_g": 1.0 + nrm(ks[16], (L, D_MODEL), 0.02),
        "ln2_b": nrm(ks[17], (L, D_MODEL), 0.02),
    }


def reference(x, w_in, conv_a_w, conv_a_b, norm_a_g, norm_a_b, pool_w, pool_scale, conv_c_w, w_out,
              ln1_g, ln1_b, peer_wq, peer_keys, peer_u, peer_v, ln2_g, ln2_b):
    for l in range(DEPTH):
        m = token_mix(x, w_in[l], conv_a_w[l], conv_a_b[l], norm_a_g[l], norm_a_b[l],
                      pool_w[l], pool_scale[l], conv_c_w[l], w_out[l])
        x = layer_norm(DEEPNORM_ALPHA * x + m, ln1_g[l], ln1_b[l])
        f = peer_ffn(x, peer_wq[l], peer_keys[l], peer_u[l], peer_v[l])
        x = layer_norm(DEEPNORM_ALPHA * x + f, ln2_g[l], ln2_b[l])
    return x
```

```python
import functools

import jax
import jax.numpy as jnp
from jax import lax
from jax.experimental import pallas as pl
from jax.experimental.pallas import tpu as pltpu

F32 = jnp.float32
BF16 = jnp.bfloat16
I32 = jnp.int32

LN_EPS = 1e-5
LANES = 128
CONF_WIDTH = 31
POOL_WINDOWS = (2, 4, 8, 16)
SCONV_WIDTH = 3
HEAD_DIM = 64
PEER_HEADS = 8
PEER_NKEYS = 128
PEER_TOPK = 16
HALO = 32
ATTN_UNDERFLOW = -100.0
VMEM_LIMIT = 56 * 1024 * 1024


def _cparams(*sem):
    return pltpu.CompilerParams(dimension_semantics=sem, vmem_limit_bytes=VMEM_LIMIT)


def _layer_norm(h, g, b):
    mu = jnp.mean(h, axis=-1, keepdims=True)
    hc = h - mu
    var = jnp.mean(hc * hc, axis=-1, keepdims=True)
    return hc * lax.rsqrt(var + LN_EPS) * g + b


def _dot_nt(a, b):
    return lax.dot_general(a, b, (((1,), (1,)), ((), ())), preferred_element_type=F32)


def _in_proj_kernel(x_ref, w_ref, o_ref):
    o_ref[...] = jnp.dot(x_ref[...].astype(BF16), w_ref[...], preferred_element_type=F32)


def in_proj(x2d, w_bf16, tm):
    t, d = x2d.shape
    n = w_bf16.shape[1]
    return pl.pallas_call(
        _in_proj_kernel,
        grid=(t // tm,),
        in_specs=[pl.BlockSpec((tm, d), lambda i: (i, 0)),
                  pl.BlockSpec((d, n), lambda i: (0, 0))],
        out_specs=pl.BlockSpec((tm, n), lambda i: (i, 0)),
        out_shape=jax.ShapeDtypeStruct((t, n), F32),
        compiler_params=_cparams("parallel"),
        name="in_proj",
    )(x2d, w_bf16)


def _local_mix_kernel(p_ref, caw_ref, cab_ref, nag_ref, nab_ref, pw_ref, ps_ref, ccw_ref,
                      o_ref, hist, *, ts, w):
    s = pl.program_id(1)

    @pl.when(s == 0)
    def _():
        hist[0:HALO, :] = jnp.zeros((HALO, 3 * w), F32)

    a_val = p_ref[:, 0:w]
    a_gate = p_ref[:, w:2 * w]
    c_h = p_ref[:, 3 * w:4 * w]
    c_gate_b = p_ref[:, 4 * w:5 * w]
    c_gate_c = p_ref[:, 5 * w:6 * w]
    hist[HALO:HALO + ts, 0:w] = a_val * jax.nn.sigmoid(a_gate)
    hist[HALO:HALO + ts, w:2 * w] = p_ref[:, 2 * w:3 * w]
    hist[HALO:HALO + ts, 2 * w:3 * w] = c_gate_c * c_h

    def past(col, back):
        return hist[HALO - back:HALO - back + ts, col * w:(col + 1) * w]

    acc = jnp.zeros((ts, w), F32)
    for k in range(CONF_WIDTH):
        acc = acc + caw_ref[k:k + 1, :] * past(0, CONF_WIDTH - 1 - k)
    h = _layer_norm(acc + cab_ref[...], nag_ref[...], nab_ref[...])
    o_ref[:, 0:w] = h * jax.nn.sigmoid(h)

    t_pos = s * ts + lax.broadcasted_iota(I32, (ts, 1), 0)
    chan = lax.broadcasted_iota(I32, (ts, w), 1)
    group = w // len(POOL_WINDOWS)
    x0 = past(1, 0)
    run = x0
    pooled = jnp.zeros((ts, w), F32)
    done = 1
    for g, win in enumerate(POOL_WINDOWS):
        for back in range(done, win):
            run = run + past(1, back)
        done = win
        inv = 1.0 / jnp.minimum(t_pos + 1, win).astype(F32)
        in_group = (chan >= g * group) & (chan < (g + 1) * group)
        pooled = jnp.where(in_group, run * inv, pooled)
    pooled = pooled - x0
    mixed = jnp.dot(pooled.astype(BF16), pw_ref[...], preferred_element_type=F32)
    o_ref[:, w:2 * w] = mixed * ps_ref[...]

    conv = jnp.zeros((ts, w), F32)
    for k in range(SCONV_WIDTH):
        conv = conv + ccw_ref[k:k + 1, :] * past(2, SCONV_WIDTH - 1 - k)
    o_ref[:, 2 * w:3 * w] = c_gate_b * conv

    hist[0:HALO, :] = hist[ts:ts + HALO, :]


def local_mix(proj3, conv_a_w, conv_a_b, norm_a_g, norm_a_b, pool_bd, pool_scale, conv_c_w, ts):
    b, s, _ = proj3.shape
    w = conv_a_w.shape[1]
    row = lambda a: a.reshape(1, w)
    full = lambda shape: pl.BlockSpec(shape, lambda bi, si: (0,) * len(shape))
    return pl.pallas_call(
        functools.partial(_local_mix_kernel, ts=ts, w=w),
        grid=(b, s // ts),
        in_specs=[pl.BlockSpec((None, ts, 6 * w), lambda bi, si: (bi, si, 0)),
                  full((CONF_WIDTH, w)), full((1, w)), full((1, w)), full((1, w)),
                  full((w, w)), full((1, w)), full((SCONV_WIDTH, w))],
        out_specs=pl.BlockSpec((None, ts, 3 * w), lambda bi, si: (bi, si, 0)),
        out_shape=jax.ShapeDtypeStruct((b, s, 3 * w), F32),
        scratch_shapes=[pltpu.VMEM((HALO + ts, 3 * w), F32)],
        compiler_params=_cparams("arbitrary", "arbitrary"),
        name="local_mix",
    )(proj3, conv_a_w, row(conv_a_b), row(norm_a_g), row(norm_a_b), pool_bd, row(pool_scale), conv_c_w)


def _attention_kernel(q_ref, k_ref, v_ref, m_ref, o_ref, acc_ref, car_ref, *, tq, scale):
    i = pl.program_id(2)
    q = q_ref[...]
    lane = lax.broadcasted_iota(I32, (tq, LANES), 1)
    q_heads = (jnp.where(lane < HEAD_DIM, q, 0.0).astype(BF16),
               jnp.where(lane >= HEAD_DIM, q, 0.0).astype(BF16))
    acc_ref[...] = jnp.zeros(acc_ref.shape, F32)
    car_ref[...] = jnp.zeros(car_ref.shape, F32)
    row = lax.broadcasted_iota(I32, (tq, tq), 0)
    col = lax.broadcasted_iota(I32, (tq, tq), 1)
    causal = col < row
    later = m_ref[...]

    def visit(j, diagonal):
        start = pl.multiple_of(j * tq, tq)
        kb = k_ref[pl.ds(start, tq), :].astype(BF16)
        vb = v_ref[pl.ds(start, tq), :].astype(BF16)
        for h in range(2):
            z = _dot_nt(q_heads[h], kb) * scale
            softplus = jnp.maximum(z, 0.0) + jnp.log1p(jnp.exp(-jnp.abs(z)))
            log_stay = -softplus
            log_att = z - softplus
            if diagonal:
                log_stay = jnp.where(causal, log_stay, 0.0)
            hi = log_stay.astype(BF16)
            lo = (log_stay - hi.astype(F32)).astype(BF16)
            inside = (jnp.dot(hi, later, preferred_element_type=F32)
                      + jnp.dot(lo, later, preferred_element_type=F32))
            carried = car_ref[h]
            wgt = jnp.exp(log_att + inside + carried)
            if diagonal:
                wgt = jnp.where(causal, wgt, 0.0)
            acc_ref[h] += jnp.dot(wgt.astype(BF16), vb, preferred_element_type=F32)
            car_ref[h] = carried + jnp.sum(log_stay, axis=1, keepdims=True)

    visit(i, True)

    def cond(c):
        j, worst = c
        return (j >= 0) & (worst > ATTN_UNDERFLOW)

    def body(c):
        j, _ = c
        visit(j, False)
        return j - 1, jnp.max(car_ref[...])

    lax.while_loop(cond, body, (i - 1, jnp.max(car_ref[...])))
    o_ref[...] = jnp.where(lane < HEAD_DIM, acc_ref[0], acc_ref[1])


def attention(proj3, later, tq, col0):
    b, s, _ = proj3.shape
    pairs = 2
    scale = HEAD_DIM ** -0.5
    return pl.pallas_call(
        functools.partial(_attention_kernel, tq=tq, scale=scale),
        grid=(b, pairs, s // tq),
        in_specs=[pl.BlockSpec((None, tq, LANES), lambda bi, p, i: (bi, i, col0 + p)),
                  pl.BlockSpec((None, s, LANES), lambda bi, p, i: (bi, 0, col0 + pairs + p)),
                  pl.BlockSpec((None, s, LANES), lambda bi, p, i: (bi, 0, col0 + 2 * pairs + p)),
                  pl.BlockSpec((tq, tq), lambda bi, p, i: (0, 0))],
        out_specs=pl.BlockSpec((None, tq, LANES), lambda bi, p, i: (bi, i, p)),
        out_shape=jax.ShapeDtypeStruct((b, s, pairs * LANES), F32),
        scratch_shapes=[pltpu.VMEM((2, tq, LANES), F32), pltpu.VMEM((2, tq, 1), F32)],
        compiler_params=_cparams("parallel", "parallel", "arbitrary"),
        name="attention",
    )(proj3, proj3, proj3, later)


def _out_ln_kernel(ya_ref, yd_ref, x_ref, w_ref, g_ref, b_ref, o_ref, *, alpha, wa):
    m = jnp.dot(ya_ref[...].astype(BF16), w_ref[0:wa, :], preferred_element_type=F32)
    m = m + jnp.dot(yd_ref[...].astype(BF16), w_ref[wa:, :], preferred_element_type=F32)
    o_ref[...] = _layer_norm(alpha * x_ref[...] + m, g_ref[...], b_ref[...])


def out_ln(y_abc, y_d, x2d, w_out_bf16, g, b, alpha, tm):
    t, d = x2d.shape
    wa, wd = y_abc.shape[1], y_d.shape[1]
    return pl.pallas_call(
        functools.partial(_out_ln_kernel, alpha=alpha, wa=wa),
        grid=(t // tm,),
        in_specs=[pl.BlockSpec((tm, wa), lambda i: (i, 0)),
                  pl.BlockSpec((tm, wd), lambda i: (i, 0)),
                  pl.BlockSpec((tm, d), lambda i: (i, 0)),
                  pl.BlockSpec((wa + wd, d), lambda i: (0, 0)),
                  pl.BlockSpec((1, d), lambda i: (0, 0)),
                  pl.BlockSpec((1, d), lambda i: (0, 0))],
        out_specs=pl.BlockSpec((tm, d), lambda i: (i, 0)),
        out_shape=jax.ShapeDtypeStruct((t, d), F32),
        compiler_params=_cparams("parallel"),
        name="out_ln",
    )(y_abc, y_d, x2d, w_out_bf16, g.reshape(1, d), b.reshape(1, d))


def _top_k_rows(vals, payload, k):
    n, m = vals.shape
    rows = lax.broadcasted_iota(I32, (n, m), 0)
    out_v, out_p = [], []
    for _ in range(k):
        best = jnp.max(vals, axis=0, keepdims=True)
        where_best = jnp.min(jnp.where(vals == best, rows, n), axis=0, keepdims=True)
        hit = rows == where_best
        out_v.append(best)
        out_p.append(jnp.max(jnp.where(hit, payload, -1), axis=0, keepdims=True))
        vals = jnp.where(hit, -jnp.inf, vals)
    return jnp.concatenate(out_v, axis=0), jnp.concatenate(out_p, axis=0)


def _route_kernel(x_ref, wq_ref, keys_ref, idx_ref, gate_ref, *, tm):
    half = PEER_NKEYS
    qt = _dot_nt(wq_ref[...], x_ref[...].astype(BF16)).astype(BF16)
    key_id = lax.broadcasted_iota(I32, (PEER_NKEYS, tm), 0)
    idx_rows, gate_rows = [], []
    for h in range(PEER_HEADS):
        tops = []
        for p in range(2):
            hp = 2 * h + p
            scores = jnp.dot(keys_ref[hp], qt[hp * half:(hp + 1) * half, :], preferred_element_type=F32)
            tops.append(_top_k_rows(scores, key_id, PEER_TOPK))
        (s0, i0), (s1, i1) = tops
        cand_s = jnp.concatenate([s0[a:a + 1, :] + s1 for a in range(PEER_TOPK)], axis=0)
        cand_i = jnp.concatenate([i0[a:a + 1, :] * PEER_NKEYS + i1 for a in range(PEER_TOPK)], axis=0)
        best_s, best_i = _top_k_rows(cand_s, cand_i, PEER_TOPK)
        e = jnp.exp(best_s - best_s[0:1, :])
        gate_rows.append(e / jnp.sum(e, axis=0, keepdims=True))
        idx_rows.append(best_i)
    idx_ref[...] = jnp.concatenate(idx_rows, axis=0).T
    gate_ref[...] = jnp.concatenate(gate_rows, axis=0).T


def route(x2d, wq_t_bf16, keys_bf16, tm):
    t, d = x2d.shape
    nq = wq_t_bf16.shape[0]
    sel = PEER_HEADS * PEER_TOPK
    return pl.pallas_call(
        functools.partial(_route_kernel, tm=tm),
        grid=(t // tm,),
        in_specs=[pl.BlockSpec((tm, d), lambda i: (i, 0)),
                  pl.BlockSpec((nq, d), lambda i: (0, 0)),
                  pl.BlockSpec(keys_bf16.shape, lambda i: (0, 0, 0))],
        out_specs=[pl.BlockSpec((tm, sel), lambda i: (i, 0)),
                   pl.BlockSpec((tm, sel), lambda i: (i, 0))],
        out_shape=[jax.ShapeDtypeStruct((t, sel), I32), jax.ShapeDtypeStruct((t, sel), F32)],
        compiler_params=_cparams("parallel"),
        name="route",
    )(x2d, wq_t_bf16, keys_bf16)


def _peer_kernel(idx_now, idx_next, x_ref, gate_ref, g_ref, b_ref, tab_ref, o_ref, rows, sems,
                 *, c, sel, d, alpha):
    i = pl.program_id(0)
    n = pl.num_programs(0)
    slot = i % 2

    def gather(idx_ref, dst_slot):
        def per_token(tok, carry):
            for j in range(sel):
                e = idx_ref[tok, j]
                pltpu.make_async_copy(tab_ref.at[pl.ds(e, 1), :],
                                      rows.at[dst_slot, pl.ds(tok * sel + j, 1), :],
                                      sems.at[dst_slot]).start()
            return carry
        lax.fori_loop(0, c, per_token, 0)

    @pl.when(i == 0)
    def _():
        gather(idx_now, 0)

    @pl.when(i + 1 < n)
    def _():
        gather(idx_next, 1 - slot)

    pltpu.make_async_copy(tab_ref.at[pl.ds(0, c * sel), :], rows.at[slot], sems.at[slot]).wait()

    x = x_ref[...]
    u = rows[slot, :, 0:d].astype(BF16)
    v = rows[slot, :, d:2 * d].astype(BF16)
    cross = _dot_nt(x.astype(BF16), u)
    tok = lax.broadcasted_iota(I32, (c, sel), 0)
    act = jnp.zeros((c, sel), F32)
    for r in range(c):
        act = act + jnp.where(tok == r, cross[:, r * sel:(r + 1) * sel], 0.0)
    act = jax.nn.gelu(act) * gate_ref[...]
    spread = jnp.concatenate([jnp.where(tok == r, act, 0.0) for r in range(c)], axis=1)
    f = jnp.dot(spread.astype(BF16), v, preferred_element_type=F32)
    o_ref[...] = _layer_norm(alpha * x + f, g_ref[...], b_ref[...])


def peer(x2d, idx, gates, table, g, b, alpha, c):
    t, d = x2d.shape
    sel = idx.shape[1]
    n = t // c
    return pl.pallas_call(
        functools.partial(_peer_kernel, c=c, sel=sel, d=d, alpha=alpha),
        grid=(n,),
        in_specs=[pl.BlockSpec((c, sel), lambda i: (i, 0), memory_space=pltpu.SMEM),
                  pl.BlockSpec((c, sel), lambda i: (jnp.minimum(i + 1, n - 1), 0), memory_space=pltpu.SMEM),
                  pl.BlockSpec((c, d), lambda i: (i, 0)),
                  pl.BlockSpec((c, sel), lambda i: (i, 0)),
                  pl.BlockSpec((1, d), lambda i: (0, 0)),
                  pl.BlockSpec((1, d), lambda i: (0, 0)),
                  pl.BlockSpec(memory_space=pl.ANY)],
        out_specs=pl.BlockSpec((c, d), lambda i: (i, 0)),
        out_shape=jax.ShapeDtypeStruct((t, d), F32),
        scratch_shapes=[pltpu.VMEM((2, c * sel, 2 * d), F32), pltpu.SemaphoreType.DMA((2,))],
        compiler_params=_cparams("arbitrary"),
        name="peer",
    )(idx, idx, x2d, gates, g.reshape(1, d), b.reshape(1, d), table)


def _block_diag(pool_w):
    g, c, _ = pool_w.shape
    out = jnp.zeros((g * c, g * c), pool_w.dtype)
    for k in range(g):
        out = out.at[k * c:(k + 1) * c, k * c:(k + 1) * c].set(pool_w[k])
    return out


def kernel(x, w_in, conv_a_w, conv_a_b, norm_a_g, norm_a_b, pool_w, pool_scale, conv_c_w, w_out,
           ln1_g, ln1_b, peer_wq, peer_keys, peer_u, peer_v, ln2_g, ln2_b):
    depth = w_in.shape[0]
    b, s, d = x.shape
    t = b * s
    alpha = float((2 * depth) ** 0.25)
    w = conv_a_w.shape[2]
    tq = min(256, s)
    ts = min(512, s)
    tm = min(512, t)
    row = lax.broadcasted_iota(I32, (tq, tq), 0)
    col = lax.broadcasted_iota(I32, (tq, tq), 1)
    later = (row > col).astype(BF16)
    xf = x.reshape(t, d)
    for l in range(depth):
        proj = in_proj(xf, w_in[l].astype(BF16), tm)
        proj3 = proj.reshape(b, s, proj.shape[1])
        y_abc = local_mix(proj3, conv_a_w[l], conv_a_b[l], norm_a_g[l], norm_a_b[l],
                          _block_diag(pool_w[l]).astype(BF16), pool_scale[l], conv_c_w[l], ts)
        y_d = attention(proj3, later, tq, (6 * w) // LANES)
        x1 = out_ln(y_abc.reshape(t, 3 * w), y_d.reshape(t, -1), xf, w_out[l].astype(BF16),
                    ln1_g[l], ln1_b[l], alpha, tm)
        keys = peer_keys[l].reshape(2 * PEER_HEADS, PEER_NKEYS, -1).astype(BF16)
        idx, gates = route(x1, peer_wq[l].T.astype(BF16), keys, min(256, t))
        table = jnp.concatenate([peer_u[l], peer_v[l]], axis=1)
        xf = peer(x1, idx, gates, table, ln2_g[l], ln2_b[l], alpha, 16)
    return xf.reshape(b, s, d)
```

```python
import functools

import jax
import jax.numpy as jnp
from jax import lax
from jax.experimental import pallas as pl
from jax.experimental.pallas import tpu as pltpu

F32 = jnp.float32
BF16 = jnp.bfloat16
I32 = jnp.int32

LN_EPS = 1e-5
LANES = 128
CONF_WIDTH = 31
POOL_WINDOWS = (2, 4, 8, 16)
SCONV_WIDTH = 3
HEAD_DIM = 64
PEER_HEADS = 8
PEER_NKEYS = 128
PEER_TOPK = 16
HALO = 32
ATTN_UNDERFLOW = -100.0
VMEM_LIMIT = 56 * 1024 * 1024


def _cparams(*sem):
    return pltpu.CompilerParams(dimension_semantics=sem, vmem_limit_bytes=VMEM_LIMIT)


def _layer_norm(h, g, b):
    mu = jnp.mean(h, axis=-1, keepdims=True)
    hc = h - mu
    var = jnp.mean(hc * hc, axis=-1, keepdims=True)
    return hc * lax.rsqrt(var + LN_EPS) * g + b


def _dot_nt(a, b):
    return lax.dot_general(a, b, (((1,), (1,)), ((), ())), preferred_element_type=F32)


def _in_proj_kernel(x_ref, w_ref, o_ref):
    o_ref[...] = jnp.dot(x_ref[...].astype(BF16), w_ref[...], preferred_element_type=F32)


def in_proj(x2d, w_bf16, tm):
    t, d = x2d.shape
    n = w_bf16.shape[1]
    return pl.pallas_call(
        _in_proj_kernel,
        grid=(t // tm,),
        in_specs=[pl.BlockSpec((tm, d), lambda i: (i, 0)),
                  pl.BlockSpec((d, n), lambda i: (0, 0))],
        out_specs=pl.BlockSpec((tm, n), lambda i: (i, 0)),
        out_shape=jax.ShapeDtypeStruct((t, n), F32),
        compiler_params=_cparams("parallel"),
        name="in_proj",
    )(x2d, w_bf16)


def _local_mix_kernel(p_ref, caw_ref, cab_ref, nag_ref, nab_ref, pw_ref, ps_ref, ccw_ref,
                      o_ref, hist, *, ts, w):
    s = pl.program_id(1)

    @pl.when(s == 0)
    def _():
        hist[0:HALO, :] = jnp.zeros((HALO, 3 * w), F32)

    a_val = p_ref[:, 0:w]
    a_gate = p_ref[:, w:2 * w]
    c_h = p_ref[:, 3 * w:4 * w]
    c_gate_b = p_ref[:, 4 * w:5 * w]
    c_gate_c = p_ref[:, 5 * w:6 * w]
    hist[HALO:HALO + ts, 0:w] = a_val * jax.nn.sigmoid(a_gate)
    hist[HALO:HALO + ts, w:2 * w] = p_ref[:, 2 * w:3 * w]
    hist[HALO:HALO + ts, 2 * w:3 * w] = c_gate_c * c_h

    def past(col, back):
        return hist[HALO - back:HALO - back + ts, col * w:(col + 1) * w]

    acc = jnp.zeros((ts, w), F32)
    for k in range(CONF_WIDTH):
        acc = acc + caw_ref[k:k + 1, :] * past(0, CONF_WIDTH - 1 - k)
    h = _layer_norm(acc + cab_ref[...], nag_ref[...], nab_ref[...])
    o_ref[:, 0:w] = h * jax.nn.sigmoid(h)

    t_pos = s * ts + lax.broadcasted_iota(I32, (ts, 1), 0)
    chan = lax.broadcasted_iota(I32, (ts, w), 1)
    group = w // len(POOL_WINDOWS)
    x0 = past(1, 0)
    run = x0
    pooled = jnp.zeros((ts, w), F32)
    done = 1
    for g, win in enumerate(POOL_WINDOWS):
        for back in range(done, win):
            run = run + past(1, back)
        done = win
        inv = 1.0 / jnp.minimum(t_pos + 1, win).astype(F32)
        in_group = (chan >= g * group) & (chan < (g + 1) * group)
        pooled = jnp.where(in_group, run * inv, pooled)
    pooled = pooled - x0
    mixed = jnp.dot(pooled.astype(BF16), pw_ref[...], preferred_element_type=F32)
    o_ref[:, w:2 * w] = mixed * ps_ref[...]

    conv = jnp.zeros((ts, w), F32)
    for k in range(SCONV_WIDTH):
        conv = conv + ccw_ref[k:k + 1, :] * past(2, SCONV_WIDTH - 1 - k)
    o_ref[:, 2 * w:3 * w] = c_gate_b * conv

    hist[0:HALO, :] = hist[ts:ts + HALO, :]


def local_mix(proj3, conv_a_w, conv_a_b, norm_a_g, norm_a_b, pool_bd, pool_scale, conv_c_w, ts):
    b, s, _ = proj3.shape
    w = conv_a_w.shape[1]
    row = lambda a: a.reshape(1, w)
    full = lambda shape: pl.BlockSpec(shape, lambda bi, si: (0,) * len(shape))
    return pl.pallas_call(
        functools.partial(_local_mix_kernel, ts=ts, w=w),
        grid=(b, s // ts),
        in_specs=[pl.BlockSpec((None, ts, 6 * w), lambda bi, si: (bi, si, 0)),
                  full((CONF_WIDTH, w)), full((1, w)), full((1, w)), full((1, w)),
                  full((w, w)), full((1, w)), full((SCONV_WIDTH, w))],
        out_specs=pl.BlockSpec((None, ts, 3 * w), lambda bi, si: (bi, si, 0)),
        out_shape=jax.ShapeDtypeStruct((b, s, 3 * w), F32),
        scratch_shapes=[pltpu.VMEM((HALO + ts, 3 * w), F32)],
        compiler_params=_cparams("arbitrary", "arbitrary"),
        name="local_mix",
    )(proj3, conv_a_w, row(conv_a_b), row(norm_a_g), row(norm_a_b), pool_bd, row(pool_scale), conv_c_w)


def _attention_kernel(q_ref, k_ref, v_ref, m_ref, o_ref, acc_ref, car_ref, *, tq, scale):
    i = pl.program_id(2)
    q = q_ref[...]
    lane = lax.broadcasted_iota(I32, (tq, LANES), 1)
    q_heads = (jnp.where(lane < HEAD_DIM, q, 0.0).astype(BF16),
               jnp.where(lane >= HEAD_DIM, q, 0.0).astype(BF16))
    acc_ref[...] = jnp.zeros(acc_ref.shape, F32)
    car_ref[...] = jnp.zeros(car_ref.shape, F32)
    row = lax.broadcasted_iota(I32, (tq, tq), 0)
    col = lax.broadcasted_iota(I32, (tq, tq), 1)
    causal = col < row
    later = m_ref[...]

    def visit(j, diagonal):
        start = pl.multiple_of(j * tq, tq)
        kb = k_ref[pl.ds(start, tq), :].astype(BF16)
        vb = v_ref[pl.ds(start, tq), :].astype(BF16)
        for h in range(2):
            z = _dot_nt(q_heads[h], kb) * scale
            softplus = jnp.maximum(z, 0.0) + jnp.log1p(jnp.exp(-jnp.abs(z)))
            log_stay = -softplus
            log_att = z - softplus
            if diagonal:
                log_stay = jnp.where(causal, log_stay, 0.0)
            hi = log_stay.astype(BF16)
            lo = (log_stay - hi.astype(F32)).astype(BF16)
            inside = (jnp.dot(hi, later, preferred_element_type=F32)
                      + jnp.dot(lo, later, preferred_element_type=F32))
            carried = car_ref[h]
            wgt = jnp.exp(log_att + inside + carried)
            if diagonal:
                wgt = jnp.where(causal, wgt, 0.0)
            acc_ref[h] += jnp.dot(wgt.astype(BF16), vb, preferred_element_type=F32)
            car_ref[h] = carried + jnp.sum(log_stay, axis=1, keepdims=True)

    visit(i, True)

    def cond(c):
        j, worst = c
        return (j >= 0) & (worst > ATTN_UNDERFLOW)

    def body(c):
        j, _ = c
        visit(j, False)
        return j - 1, jnp.max(car_ref[...])

    lax.while_loop(cond, body, (i - 1, jnp.max(car_ref[...])))
    o_ref[...] = jnp.where(lane < HEAD_DIM, acc_ref[0], acc_ref[1])


def attention(proj3, later, tq, col0):
    b, s, _ = proj3.shape
    pairs = 2
    scale = HEAD_DIM ** -0.5
    return pl.pallas_call(
        functools.partial(_attention_kernel, tq=tq, scale=scale),
        grid=(b, pairs, s // tq),
        in_specs=[pl.BlockSpec((None, tq, LANES), lambda bi, p, i: (bi, i, col0 + p)),
                  pl.BlockSpec((None, s, LANES), lambda bi, p, i: (bi, 0, col0 + pairs + p)),
                  pl.BlockSpec((None, s, LANES), lambda bi, p, i: (bi, 0, col0 + 2 * pairs + p)),
                  pl.BlockSpec((tq, tq), lambda bi, p, i: (0, 0))],
        out_specs=pl.BlockSpec((None, tq, LANES), lambda bi, p, i: (bi, i, p)),
        out_shape=jax.ShapeDtypeStruct((b, s, pairs * LANES), F32),
        scratch_shapes=[pltpu.VMEM((2, tq, LANES), F32), pltpu.VMEM((2, tq, 1), F32)],
        compiler_params=_cparams("parallel", "parallel", "arbitrary"),
        name="attention",
    )(proj3, proj3, proj3, later)


def _out_ln_kernel(ya_ref, yd_ref, x_ref, w_ref, g_ref, b_ref, o_ref, *, alpha, wa):
    m = jnp.dot(ya_ref[...].astype(BF16), w_ref[0:wa, :], preferred_element_type=F32)
    m = m + jnp.dot(yd_ref[...].astype(BF16), w_ref[wa:, :], preferred_element_type=F32)
    o_ref[...] = _layer_norm(alpha * x_ref[...] + m, g_ref[...], b_ref[...])


def out_ln(y_abc, y_d, x2d, w_out_bf16, g, b, alpha, tm):
    t, d = x2d.shape
    wa, wd = y_abc.shape[1], y_d.shape[1]
    return pl.pallas_call(
        functools.partial(_out_ln_kernel, alpha=alpha, wa=wa),
        grid=(t // tm,),
        in_specs=[pl.BlockSpec((tm, wa), lambda i: (i, 0)),
                  pl.BlockSpec((tm, wd), lambda i: (i, 0)),
                  pl.BlockSpec((tm, d), lambda i: (i, 0)),
                  pl.BlockSpec((wa + wd, d), lambda i: (0, 0)),
                  pl.BlockSpec((1, d), lambda i: (0, 0)),
                  pl.BlockSpec((1, d), lambda i: (0, 0))],
        out_specs=pl.BlockSpec((tm, d), lambda i: (i, 0)),
        out_shape=jax.ShapeDtypeStruct((t, d), F32),
        compiler_params=_cparams("parallel"),
        name="out_ln",
    )(y_abc, y_d, x2d, w_out_bf16, g.reshape(1, d), b.reshape(1, d))


NEVER = 1e9


def _top_k_rows(vals, order, payload, k):
    out_v, out_p = [], []
    for _ in range(k):
        best = jnp.max(vals, axis=0, keepdims=True)
        first = jnp.min(jnp.where(vals == best, order, NEVER), axis=0, keepdims=True)
        hit = order == first
        out_v.append(best)
        if payload is None:
            out_p.append(first)
        else:
            out_p.append(jnp.max(jnp.where(hit, payload, -1.0), axis=0, keepdims=True))
        vals = jnp.where(hit, -jnp.inf, vals)
    return out_v, out_p


def _stack(rows):
    return jnp.concatenate(rows, axis=0)


def _pair_candidates(s0, i0, s1, i1, tm):
    k = PEER_TOPK
    sub = 8
    b_all = lax.broadcasted_iota(I32, (k, tm), 0).astype(F32)
    b_low = lax.broadcasted_iota(I32, (sub, tm), 0).astype(F32)
    s1_low, i1_low = _stack(s1[:sub]), _stack(i1[:sub])
    s0_low, i0_low = _stack(s0[:sub]), _stack(i0[:sub])
    vals, order, ident = [], [], []

    def add(v, o, e, keep=None):
        if keep is not None:
            v = jnp.where(keep, v, -jnp.inf)
            o = jnp.where(keep, o, NEVER)
        vals.append(v)
        order.append(o)
        ident.append(e)

    add(s0[0] + _stack(s1), b_all, i0[0] * PEER_NKEYS + _stack(i1))
    full_rows = 4
    for a in range(1, full_rows + 1):
        add(s0[a] + s1_low, a * k + b_low, i0[a] * PEER_NKEYS + i1_low)
    add(_stack(s0[sub:]) + s1[0], (b_low + sub) * k, _stack(i0[sub:]) * PEER_NKEYS + i1[0])
    rest = b_low > full_rows
    for b in range(2):
        add(s0_low + s1[b], b_low * k + b, i0_low * PEER_NKEYS + i1[b], rest)
    return _stack(vals), _stack(order), _stack(ident)


def _route_kernel(x_ref, wq_ref, keys_ref, idx_ref, gate_ref, *, tm):
    half = PEER_NKEYS
    qt = _dot_nt(wq_ref[...], x_ref[...].astype(BF16)).astype(BF16)
    key_id = lax.broadcasted_iota(I32, (PEER_NKEYS, tm), 0).astype(F32)
    idx_rows, gate_rows = [], []
    for h in range(PEER_HEADS):
        tops = []
        for p in range(2):
            hp = 2 * h + p
            scores = jnp.dot(keys_ref[hp], qt[hp * half:(hp + 1) * half, :], preferred_element_type=F32)
            tops.append(_top_k_rows(scores, key_id, None, PEER_TOPK))
        (s0, i0), (s1, i1) = tops
        cand_s, cand_order, cand_e = _pair_candidates(s0, i0, s1, i1, tm)
        best_s, best_e = _top_k_rows(cand_s, cand_order, cand_e, PEER_TOPK)
        e = jnp.exp(_stack(best_s) - best_s[0])
        gate_rows.append(e / jnp.sum(e, axis=0, keepdims=True))
        idx_rows.extend(best_e)
    idx_ref[...] = _stack(idx_rows).astype(I32).T
    gate_ref[...] = _stack(gate_rows).T


def route(x2d, wq_t_bf16, keys_bf16, tm):
    t, d = x2d.shape
    nq = wq_t_bf16.shape[0]
    sel = PEER_HEADS * PEER_TOPK
    return pl.pallas_call(
        functools.partial(_route_kernel, tm=tm),
        grid=(t // tm,),
        in_specs=[pl.BlockSpec((tm, d), lambda i: (i, 0)),
                  pl.BlockSpec((nq, d), lambda i: (0, 0)),
                  pl.BlockSpec(keys_bf16.shape, lambda i: (0, 0, 0))],
        out_specs=[pl.BlockSpec((tm, sel), lambda i: (i, 0)),
                   pl.BlockSpec((tm, sel), lambda i: (i, 0))],
        out_shape=[jax.ShapeDtypeStruct((t, sel), I32), jax.ShapeDtypeStruct((t, sel), F32)],
        compiler_params=_cparams("parallel"),
        name="route",
    )(x2d, wq_t_bf16, keys_bf16)


PEER_PIECES = 4
MXU_ROWS = 8


def _peer_kernel(idx_now, idx_next, x_ref, gate_ref, g_ref, b_ref, tab_ref, o_ref, rows, sems,
                 *, c, sel, d, alpha):
    i = pl.program_id(0)
    n = pl.num_programs(0)
    piece = d // PEER_PIECES
    per = sel // (2 * PEER_PIECES)

    def start_rows(idx_ref, src_row, half, tk, first):
        for j in range(first, first + per):
            pltpu.make_async_copy(tab_ref.at[idx_ref[src_row, j]], rows.at[half, tk, pl.ds(j, 1), :],
                                  sems.at[half, tk]).start(priority=j % 2)

    def wait_token(half, tk):
        pltpu.make_async_copy(rows.at[1 - half, tk], rows.at[half, tk], sems.at[half, tk]).wait()

    @pl.when(i == 0)
    def _():
        for tk in range(c):
            for first in range(0, sel, per):
                start_rows(idx_now, tk, 0, tk, first)

    def score_piece(tok, xb, q):
        half, tk = divmod(tok, c)
        cols = slice(q * piece, (q + 1) * piece)
        return _dot_nt(xb[:, cols], rows[half, tk, :, cols].astype(BF16))

    def mix_piece(tok, act, q):
        half, tk = divmod(tok, c)
        cols = slice(d + q * piece, d + (q + 1) * piece)
        return jnp.dot(act, rows[half, tk, :, cols].astype(BF16), preferred_element_type=F32)

    groups = 2 * PEER_PIECES
    scored = None
    mixed = None
    for k in range(2 * c + 2):
        tok = k if k < 2 * c else None
        if tok is not None:
            half, tk = divmod(tok, c)
            ahead_ref, ahead_row = (idx_now, tok + c) if half == 0 else (idx_next, tk)
            wait_token(half, tk)
            x = jnp.broadcast_to(x_ref[tok:tok + 1, :], (MXU_ROWS, d))
            xb = x.astype(BF16)
        y_done = None
        if mixed is not None:
            m_tok, m_x, m_f = mixed
            y_done = (m_tok, _layer_norm(alpha * m_x + m_f, g_ref[...], b_ref[...]))
        act = None
        if scored is not None:
            s_tok, s_x, s_score = scored
            act = (jax.nn.gelu(s_score) * gate_ref[s_tok:s_tok + 1, :]).astype(BF16)
        score = jnp.zeros((MXU_ROWS, sel), F32)
        parts = []
        issued = 0
        for batch in range(2):
            for q in range(batch * PEER_PIECES // 2, (batch + 1) * PEER_PIECES // 2):
                if act is not None:
                    parts.append(mix_piece(s_tok, act, q))
                if tok is not None:
                    score = score + score_piece(tok, xb, q)
            if tok is not None:
                for _ in range(groups // 4 if batch == 0 else groups - groups // 4):
                    start_rows(ahead_ref, ahead_row, 1 - half, tk, issued * per)
                    issued += 1
        if y_done is not None:
            o_ref[y_done[0]:y_done[0] + 1, :] = y_done[1][0:1, :]
        mixed = (s_tok, s_x, jnp.concatenate(parts, axis=1)) if act is not None else None
        scored = (tok, x, score) if tok is not None else None

    @pl.when(i == n - 1)
    def _():
        for tk in range(c):
            wait_token(0, tk)


def peer(x2d, idx, gates, table, g, b, alpha, c):
    t, d = x2d.shape
    sel = idx.shape[1]
    n = t // (2 * c)
    return pl.pallas_call(
        functools.partial(_peer_kernel, c=c, sel=sel, d=d, alpha=alpha),
        grid=(n,),
        in_specs=[pl.BlockSpec((2 * c, sel), lambda i: (i, 0), memory_space=pltpu.SMEM),
                  pl.BlockSpec((2 * c, sel), lambda i: (jnp.minimum(i + 1, n - 1), 0), memory_space=pltpu.SMEM),
                  pl.BlockSpec((2 * c, d), lambda i: (i, 0)),
                  pl.BlockSpec((2 * c, sel), lambda i: (i, 0)),
                  pl.BlockSpec((1, d), lambda i: (0, 0)),
                  pl.BlockSpec((1, d), lambda i: (0, 0)),
                  pl.BlockSpec(memory_space=pl.ANY)],
        out_specs=pl.BlockSpec((2 * c, d), lambda i: (i, 0)),
        out_shape=jax.ShapeDtypeStruct((t, d), F32),
        scratch_shapes=[pltpu.VMEM((2, c, sel, 2 * d), F32), pltpu.SemaphoreType.DMA((2, c))],
        compiler_params=_cparams("arbitrary"),
        name="peer",
    )(idx, idx, x2d, gates, g.reshape(1, d), b.reshape(1, d), table)


def _block_diag(pool_w):
    g, c, _ = pool_w.shape
    out = jnp.zeros((g * c, g * c), pool_w.dtype)
    for k in range(g):
        out = out.at[k * c:(k + 1) * c, k * c:(k + 1) * c].set(pool_w[k])
    return out


def kernel(x, w_in, conv_a_w, conv_a_b, norm_a_g, norm_a_b, pool_w, pool_scale, conv_c_w, w_out,
           ln1_g, ln1_b, peer_wq, peer_keys, peer_u, peer_v, ln2_g, ln2_b):
    depth = w_in.shape[0]
    b, s, d = x.shape
    t = b * s
    alpha = float((2 * depth) ** 0.25)
    w = conv_a_w.shape[2]
    tq = min(256, s)
    ts = min(512, s)
    tm = min(512, t)
    row = lax.broadcasted_iota(I32, (tq, tq), 0)
    col = lax.broadcasted_iota(I32, (tq, tq), 1)
    later = (row > col).astype(BF16)
    xf = x.reshape(t, d)
    for l in range(depth):
        proj = in_proj(xf, w_in[l].astype(BF16), tm)
        proj3 = proj.reshape(b, s, proj.shape[1])
        y_abc = local_mix(proj3, conv_a_w[l], conv_a_b[l], norm_a_g[l], norm_a_b[l],
                          _block_diag(pool_w[l]).astype(BF16), pool_scale[l], conv_c_w[l], ts)
        y_d = attention(proj3, later, tq, (6 * w) // LANES)
        x1 = out_ln(y_abc.reshape(t, 3 * w), y_d.reshape(t, -1), xf, w_out[l].astype(BF16),
                    ln1_g[l], ln1_b[l], alpha, tm)
        keys = peer_keys[l].reshape(2 * PEER_HEADS, PEER_NKEYS, -1).astype(BF16)
        idx, gates = route(x1, peer_wq[l].T.astype(BF16), keys, min(256, t))
        table = jnp.concatenate([peer_u[l], peer_v[l]], axis=1)[:, None, :]
        xf = peer(x1, idx, gates, table, ln2_g[l], ln2_b[l], alpha, 8)
    return xf.reshape(b, s, d)
```

```python
import functools

import jax
import jax.numpy as jnp
from jax import lax
from jax.experimental import pallas as pl
from jax.experimental.pallas import tpu as pltpu

F32 = jnp.float32
BF16 = jnp.bfloat16
I32 = jnp.int32

LN_EPS = 1e-5
LANES = 128
CONF_WIDTH = 31
POOL_WINDOWS = (2, 4, 8, 16)
SCONV_WIDTH = 3
HEAD_DIM = 64
PEER_HEADS = 8
PEER_NKEYS = 128
PEER_TOPK = 16
HALO = 32
ATTN_UNDERFLOW = -100.0
VMEM_LIMIT = 56 * 1024 * 1024


def _cparams(*sem):
    return pltpu.CompilerParams(dimension_semantics=sem, vmem_limit_bytes=VMEM_LIMIT)


def _layer_norm(h, g, b):
    mu = jnp.mean(h, axis=-1, keepdims=True)
    hc = h - mu
    var = jnp.mean(hc * hc, axis=-1, keepdims=True)
    return hc * lax.rsqrt(var + LN_EPS) * g + b


def _dot_nt(a, b):
    return lax.dot_general(a, b, (((1,), (1,)), ((), ())), preferred_element_type=F32)


def _in_proj_kernel(x_ref, w_ref, o_ref):
    o_ref[...] = jnp.dot(x_ref[...].astype(BF16), w_ref[...], preferred_element_type=F32)


def in_proj(x2d, w_bf16, tm):
    t, d = x2d.shape
    n = w_bf16.shape[1]
    return pl.pallas_call(
        _in_proj_kernel,
        grid=(t // tm,),
        in_specs=[pl.BlockSpec((tm, d), lambda i: (i, 0)),
                  pl.BlockSpec((d, n), lambda i: (0, 0))],
        out_specs=pl.BlockSpec((tm, n), lambda i: (i, 0)),
        out_shape=jax.ShapeDtypeStruct((t, n), F32),
        compiler_params=_cparams("parallel"),
        name="in_proj",
    )(x2d, w_bf16)


def _local_mix_kernel(p_ref, caw_ref, cab_ref, nag_ref, nab_ref, pw_ref, ps_ref, ccw_ref,
                      o_ref, hist, *, ts, w):
    s = pl.program_id(1)

    @pl.when(s == 0)
    def _():
        hist[0:HALO, :] = jnp.zeros((HALO, 3 * w), F32)

    a_val = p_ref[:, 0:w]
    a_gate = p_ref[:, w:2 * w]
    c_h = p_ref[:, 3 * w:4 * w]
    c_gate_b = p_ref[:, 4 * w:5 * w]
    c_gate_c = p_ref[:, 5 * w:6 * w]
    hist[HALO:HALO + ts, 0:w] = a_val * jax.nn.sigmoid(a_gate)
    hist[HALO:HALO + ts, w:2 * w] = p_ref[:, 2 * w:3 * w]
    hist[HALO:HALO + ts, 2 * w:3 * w] = c_gate_c * c_h

    def past(col, back):
        return hist[HALO - back:HALO - back + ts, col * w:(col + 1) * w]

    acc = jnp.zeros((ts, w), F32)
    for k in range(CONF_WIDTH):
        acc = acc + caw_ref[k:k + 1, :] * past(0, CONF_WIDTH - 1 - k)
    h = _layer_norm(acc + cab_ref[...], nag_ref[...], nab_ref[...])
    o_ref[:, 0:w] = h * jax.nn.sigmoid(h)

    t_pos = s * ts + lax.broadcasted_iota(I32, (ts, 1), 0)
    chan = lax.broadcasted_iota(I32, (ts, w), 1)
    group = w // len(POOL_WINDOWS)
    x0 = past(1, 0)
    run = x0
    pooled = jnp.zeros((ts, w), F32)
    done = 1
    for g, win in enumerate(POOL_WINDOWS):
        for back in range(done, win):
            run = run + past(1, back)
        done = win
        inv = 1.0 / jnp.minimum(t_pos + 1, win).astype(F32)
        in_group = (chan >= g * group) & (chan < (g + 1) * group)
        pooled = jnp.where(in_group, run * inv, pooled)
    pooled = pooled - x0
    mixed = jnp.dot(pooled.astype(BF16), pw_ref[...], preferred_element_type=F32)
    o_ref[:, w:2 * w] = mixed * ps_ref[...]

    conv = jnp.zeros((ts, w), F32)
    for k in range(SCONV_WIDTH):
        conv = conv + ccw_ref[k:k + 1, :] * past(2, SCONV_WIDTH - 1 - k)
    o_ref[:, 2 * w:3 * w] = c_gate_b * conv

    hist[0:HALO, :] = hist[ts:ts + HALO, :]


def local_mix(proj3, conv_a_w, conv_a_b, norm_a_g, norm_a_b, pool_bd, pool_scale, conv_c_w, ts):
    b, s, _ = proj3.shape
    w = conv_a_w.shape[1]
    row = lambda a: a.reshape(1, w)
    full = lambda shape: pl.BlockSpec(shape, lambda bi, si: (0,) * len(shape))
    return pl.pallas_call(
        functools.partial(_local_mix_kernel, ts=ts, w=w),
        grid=(b, s // ts),
        in_specs=[pl.BlockSpec((None, ts, 6 * w), lambda bi, si: (bi, si, 0)),
                  full((CONF_WIDTH, w)), full((1, w)), full((1, w)), full((1, w)),
                  full((w, w)), full((1, w)), full((SCONV_WIDTH, w))],
        out_specs=pl.BlockSpec((None, ts, 3 * w), lambda bi, si: (bi, si, 0)),
        out_shape=jax.ShapeDtypeStruct((b, s, 3 * w), F32),
        scratch_shapes=[pltpu.VMEM((HALO + ts, 3 * w), F32)],
        compiler_params=_cparams("arbitrary", "arbitrary"),
        name="local_mix",
    )(proj3, conv_a_w, row(conv_a_b), row(norm_a_g), row(norm_a_b), pool_bd, row(pool_scale), conv_c_w)


def _attention_kernel(q_ref, k_ref, v_ref, m_ref, o_ref, acc_ref, car_ref, *, tq, scale):
    i = pl.program_id(2)
    q = q_ref[...]
    lane = lax.broadcasted_iota(I32, (tq, LANES), 1)
    q_heads = (jnp.where(lane < HEAD_DIM, q, 0.0).astype(BF16),
               jnp.where(lane >= HEAD_DIM, q, 0.0).astype(BF16))
    acc_ref[...] = jnp.zeros(acc_ref.shape, F32)
    car_ref[...] = jnp.zeros(car_ref.shape, F32)
    row = lax.broadcasted_iota(I32, (tq, tq), 0)
    col = lax.broadcasted_iota(I32, (tq, tq), 1)
    causal = col < row
    later = m_ref[...]

    def visit(j, diagonal):
        start = pl.multiple_of(j * tq, tq)
        kb = k_ref[pl.ds(start, tq), :].astype(BF16)
        vb = v_ref[pl.ds(start, tq), :].astype(BF16)
        for h in range(2):
            z = _dot_nt(q_heads[h], kb) * scale
            softplus = jnp.maximum(z, 0.0) + jnp.log1p(jnp.exp(-jnp.abs(z)))
            log_stay = -softplus
            log_att = z - softplus
            if diagonal:
                log_stay = jnp.where(causal, log_stay, 0.0)
            hi = log_stay.astype(BF16)
            lo = (log_stay - hi.astype(F32)).astype(BF16)
            inside = (jnp.dot(hi, later, preferred_element_type=F32)
                      + jnp.dot(lo, later, preferred_element_type=F32))
            carried = car_ref[h]
            wgt = jnp.exp(log_att + inside + carried)
            if diagonal:
                wgt = jnp.where(causal, wgt, 0.0)
            acc_ref[h] += jnp.dot(wgt.astype(BF16), vb, preferred_element_type=F32)
            car_ref[h] = carried + jnp.sum(log_stay, axis=1, keepdims=True)

    visit(i, True)

    def cond(c):
        j, worst = c
        return (j >= 0) & (worst > ATTN_UNDERFLOW)

    def body(c):
        j, _ = c
        visit(j, False)
        return j - 1, jnp.max(car_ref[...])

    lax.while_loop(cond, body, (i - 1, jnp.max(car_ref[...])))
    o_ref[...] = jnp.where(lane < HEAD_DIM, acc_ref[0], acc_ref[1])


def attention(proj3, later, tq, col0):
    b, s, _ = proj3.shape
    pairs = 2
    scale = HEAD_DIM ** -0.5
    return pl.pallas_call(
        functools.partial(_attention_kernel, tq=tq, scale=scale),
        grid=(b, pairs, s // tq),
        in_specs=[pl.BlockSpec((None, tq, LANES), lambda bi, p, i: (bi, i, col0 + p)),
                  pl.BlockSpec((None, s, LANES), lambda bi, p, i: (bi, 0, col0 + pairs + p)),
                  pl.BlockSpec((None, s, LANES), lambda bi, p, i: (bi, 0, col0 + 2 * pairs + p)),
                  pl.BlockSpec((tq, tq), lambda bi, p, i: (0, 0))],
        out_specs=pl.BlockSpec((None, tq, LANES), lambda bi, p, i: (bi, i, p)),
        out_shape=jax.ShapeDtypeStruct((b, s, pairs * LANES), F32),
        scratch_shapes=[pltpu.VMEM((2, tq, LANES), F32), pltpu.VMEM((2, tq, 1), F32)],
        compiler_params=_cparams("parallel", "parallel", "arbitrary"),
        name="attention",
    )(proj3, proj3, proj3, later)


def _out_ln_kernel(ya_ref, yd_ref, x_ref, w_ref, g_ref, b_ref, o_ref, *, alpha, wa):
    m = jnp.dot(ya_ref[...].astype(BF16), w_ref[0:wa, :], preferred_element_type=F32)
    m = m + jnp.dot(yd_ref[...].astype(BF16), w_ref[wa:, :], preferred_element_type=F32)
    o_ref[...] = _layer_norm(alpha * x_ref[...] + m, g_ref[...], b_ref[...])


def out_ln(y_abc, y_d, x2d, w_out_bf16, g, b, alpha, tm):
    t, d = x2d.shape
    wa, wd = y_abc.shape[1], y_d.shape[1]
    return pl.pallas_call(
        functools.partial(_out_ln_kernel, alpha=alpha, wa=wa),
        grid=(t // tm,),
        in_specs=[pl.BlockSpec((tm, wa), lambda i: (i, 0)),
                  pl.BlockSpec((tm, wd), lambda i: (i, 0)),
                  pl.BlockSpec((tm, d), lambda i: (i, 0)),
                  pl.BlockSpec((wa + wd, d), lambda i: (0, 0)),
                  pl.BlockSpec((1, d), lambda i: (0, 0)),
                  pl.BlockSpec((1, d), lambda i: (0, 0))],
        out_specs=pl.BlockSpec((tm, d), lambda i: (i, 0)),
        out_shape=jax.ShapeDtypeStruct((t, d), F32),
        compiler_params=_cparams("parallel"),
        name="out_ln",
    )(y_abc, y_d, x2d, w_out_bf16, g.reshape(1, d), b.reshape(1, d))


NEVER = 1e9


def _top_k_rows(vals, order, payload, k):
    out_v, out_p = [], []
    for _ in range(k):
        best = jnp.max(vals, axis=0, keepdims=True)
        first = jnp.min(jnp.where(vals == best, order, NEVER), axis=0, keepdims=True)
        hit = order == first
        out_v.append(best)
        if payload is None:
            out_p.append(first)
        else:
            out_p.append(jnp.max(jnp.where(hit, payload, -1.0), axis=0, keepdims=True))
        vals = jnp.where(hit, -jnp.inf, vals)
    return out_v, out_p


def _stack(rows):
    return jnp.concatenate(rows, axis=0)


def _pair_candidates(s0, i0, s1, i1, tm):
    k = PEER_TOPK
    sub = 8
    b_all = lax.broadcasted_iota(I32, (k, tm), 0).astype(F32)
    b_low = lax.broadcasted_iota(I32, (sub, tm), 0).astype(F32)
    s1_low, i1_low = _stack(s1[:sub]), _stack(i1[:sub])
    s0_low, i0_low = _stack(s0[:sub]), _stack(i0[:sub])
    vals, order, ident = [], [], []

    def add(v, o, e, keep=None):
        if keep is not None:
            v = jnp.where(keep, v, -jnp.inf)
            o = jnp.where(keep, o, NEVER)
        vals.append(v)
        order.append(o)
        ident.append(e)

    add(s0[0] + _stack(s1), b_all, i0[0] * PEER_NKEYS + _stack(i1))
    full_rows = 4
    for a in range(1, full_rows + 1):
        add(s0[a] + s1_low, a * k + b_low, i0[a] * PEER_NKEYS + i1_low)
    add(_stack(s0[sub:]) + s1[0], (b_low + sub) * k, _stack(i0[sub:]) * PEER_NKEYS + i1[0])
    rest = b_low > full_rows
    for b in range(2):
        add(s0_low + s1[b], b_low * k + b, i0_low * PEER_NKEYS + i1[b], rest)
    return _stack(vals), _stack(order), _stack(ident)


def _route_kernel(x_ref, wq_ref, keys_ref, idx_ref, gate_ref, *, tm):
    half = PEER_NKEYS
    qt = _dot_nt(wq_ref[...], x_ref[...].astype(BF16)).astype(BF16)
    key_id = lax.broadcasted_iota(I32, (PEER_NKEYS, tm), 0).astype(F32)
    idx_rows, gate_rows = [], []
    for h in range(PEER_HEADS):
        tops = []
        for p in range(2):
            hp = 2 * h + p
            scores = jnp.dot(keys_ref[hp], qt[hp * half:(hp + 1) * half, :], preferred_element_type=F32)
            tops.append(_top_k_rows(scores, key_id, None, PEER_TOPK))
        (s0, i0), (s1, i1) = tops
        cand_s, cand_order, cand_e = _pair_candidates(s0, i0, s1, i1, tm)
        best_s, best_e = _top_k_rows(cand_s, cand_order, cand_e, PEER_TOPK)
        e = jnp.exp(_stack(best_s) - best_s[0])
        gate_rows.append(e / jnp.sum(e, axis=0, keepdims=True))
        idx_rows.extend(best_e)
    idx_ref[...] = _stack(idx_rows).astype(I32).T
    gate_ref[...] = _stack(gate_rows).T


def route(x2d, wq_t_bf16, keys_bf16, tm):
    t, d = x2d.shape
    nq = wq_t_bf16.shape[0]
    sel = PEER_HEADS * PEER_TOPK
    return pl.pallas_call(
        functools.partial(_route_kernel, tm=tm),
        grid=(t // tm,),
        in_specs=[pl.BlockSpec((tm, d), lambda i: (i, 0)),
                  pl.BlockSpec((nq, d), lambda i: (0, 0)),
                  pl.BlockSpec(keys_bf16.shape, lambda i: (0, 0, 0))],
        out_specs=[pl.BlockSpec((tm, sel), lambda i: (i, 0)),
                   pl.BlockSpec((tm, sel), lambda i: (i, 0))],
        out_shape=[jax.ShapeDtypeStruct((t, sel), I32), jax.ShapeDtypeStruct((t, sel), F32)],
        compiler_params=_cparams("parallel"),
        name="route",
    )(x2d, wq_t_bf16, keys_bf16)


PEER_PIECES = 4
MXU_ROWS = 8


def _peer_kernel(idx_now, idx_next, x_ref, gate_ref, g_ref, b_ref, tab_ref, o_ref, rows, sems,
                 *, c, sel, d, alpha):
    i = pl.program_id(0)
    n = pl.num_programs(0)
    piece = d // PEER_PIECES
    per = sel // (2 * PEER_PIECES)

    def start_rows(idx_ref, src_row, half, tk, first):
        for j in range(first, first + per):
            pltpu.make_async_copy(tab_ref.at[idx_ref[src_row, j]], rows.at[half, tk, pl.ds(j, 1), :],
                                  sems.at[half, tk]).start(priority=j % 2)

    def wait_token(half, tk):
        pltpu.make_async_copy(rows.at[1 - half, tk], rows.at[half, tk], sems.at[half, tk]).wait()

    @pl.when(i == 0)
    def _():
        for tk in range(c):
            for first in range(0, sel, per):
                start_rows(idx_now, tk, 0, tk, first)

    def score_piece(tok, xb, q):
        half, tk = divmod(tok, c)
        cols = slice(q * piece, (q + 1) * piece)
        u = lax.bitcast_convert_type(rows[half, tk, :, cols] << 16, F32).astype(BF16)
        return _dot_nt(xb[:, cols], u)

    def mix_piece(tok, act, q):
        half, tk = divmod(tok, c)
        cols = slice(q * piece, (q + 1) * piece)
        v = lax.bitcast_convert_type(rows[half, tk, :, cols] & jnp.uint32(0xFFFF0000), F32).astype(BF16)
        return jnp.dot(act, v, preferred_element_type=F32)

    groups = 2 * PEER_PIECES
    scored = None
    mixed = None
    for k in range(2 * c + 2):
        tok = k if k < 2 * c else None
        if tok is not None:
            half, tk = divmod(tok, c)
            ahead_ref, ahead_row = (idx_now, tok + c) if half == 0 else (idx_next, tk)
            wait_token(half, tk)
            x = jnp.broadcast_to(x_ref[tok:tok + 1, :], (MXU_ROWS, d))
            xb = x.astype(BF16)
        y_done = None
        if mixed is not None:
            m_tok, m_x, m_f = mixed
            y_done = (m_tok, _layer_norm(alpha * m_x + m_f, g_ref[...], b_ref[...]))
        act = None
        if scored is not None:
            s_tok, s_x, s_score = scored
            act = (jax.nn.gelu(s_score) * gate_ref[s_tok:s_tok + 1, :]).astype(BF16)
        score = jnp.zeros((MXU_ROWS, sel), F32)
        parts = []
        issued = 0
        for batch in range(2):
            for q in range(batch * PEER_PIECES // 2, (batch + 1) * PEER_PIECES // 2):
                if act is not None:
                    parts.append(mix_piece(s_tok, act, q))
                if tok is not None:
                    score = score + score_piece(tok, xb, q)
            if tok is not None:
                for _ in range(groups // 4 if batch == 0 else groups - groups // 4):
                    start_rows(ahead_ref, ahead_row, 1 - half, tk, issued * per)
                    issued += 1
        if y_done is not None:
            o_ref[y_done[0]:y_done[0] + 1, :] = y_done[1][0:1, :]
        mixed = (s_tok, s_x, jnp.concatenate(parts, axis=1)) if act is not None else None
        scored = (tok, x, score) if tok is not None else None

    @pl.when(i == n - 1)
    def _():
        for tk in range(c):
            wait_token(0, tk)


def peer(x2d, idx, gates, table, g, b, alpha, c):
    t, d = x2d.shape
    sel = idx.shape[1]
    n = t // (2 * c)
    return pl.pallas_call(
        functools.partial(_peer_kernel, c=c, sel=sel, d=d, alpha=alpha),
        grid=(n,),
        in_specs=[pl.BlockSpec((2 * c, sel), lambda i: (i, 0), memory_space=pltpu.SMEM),
                  pl.BlockSpec((2 * c, sel), lambda i: (jnp.minimum(i + 1, n - 1), 0), memory_space=pltpu.SMEM),
                  pl.BlockSpec((2 * c, d), lambda i: (i, 0)),
                  pl.BlockSpec((2 * c, sel), lambda i: (i, 0)),
                  pl.BlockSpec((1, d), lambda i: (0, 0)),
                  pl.BlockSpec((1, d), lambda i: (0, 0)),
                  pl.BlockSpec(memory_space=pl.ANY)],
        out_specs=pl.BlockSpec((2 * c, d), lambda i: (i, 0)),
        out_shape=jax.ShapeDtypeStruct((t, d), F32),
        scratch_shapes=[pltpu.VMEM((2, c, sel, d), jnp.uint32), pltpu.SemaphoreType.DMA((2, c))],
        compiler_params=_cparams("arbitrary"),
        name="peer",
    )(idx, idx, x2d, gates, g.reshape(1, d), b.reshape(1, d), table)


def _block_diag(pool_w):
    g, c, _ = pool_w.shape
    out = jnp.zeros((g * c, g * c), pool_w.dtype)
    for k in range(g):
        out = out.at[k * c:(k + 1) * c, k * c:(k + 1) * c].set(pool_w[k])
    return out


def _pack_rows(u, v):
    bits = lambda a: lax.bitcast_convert_type(a.astype(BF16), jnp.uint16).astype(jnp.uint32)
    return (bits(u) | (bits(v) << 16))[:, None, :]


def kernel(x, w_in, conv_a_w, conv_a_b, norm_a_g, norm_a_b, pool_w, pool_scale, conv_c_w, w_out,
           ln1_g, ln1_b, peer_wq, peer_keys, peer_u, peer_v, ln2_g, ln2_b):
    depth = w_in.shape[0]
    b, s, d = x.shape
    t = b * s
    alpha = float((2 * depth) ** 0.25)
    w = conv_a_w.shape[2]
    tq = min(256, s)
    ts = min(512, s)
    tm = min(512, t)
    row = lax.broadcasted_iota(I32, (tq, tq), 0)
    col = lax.broadcasted_iota(I32, (tq, tq), 1)
    later = (row > col).astype(BF16)
    xf = x.reshape(t, d)
    for l in range(depth):
        proj = in_proj(xf, w_in[l].astype(BF16), tm)
        proj3 = proj.reshape(b, s, proj.shape[1])
        y_abc = local_mix(proj3, conv_a_w[l], conv_a_b[l], norm_a_g[l], norm_a_b[l],
                          _block_diag(pool_w[l]).astype(BF16), pool_scale[l], conv_c_w[l], ts)
        y_d = attention(proj3, later, tq, (6 * w) // LANES)
        x1 = out_ln(y_abc.reshape(t, 3 * w), y_d.reshape(t, -1), xf, w_out[l].astype(BF16),
                    ln1_g[l], ln1_b[l], alpha, tm)
        keys = peer_keys[l].reshape(2 * PEER_HEADS, PEER_NKEYS, -1).astype(BF16)
        idx, gates = route(x1, peer_wq[l].T.astype(BF16), keys, min(256, t))
        xf = peer(x1, idx, gates, _pack_rows(peer_u[l], peer_v[l]), ln2_g[l], ln2_b[l], alpha, 16)
    return xf.reshape(b, s, d)
```

```python
import functools

import jax
import jax.numpy as jnp
from jax import lax
from jax.experimental import pallas as pl
from jax.experimental.pallas import tpu as pltpu

F32 = jnp.float32
BF16 = jnp.bfloat16
I32 = jnp.int32

LN_EPS = 1e-5
LANES = 128
CONF_WIDTH = 31
POOL_WINDOWS = (2, 4, 8, 16)
SCONV_WIDTH = 3
HEAD_DIM = 64
PEER_HEADS = 8
PEER_NKEYS = 128
PEER_TOPK = 16
HALO = 32
ATTN_UNDERFLOW = -100.0
VMEM_LIMIT = 56 * 1024 * 1024


def _cparams(*sem):
    return pltpu.CompilerParams(dimension_semantics=sem, vmem_limit_bytes=VMEM_LIMIT)


def _layer_norm(h, g, b):
    mu = jnp.mean(h, axis=-1, keepdims=True)
    hc = h - mu
    var = jnp.mean(hc * hc, axis=-1, keepdims=True)
    return hc * lax.rsqrt(var + LN_EPS) * g + b


def _dot_nt(a, b):
    return lax.dot_general(a, b, (((1,), (1,)), ((), ())), preferred_element_type=F32)


def _in_proj_kernel(x_ref, w_ref, o_ref):
    o_ref[...] = jnp.dot(x_ref[...].astype(BF16), w_ref[...], preferred_element_type=F32)


def in_proj(x2d, w_bf16, tm):
    t, d = x2d.shape
    n = w_bf16.shape[1]
    return pl.pallas_call(
        _in_proj_kernel,
        grid=(t // tm,),
        in_specs=[pl.BlockSpec((tm, d), lambda i: (i, 0)),
                  pl.BlockSpec((d, n), lambda i: (0, 0))],
        out_specs=pl.BlockSpec((tm, n), lambda i: (i, 0)),
        out_shape=jax.ShapeDtypeStruct((t, n), F32),
        compiler_params=_cparams("parallel"),
        name="in_proj",
    )(x2d, w_bf16)


def _local_mix_kernel(p_ref, caw_ref, cab_ref, nag_ref, nab_ref, pw_ref, ps_ref, ccw_ref,
                      o_ref, hist, *, ts, w):
    s = pl.program_id(1)

    @pl.when(s == 0)
    def _():
        hist[0:HALO, :] = jnp.zeros((HALO, 3 * w), F32)

    a_val = p_ref[:, 0:w]
    a_gate = p_ref[:, w:2 * w]
    c_h = p_ref[:, 3 * w:4 * w]
    c_gate_b = p_ref[:, 4 * w:5 * w]
    c_gate_c = p_ref[:, 5 * w:6 * w]
    hist[HALO:HALO + ts, 0:w] = a_val * jax.nn.sigmoid(a_gate)
    hist[HALO:HALO + ts, w:2 * w] = p_ref[:, 2 * w:3 * w]
    hist[HALO:HALO + ts, 2 * w:3 * w] = c_gate_c * c_h

    def past(col, back):
        return hist[HALO - back:HALO - back + ts, col * w:(col + 1) * w]

    acc = jnp.zeros((ts, w), F32)
    for k in range(CONF_WIDTH):
        acc = acc + caw_ref[k:k + 1, :] * past(0, CONF_WIDTH - 1 - k)
    h = _layer_norm(acc + cab_ref[...], nag_ref[...], nab_ref[...])
    o_ref[:, 0:w] = h * jax.nn.sigmoid(h)

    t_pos = s * ts + lax.broadcasted_iota(I32, (ts, 1), 0)
    chan = lax.broadcasted_iota(I32, (ts, w), 1)
    group = w // len(POOL_WINDOWS)
    x0 = past(1, 0)
    run = x0
    pooled = jnp.zeros((ts, w), F32)
    done = 1
    for g, win in enumerate(POOL_WINDOWS):
        for back in range(done, win):
            run = run + past(1, back)
        done = win
        inv = 1.0 / jnp.minimum(t_pos + 1, win).astype(F32)
        in_group = (chan >= g * group) & (chan < (g + 1) * group)
        pooled = jnp.where(in_group, run * inv, pooled)
    pooled = pooled - x0
    mixed = jnp.dot(pooled.astype(BF16), pw_ref[...], preferred_element_type=F32)
    o_ref[:, w:2 * w] = mixed * ps_ref[...]

    conv = jnp.zeros((ts, w), F32)
    for k in range(SCONV_WIDTH):
        conv = conv + ccw_ref[k:k + 1, :] * past(2, SCONV_WIDTH - 1 - k)
    o_ref[:, 2 * w:3 * w] = c_gate_b * conv

    hist[0:HALO, :] = hist[ts:ts + HALO, :]


def local_mix(proj3, conv_a_w, conv_a_b, norm_a_g, norm_a_b, pool_bd, pool_scale, conv_c_w, ts):
    b, s, _ = proj3.shape
    w = conv_a_w.shape[1]
    row = lambda a: a.reshape(1, w)
    full = lambda shape: pl.BlockSpec(shape, lambda bi, si: (0,) * len(shape))
    return pl.pallas_call(
        functools.partial(_local_mix_kernel, ts=ts, w=w),
        grid=(b, s // ts),
        in_specs=[pl.BlockSpec((None, ts, 6 * w), lambda bi, si: (bi, si, 0)),
                  full((CONF_WIDTH, w)), full((1, w)), full((1, w)), full((1, w)),
                  full((w, w)), full((1, w)), full((SCONV_WIDTH, w))],
        out_specs=pl.BlockSpec((None, ts, 3 * w), lambda bi, si: (bi, si, 0)),
        out_shape=jax.ShapeDtypeStruct((b, s, 3 * w), F32),
        scratch_shapes=[pltpu.VMEM((HALO + ts, 3 * w), F32)],
        compiler_params=_cparams("arbitrary", "arbitrary"),
        name="local_mix",
    )(proj3, conv_a_w, row(conv_a_b), row(norm_a_g), row(norm_a_b), pool_bd, row(pool_scale), conv_c_w)


def _attention_kernel(q_ref, k_ref, v_ref, m_ref, o_ref, acc_ref, car_ref, *, tq, scale):
    i = pl.program_id(2)
    q = q_ref[...]
    lane = lax.broadcasted_iota(I32, (tq, LANES), 1)
    q_heads = (jnp.where(lane < HEAD_DIM, q, 0.0).astype(BF16),
               jnp.where(lane >= HEAD_DIM, q, 0.0).astype(BF16))
    acc_ref[...] = jnp.zeros(acc_ref.shape, F32)
    car_ref[...] = jnp.zeros(car_ref.shape, F32)
    row = lax.broadcasted_iota(I32, (tq, tq), 0)
    col = lax.broadcasted_iota(I32, (tq, tq), 1)
    causal = col < row
    later = m_ref[...]

    def visit(j, diagonal):
        start = pl.multiple_of(j * tq, tq)
        kb = k_ref[pl.ds(start, tq), :].astype(BF16)
        vb = v_ref[pl.ds(start, tq), :].astype(BF16)
        for h in range(2):
            z = _dot_nt(q_heads[h], kb) * scale
            softplus = jnp.maximum(z, 0.0) + jnp.log1p(jnp.exp(-jnp.abs(z)))
            log_stay = -softplus
            log_att = z - softplus
            if diagonal:
                log_stay = jnp.where(causal, log_stay, 0.0)
            hi = log_stay.astype(BF16)
            lo = (log_stay - hi.astype(F32)).astype(BF16)
            inside = (jnp.dot(hi, later, preferred_element_type=F32)
                      + jnp.dot(lo, later, preferred_element_type=F32))
            carried = car_ref[h]
            wgt = jnp.exp(log_att + inside + carried)
            if diagonal:
                wgt = jnp.where(causal, wgt, 0.0)
            acc_ref[h] += jnp.dot(wgt.astype(BF16), vb, preferred_element_type=F32)
            car_ref[h] = carried + jnp.sum(log_stay, axis=1, keepdims=True)

    visit(i, True)

    def cond(c):
        j, worst = c
        return (j >= 0) & (worst > ATTN_UNDERFLOW)

    def body(c):
        j, _ = c
        visit(j, False)
        return j - 1, jnp.max(car_ref[...])

    lax.while_loop(cond, body, (i - 1, jnp.max(car_ref[...])))
    o_ref[...] = jnp.where(lane < HEAD_DIM, acc_ref[0], acc_ref[1])


def attention(proj3, later, tq, col0):
    b, s, _ = proj3.shape
    pairs = 2
    scale = HEAD_DIM ** -0.5
    return pl.pallas_call(
        functools.partial(_attention_kernel, tq=tq, scale=scale),
        grid=(b, pairs, s // tq),
        in_specs=[pl.BlockSpec((None, tq, LANES), lambda bi, p, i: (bi, i, col0 + p)),
                  pl.BlockSpec((None, s, LANES), lambda bi, p, i: (bi, 0, col0 + pairs + p)),
                  pl.BlockSpec((None, s, LANES), lambda bi, p, i: (bi, 0, col0 + 2 * pairs + p)),
                  pl.BlockSpec((tq, tq), lambda bi, p, i: (0, 0))],
        out_specs=pl.BlockSpec((None, tq, LANES), lambda bi, p, i: (bi, i, p)),
        out_shape=jax.ShapeDtypeStruct((b, s, pairs * LANES), F32),
        scratch_shapes=[pltpu.VMEM((2, tq, LANES), F32), pltpu.VMEM((2, tq, 1), F32)],
        compiler_params=_cparams("parallel", "parallel", "arbitrary"),
        name="attention",
    )(proj3, proj3, proj3, later)


def _out_ln_kernel(ya_ref, yd_ref, x_ref, w_ref, g_ref, b_ref, o_ref, *, alpha, wa):
    m = jnp.dot(ya_ref[...].astype(BF16), w_ref[0:wa, :], preferred_element_type=F32)
    m = m + jnp.dot(yd_ref[...].astype(BF16), w_ref[wa:, :], preferred_element_type=F32)
    o_ref[...] = _layer_norm(alpha * x_ref[...] + m, g_ref[...], b_ref[...])


def out_ln(y_abc, y_d, x2d, w_out_bf16, g, b, alpha, tm):
    t, d = x2d.shape
    wa, wd = y_abc.shape[1], y_d.shape[1]
    return pl.pallas_call(
        functools.partial(_out_ln_kernel, alpha=alpha, wa=wa),
        grid=(t // tm,),
        in_specs=[pl.BlockSpec((tm, wa), lambda i: (i, 0)),
                  pl.BlockSpec((tm, wd), lambda i: (i, 0)),
                  pl.BlockSpec((tm, d), lambda i: (i, 0)),
                  pl.BlockSpec((wa + wd, d), lambda i: (0, 0)),
                  pl.BlockSpec((1, d), lambda i: (0, 0)),
                  pl.BlockSpec((1, d), lambda i: (0, 0))],
        out_specs=pl.BlockSpec((tm, d), lambda i: (i, 0)),
        out_shape=jax.ShapeDtypeStruct((t, d), F32),
        compiler_params=_cparams("parallel"),
        name="out_ln",
    )(y_abc, y_d, x2d, w_out_bf16, g.reshape(1, d), b.reshape(1, d))


NEVER = 1e9


def _top_k_rows(vals, order, payload, k):
    out_v, out_p = [], []
    for _ in range(k):
        best = jnp.max(vals, axis=0, keepdims=True)
        first = jnp.min(jnp.where(vals == best, order, NEVER), axis=0, keepdims=True)
        hit = order == first
        out_v.append(best)
        if payload is None:
            out_p.append(first)
        else:
            out_p.append(jnp.max(jnp.where(hit, payload, -1.0), axis=0, keepdims=True))
        vals = jnp.where(hit, -jnp.inf, vals)
    return out_v, out_p


def _stack(rows):
    return jnp.concatenate(rows, axis=0)


def _pair_candidates(s0, i0, s1, i1, tm):
    k = PEER_TOPK
    sub = 8
    b_all = lax.broadcasted_iota(I32, (k, tm), 0).astype(F32)
    b_low = lax.broadcasted_iota(I32, (sub, tm), 0).astype(F32)
    s1_low, i1_low = _stack(s1[:sub]), _stack(i1[:sub])
    s0_low, i0_low = _stack(s0[:sub]), _stack(i0[:sub])
    vals, order, ident = [], [], []

    def add(v, o, e, keep=None):
        if keep is not None:
            v = jnp.where(keep, v, -jnp.inf)
            o = jnp.where(keep, o, NEVER)
        vals.append(v)
        order.append(o)
        ident.append(e)

    add(s0[0] + _stack(s1), b_all, i0[0] * PEER_NKEYS + _stack(i1))
    full_rows = 4
    for a in range(1, full_rows + 1):
        add(s0[a] + s1_low, a * k + b_low, i0[a] * PEER_NKEYS + i1_low)
    add(_stack(s0[sub:]) + s1[0], (b_low + sub) * k, _stack(i0[sub:]) * PEER_NKEYS + i1[0])
    rest = b_low > full_rows
    for b in range(2):
        add(s0_low + s1[b], b_low * k + b, i0_low * PEER_NKEYS + i1[b], rest)
    return _stack(vals), _stack(order), _stack(ident)


def _route_kernel(x_ref, wq_ref, keys_ref, idx_ref, gate_ref, *, tm):
    half = PEER_NKEYS
    qt = _dot_nt(wq_ref[...], x_ref[...].astype(BF16)).astype(BF16)
    key_id = lax.broadcasted_iota(I32, (PEER_NKEYS, tm), 0).astype(F32)
    idx_rows, gate_rows = [], []
    for h in range(PEER_HEADS):
        tops = []
        for p in range(2):
            hp = 2 * h + p
            scores = jnp.dot(keys_ref[hp], qt[hp * half:(hp + 1) * half, :], preferred_element_type=F32)
            tops.append(_top_k_rows(scores, key_id, None, PEER_TOPK))
        (s0, i0), (s1, i1) = tops
        cand_s, cand_order, cand_e = _pair_candidates(s0, i0, s1, i1, tm)
        best_s, best_e = _top_k_rows(cand_s, cand_order, cand_e, PEER_TOPK)
        e = jnp.exp(_stack(best_s) - best_s[0])
        gate_rows.append(e / jnp.sum(e, axis=0, keepdims=True))
        idx_rows.extend(best_e)
    idx_ref[...] = _stack(idx_rows).astype(I32).T
    gate_ref[...] = _stack(gate_rows).T


def route(x2d, wq_t_bf16, keys_bf16, tm):
    t, d = x2d.shape
    nq = wq_t_bf16.shape[0]
    sel = PEER_HEADS * PEER_TOPK
    return pl.pallas_call(
        functools.partial(_route_kernel, tm=tm),
        grid=(t // tm,),
        in_specs=[pl.BlockSpec((tm, d), lambda i: (i, 0)),
                  pl.BlockSpec((nq, d), lambda i: (0, 0)),
                  pl.BlockSpec(keys_bf16.shape, lambda i: (0, 0, 0))],
        out_specs=[pl.BlockSpec((tm, sel), lambda i: (i, 0)),
                   pl.BlockSpec((tm, sel), lambda i: (i, 0))],
        out_shape=[jax.ShapeDtypeStruct((t, sel), I32), jax.ShapeDtypeStruct((t, sel), F32)],
        compiler_params=_cparams("parallel"),
        name="route",
    )(x2d, wq_t_bf16, keys_bf16)


PEER_PIECES = 4
MXU_ROWS = 8


def _peer_kernel(idx_now, idx_next, x_ref, gate_ref, g_ref, b_ref, tab_ref, o_ref, rows, sems,
                 *, c, sel, d, alpha):
    i = pl.program_id(0)
    n = pl.num_programs(0)
    piece = d // PEER_PIECES
    per = sel // (2 * PEER_PIECES)

    def start_rows(idx_ref, src_row, half, tk, first):
        for j in range(first, first + per):
            pltpu.make_async_copy(tab_ref.at[idx_ref[src_row, j]], rows.at[half, tk, pl.ds(j, 1), :],
                                  sems.at[half, tk]).start(priority=j % 2)

    def wait_token(half, tk):
        pltpu.make_async_copy(rows.at[1 - half, tk], rows.at[half, tk], sems.at[half, tk]).wait()

    @pl.when(i == 0)
    def _():
        for tk in range(c):
            for first in range(0, sel, per):
                start_rows(idx_now, tk, 0, tk, first)

    def score_piece(tok, xb, q):
        half, tk = divmod(tok, c)
        cols = slice(q * piece, (q + 1) * piece)
        u = lax.bitcast_convert_type(rows[half, tk, :, cols] << 16, F32).astype(BF16)
        return _dot_nt(xb[:, cols], u)

    def mix_piece(tok, act, q):
        half, tk = divmod(tok, c)
        cols = slice(q * piece, (q + 1) * piece)
        v = lax.bitcast_convert_type(rows[half, tk, :, cols] & jnp.uint32(0xFFFF0000), F32).astype(BF16)
        return jnp.dot(act, v, preferred_element_type=F32)

    groups = 2 * PEER_PIECES
    scored = None
    mixed = None
    for k in range(2 * c + 2):
        tok = k if k < 2 * c else None
        if tok is not None:
            half, tk = divmod(tok, c)
            ahead_ref, ahead_row = (idx_now, tok + c) if half == 0 else (idx_next, tk)
            wait_token(half, tk)
            x = jnp.broadcast_to(x_ref[tok:tok + 1, :], (MXU_ROWS, d))
            xb = x.astype(BF16)
        y_done = None
        if mixed is not None:
            m_tok, m_x, m_f = mixed
            y_done = (m_tok, _layer_norm(alpha * m_x + m_f, g_ref[...], b_ref[...]))
        act = None
        if scored is not None:
            s_tok, s_x, s_score = scored
            act = (jax.nn.gelu(s_score) * gate_ref[s_tok:s_tok + 1, :]).astype(BF16)
        score = jnp.zeros((MXU_ROWS, sel), F32)
        parts = []
        issued = 0
        for batch in range(2):
            for q in range(batch * PEER_PIECES // 2, (batch + 1) * PEER_PIECES // 2):
                if act is not None:
                    parts.append(mix_piece(s_tok, act, q))
                if tok is not None:
                    score = score + score_piece(tok, xb, q)
            if tok is not None:
                for _ in range(groups // 4 if batch == 0 else groups - groups // 4):
                    start_rows(ahead_ref, ahead_row, 1 - half, tk, issued * per)
                    issued += 1
        if y_done is not None:
            o_ref[y_done[0]:y_done[0] + 1, :] = y_done[1][0:1, :]
        mixed = (s_tok, s_x, jnp.concatenate(parts, axis=1)) if act is not None else None
        scored = (tok, x, score) if tok is not None else None

    @pl.when(i == n - 1)
    def _():
        for tk in range(c):
            wait_token(0, tk)


def peer(x2d, idx, gates, table, g, b, alpha, c):
    t, d = x2d.shape
    sel = idx.shape[1]
    n = t // (2 * c)
    return pl.pallas_call(
        functools.partial(_peer_kernel, c=c, sel=sel, d=d, alpha=alpha),
        grid=(n,),
        in_specs=[pl.BlockSpec((2 * c, sel), lambda i: (i, 0), memory_space=pltpu.SMEM),
                  pl.BlockSpec((2 * c, sel), lambda i: (jnp.minimum(i + 1, n - 1), 0), memory_space=pltpu.SMEM),
                  pl.BlockSpec((2 * c, d), lambda i: (i, 0)),
                  pl.BlockSpec((2 * c, sel), lambda i: (i, 0)),
                  pl.BlockSpec((1, d), lambda i: (0, 0)),
                  pl.BlockSpec((1, d), lambda i: (0, 0)),
                  pl.BlockSpec(memory_space=pl.ANY)],
        out_specs=pl.BlockSpec((2 * c, d), lambda i: (i, 0)),
        out_shape=jax.ShapeDtypeStruct((t, d), F32),
        scratch_shapes=[pltpu.VMEM((2, c, sel, d), jnp.uint32), pltpu.SemaphoreType.DMA((2, c))],
        compiler_params=_cparams("arbitrary"),
        name="peer",
    )(idx, idx, x2d, gates, g.reshape(1, d), b.reshape(1, d), table)


SC_LANES = 16
SC_WORKERS = 32


def _sc_peer_body(x_hbm, idx_hbm, gate_hbm, tab_hbm, f_hbm, x_v, idx_v, gate_v, out_v, rows_v, sems,
                  *, per_worker, d, sel):
    heads = sel // PEER_TOPK
    chunks = d // SC_LANES
    wid = lax.axis_index("s") * 2 + lax.axis_index("c")
    lane = lax.iota(I32, SC_LANES)

    def gather(h, slot):
        return pltpu.make_async_copy(tab_hbm.at[idx_v.at[pl.ds(h * PEER_TOPK, PEER_TOPK)]],
                                     rows_v.at[slot], sems.at[slot])

    def token(t, carry):
        g = wid * per_worker + t
        pltpu.sync_copy(idx_hbm.at[pl.ds(g * sel, sel)], idx_v)
        pltpu.sync_copy(gate_hbm.at[pl.ds(g * sel, sel)], gate_v)
        pltpu.sync_copy(x_hbm.at[pl.ds(g * d, d)], x_v)

        def clear(ci, c2):
            out_v[pl.ds(ci * SC_LANES, SC_LANES)] = jnp.zeros((SC_LANES,), F32)
            return c2
        lax.fori_loop(0, chunks, clear, 0)

        gather(0, 0).start()
        for h in range(heads):
            slot = h % 2
            if h + 1 < heads:
                gather(h + 1, 1 - slot).start()
            gather(h, slot).wait()

            def dots(ci, accs):
                xc = x_v[pl.ds(ci * SC_LANES, SC_LANES)]
                out = []
                for r in range(PEER_TOPK):
                    w = rows_v[slot, r, pl.ds(ci * SC_LANES, SC_LANES)]
                    out.append(accs[r] + lax.bitcast_convert_type(w << 16, F32) * xc)
                return tuple(out)
            accs = lax.fori_loop(0, chunks, dots, tuple(jnp.zeros((SC_LANES,), F32) for _ in range(PEER_TOPK)))
            score = jnp.zeros((SC_LANES,), F32)
            for r in range(PEER_TOPK):
                score = jnp.where(lane == r, jnp.sum(accs[r]), score)
            inner = 0.7978845608028654 * (score + 0.044715 * score * score * score)
            tanh = 1.0 - 2.0 / (jnp.exp(2.0 * inner) + 1.0)
            act = 0.5 * score * (1.0 + tanh) * gate_v[pl.ds(h * PEER_TOPK, PEER_TOPK)]
            weights = [jnp.sum(jnp.where(lane == r, act, 0.0)) for r in range(PEER_TOPK)]

            def mix(ci, c2):
                part = [out_v[pl.ds(ci * SC_LANES, SC_LANES)]] + [jnp.zeros((SC_LANES,), F32)] * 3
                for r in range(PEER_TOPK):
                    w = rows_v[slot, r, pl.ds(ci * SC_LANES, SC_LANES)]
                    part[r % 4] = part[r % 4] + weights[r] * lax.bitcast_convert_type(
                        w & jnp.uint32(0xFFFF0000), F32)
                out_v[pl.ds(ci * SC_LANES, SC_LANES)] = (part[0] + part[1]) + (part[2] + part[3])
                return c2
            lax.fori_loop(0, chunks, mix, 0)
        pltpu.sync_copy(out_v, f_hbm.at[pl.ds(g * d, d)])
        return carry

    lax.fori_loop(0, per_worker, token, 0)


def sc_peer(x_flat, idx_flat, gate_flat, table2d):
    from jax.experimental.pallas import tpu_sc as plsc
    d = table2d.shape[1]
    tokens = x_flat.shape[0] // d
    sel = idx_flat.shape[0] // tokens
    per_worker = tokens // SC_WORKERS
    mesh = plsc.VectorSubcoreMesh(core_axis_name="c", subcore_axis_name="s")
    return pl.kernel(
        functools.partial(_sc_peer_body, per_worker=per_worker, d=d, sel=sel),
        out_type=jax.ShapeDtypeStruct((tokens * d,), F32),
        mesh=mesh,
        scratch_types=[pltpu.VMEM((d,), F32), pltpu.VMEM((sel,), I32), pltpu.VMEM((sel,), F32),
                       pltpu.VMEM((d,), F32), pltpu.VMEM((2, PEER_TOPK, d), jnp.uint32),
                       pltpu.SemaphoreType.DMA((2,))],
        compiler_params=pltpu.CompilerParams(needs_layout_passes=False),
        name="sc_peer",
    )(x_flat, idx_flat, gate_flat, table2d)


def _block_diag(pool_w):
    g, c, _ = pool_w.shape
    out = jnp.zeros((g * c, g * c), pool_w.dtype)
    for k in range(g):
        out = out.at[k * c:(k + 1) * c, k * c:(k + 1) * c].set(pool_w[k])
    return out


def _pack_rows(u, v):
    bits = lambda a: lax.bitcast_convert_type(a.astype(BF16), jnp.uint16).astype(jnp.uint32)
    return bits(u) | (bits(v) << 16)


def _residual_ln_kernel(x_ref, f_ref, g_ref, b_ref, o_ref, *, alpha):
    o_ref[...] = _layer_norm(alpha * x_ref[...] + f_ref[...], g_ref[...], b_ref[...])


def residual_ln(x2d, f2d, g, b, alpha, tm):
    t, d = x2d.shape
    return pl.pallas_call(
        functools.partial(_residual_ln_kernel, alpha=alpha),
        grid=(t // tm,),
        in_specs=[pl.BlockSpec((tm, d), lambda i: (i, 0)),
                  pl.BlockSpec((tm, d), lambda i: (i, 0)),
                  pl.BlockSpec((1, d), lambda i: (0, 0)),
                  pl.BlockSpec((1, d), lambda i: (0, 0))],
        out_specs=pl.BlockSpec((tm, d), lambda i: (i, 0)),
        out_shape=jax.ShapeDtypeStruct((t, d), F32),
        compiler_params=_cparams("parallel"),
        name="residual_ln",
    )(x2d, f2d, g.reshape(1, d), b.reshape(1, d))


def _sc_share(t):
    return (t // 4) // (2 * SC_WORKERS) * (2 * SC_WORKERS)


def kernel(x, w_in, conv_a_w, conv_a_b, norm_a_g, norm_a_b, pool_w, pool_scale, conv_c_w, w_out,
           ln1_g, ln1_b, peer_wq, peer_keys, peer_u, peer_v, ln2_g, ln2_b):
    depth = w_in.shape[0]
    b, s, d = x.shape
    t = b * s
    alpha = float((2 * depth) ** 0.25)
    w = conv_a_w.shape[2]
    tq = min(256, s)
    ts = min(512, s)
    tm = min(512, t)
    row = lax.broadcasted_iota(I32, (tq, tq), 0)
    col = lax.broadcasted_iota(I32, (tq, tq), 1)
    later = (row > col).astype(BF16)
    xf = x.reshape(t, d)
    for l in range(depth):
        proj = in_proj(xf, w_in[l].astype(BF16), tm)
        proj3 = proj.reshape(b, s, proj.shape[1])
        y_abc = local_mix(proj3, conv_a_w[l], conv_a_b[l], norm_a_g[l], norm_a_b[l],
                          _block_diag(pool_w[l]).astype(BF16), pool_scale[l], conv_c_w[l], ts)
        y_d = attention(proj3, later, tq, (6 * w) // LANES)
        x1 = out_ln(y_abc.reshape(t, 3 * w), y_d.reshape(t, -1), xf, w_out[l].astype(BF16),
                    ln1_g[l], ln1_b[l], alpha, tm)
        keys = peer_keys[l].reshape(2 * PEER_HEADS, PEER_NKEYS, -1).astype(BF16)
        idx, gates = route(x1, peer_wq[l].T.astype(BF16), keys, min(256, t))
        words = _pack_rows(peer_u[l], peer_v[l])
        t_tc = t - _sc_share(t)
        f_sc = sc_peer(x1[t_tc:].reshape(-1), idx[t_tc:].reshape(-1), gates[t_tc:].reshape(-1), words)
        x_tc = peer(x1[:t_tc], idx[:t_tc], gates[:t_tc], words[:, None, :], ln2_g[l], ln2_b[l], alpha, 16)
        x_sc = residual_ln(x1[t_tc:], f_sc.reshape(t - t_tc, d), ln2_g[l], ln2_b[l], alpha, tm)
        xf = jnp.concatenate([x_tc, x_sc], axis=0)
    return xf.reshape(b, s, d)
```

```python
import functools

import jax
import jax.numpy as jnp
from jax import lax
from jax.experimental import pallas as pl
from jax.experimental.pallas import tpu as pltpu

F32 = jnp.float32
BF16 = jnp.bfloat16
I32 = jnp.int32

LN_EPS = 1e-5
LANES = 128
CONF_WIDTH = 31
POOL_WINDOWS = (2, 4, 8, 16)
SCONV_WIDTH = 3
HEAD_DIM = 64
PEER_HEADS = 8
PEER_NKEYS = 128
PEER_TOPK = 16
HALO = 32
ATTN_UNDERFLOW = -100.0
VMEM_LIMIT = 56 * 1024 * 1024


def _cparams(*sem):
    return pltpu.CompilerParams(dimension_semantics=sem, vmem_limit_bytes=VMEM_LIMIT)


def _layer_norm(h, g, b):
    mu = jnp.mean(h, axis=-1, keepdims=True)
    hc = h - mu
    var = jnp.mean(hc * hc, axis=-1, keepdims=True)
    return hc * lax.rsqrt(var + LN_EPS) * g + b


def _dot_nt(a, b):
    return lax.dot_general(a, b, (((1,), (1,)), ((), ())), preferred_element_type=F32)


def _in_proj_kernel(x_ref, w_ref, o_ref):
    o_ref[...] = jnp.dot(x_ref[...].astype(BF16), w_ref[...], preferred_element_type=F32)


def in_proj(x2d, w_bf16, tm):
    t, d = x2d.shape
    n = w_bf16.shape[1]
    return pl.pallas_call(
        _in_proj_kernel,
        grid=(t // tm,),
        in_specs=[pl.BlockSpec((tm, d), lambda i: (i, 0)),
                  pl.BlockSpec((d, n), lambda i: (0, 0))],
        out_specs=pl.BlockSpec((tm, n), lambda i: (i, 0)),
        out_shape=jax.ShapeDtypeStruct((t, n), F32),
        compiler_params=_cparams("parallel"),
        name="in_proj",
    )(x2d, w_bf16)


def _local_mix_kernel(p_ref, caw_ref, cab_ref, nag_ref, nab_ref, pw_ref, ps_ref, ccw_ref,
                      o_ref, hist, *, ts, w):
    s = pl.program_id(1)

    @pl.when(s == 0)
    def _():
        hist[0:HALO, :] = jnp.zeros((HALO, 3 * w), F32)

    a_val = p_ref[:, 0:w]
    a_gate = p_ref[:, w:2 * w]
    c_h = p_ref[:, 3 * w:4 * w]
    c_gate_b = p_ref[:, 4 * w:5 * w]
    c_gate_c = p_ref[:, 5 * w:6 * w]
    hist[HALO:HALO + ts, 0:w] = a_val * jax.nn.sigmoid(a_gate)
    hist[HALO:HALO + ts, w:2 * w] = p_ref[:, 2 * w:3 * w]
    hist[HALO:HALO + ts, 2 * w:3 * w] = c_gate_c * c_h

    def past(col, back):
        return hist[HALO - back:HALO - back + ts, col * w:(col + 1) * w]

    acc = jnp.zeros((ts, w), F32)
    for k in range(CONF_WIDTH):
        acc = acc + caw_ref[k:k + 1, :] * past(0, CONF_WIDTH - 1 - k)
    h = _layer_norm(acc + cab_ref[...], nag_ref[...], nab_ref[...])
    o_ref[:, 0:w] = h * jax.nn.sigmoid(h)

    t_pos = s * ts + lax.broadcasted_iota(I32, (ts, 1), 0)
    chan = lax.broadcasted_iota(I32, (ts, w), 1)
    group = w // len(POOL_WINDOWS)
    x0 = past(1, 0)
    run = x0
    pooled = jnp.zeros((ts, w), F32)
    done = 1
    for g, win in enumerate(POOL_WINDOWS):
        for back in range(done, win):
            run = run + past(1, back)
        done = win
        inv = 1.0 / jnp.minimum(t_pos + 1, win).astype(F32)
        in_group = (chan >= g * group) & (chan < (g + 1) * group)
        pooled = jnp.where(in_group, run * inv, pooled)
    pooled = pooled - x0
    mixed = jnp.dot(pooled.astype(BF16), pw_ref[...], preferred_element_type=F32)
    o_ref[:, w:2 * w] = mixed * ps_ref[...]

    conv = jnp.zeros((ts, w), F32)
    for k in range(SCONV_WIDTH):
        conv = conv + ccw_ref[k:k + 1, :] * past(2, SCONV_WIDTH - 1 - k)
    o_ref[:, 2 * w:3 * w] = c_gate_b * conv

    hist[0:HALO, :] = hist[ts:ts + HALO, :]


def local_mix(proj3, conv_a_w, conv_a_b, norm_a_g, norm_a_b, pool_bd, pool_scale, conv_c_w, ts):
    b, s, _ = proj3.shape
    w = conv_a_w.shape[1]
    row = lambda a: a.reshape(1, w)
    full = lambda shape: pl.BlockSpec(shape, lambda bi, si: (0,) * len(shape))
    return pl.pallas_call(
        functools.partial(_local_mix_kernel, ts=ts, w=w),
        grid=(b, s // ts),
        in_specs=[pl.BlockSpec((None, ts, 6 * w), lambda bi, si: (bi, si, 0)),
                  full((CONF_WIDTH, w)), full((1, w)), full((1, w)), full((1, w)),
                  full((w, w)), full((1, w)), full((SCONV_WIDTH, w))],
        out_specs=pl.BlockSpec((None, ts, 3 * w), lambda bi, si: (bi, si, 0)),
        out_shape=jax.ShapeDtypeStruct((b, s, 3 * w), F32),
        scratch_shapes=[pltpu.VMEM((HALO + ts, 3 * w), F32)],
        compiler_params=_cparams("arbitrary", "arbitrary"),
        name="local_mix",
    )(proj3, conv_a_w, row(conv_a_b), row(norm_a_g), row(norm_a_b), pool_bd, row(pool_scale), conv_c_w)


def _attention_kernel(q_ref, k_ref, v_ref, m_ref, o_ref, acc_ref, car_ref, *, tq, scale):
    i = pl.program_id(2)
    q = q_ref[...]
    lane = lax.broadcasted_iota(I32, (tq, LANES), 1)
    q_heads = (jnp.where(lane < HEAD_DIM, q, 0.0).astype(BF16),
               jnp.where(lane >= HEAD_DIM, q, 0.0).astype(BF16))
    acc_ref[...] = jnp.zeros(acc_ref.shape, F32)
    car_ref[...] = jnp.zeros(car_ref.shape, F32)
    row = lax.broadcasted_iota(I32, (tq, tq), 0)
    col = lax.broadcasted_iota(I32, (tq, tq), 1)
    causal = col < row
    later = m_ref[...]

    def visit(j, diagonal):
        start = pl.multiple_of(j * tq, tq)
        kb = k_ref[pl.ds(start, tq), :].astype(BF16)
        vb = v_ref[pl.ds(start, tq), :].astype(BF16)
        for h in range(2):
            z = _dot_nt(q_heads[h], kb) * scale
            softplus = jnp.maximum(z, 0.0) + jnp.log1p(jnp.exp(-jnp.abs(z)))
            log_stay = -softplus
            log_att = z - softplus
            if diagonal:
                log_stay = jnp.where(causal, log_stay, 0.0)
            hi = log_stay.astype(BF16)
            lo = (log_stay - hi.astype(F32)).astype(BF16)
            inside = (jnp.dot(hi, later, preferred_element_type=F32)
                      + jnp.dot(lo, later, preferred_element_type=F32))
            carried = car_ref[h]
            wgt = jnp.exp(log_att + inside + carried)
            if diagonal:
                wgt = jnp.where(causal, wgt, 0.0)
            acc_ref[h] += jnp.dot(wgt.astype(BF16), vb, preferred_element_type=F32)
            car_ref[h] = carried + jnp.sum(log_stay, axis=1, keepdims=True)

    visit(i, True)

    def cond(c):
        j, worst = c
        return (j >= 0) & (worst > ATTN_UNDERFLOW)

    def body(c):
        j, _ = c
        visit(j, False)
        return j - 1, jnp.max(car_ref[...])

    lax.while_loop(cond, body, (i - 1, jnp.max(car_ref[...])))
    o_ref[...] = jnp.where(lane < HEAD_DIM, acc_ref[0], acc_ref[1])


def attention(proj3, later, tq, col0):
    b, s, _ = proj3.shape
    pairs = 2
    scale = HEAD_DIM ** -0.5
    return pl.pallas_call(
        functools.partial(_attention_kernel, tq=tq, scale=scale),
        grid=(b, pairs, s // tq),
        in_specs=[pl.BlockSpec((None, tq, LANES), lambda bi, p, i: (bi, i, col0 + p)),
                  pl.BlockSpec((None, s, LANES), lambda bi, p, i: (bi, 0, col0 + pairs + p)),
                  pl.BlockSpec((None, s, LANES), lambda bi, p, i: (bi, 0, col0 + 2 * pairs + p)),
                  pl.BlockSpec((tq, tq), lambda bi, p, i: (0, 0))],
        out_specs=pl.BlockSpec((None, tq, LANES), lambda bi, p, i: (bi, i, p)),
        out_shape=jax.ShapeDtypeStruct((b, s, pairs * LANES), F32),
        scratch_shapes=[pltpu.VMEM((2, tq, LANES), F32), pltpu.VMEM((2, tq, 1), F32)],
        compiler_params=_cparams("parallel", "parallel", "arbitrary"),
        name="attention",
    )(proj3, proj3, proj3, later)


def _out_ln_kernel(ya_ref, yd_ref, x_ref, w_ref, g_ref, b_ref, o_ref, *, alpha, wa):
    m = jnp.dot(ya_ref[...].astype(BF16), w_ref[0:wa, :], preferred_element_type=F32)
    m = m + jnp.dot(yd_ref[...].astype(BF16), w_ref[wa:, :], preferred_element_type=F32)
    o_ref[...] = _layer_norm(alpha * x_ref[...] + m, g_ref[...], b_ref[...])


def out_ln(y_abc, y_d, x2d, w_out_bf16, g, b, alpha, tm):
    t, d = x2d.shape
    wa, wd = y_abc.shape[1], y_d.shape[1]
    return pl.pallas_call(
        functools.partial(_out_ln_kernel, alpha=alpha, wa=wa),
        grid=(t // tm,),
        in_specs=[pl.BlockSpec((tm, wa), lambda i: (i, 0)),
                  pl.BlockSpec((tm, wd), lambda i: (i, 0)),
                  pl.BlockSpec((tm, d), lambda i: (i, 0)),
                  pl.BlockSpec((wa + wd, d), lambda i: (0, 0)),
                  pl.BlockSpec((1, d), lambda i: (0, 0)),
                  pl.BlockSpec((1, d), lambda i: (0, 0))],
        out_specs=pl.BlockSpec((tm, d), lambda i: (i, 0)),
        out_shape=jax.ShapeDtypeStruct((t, d), F32),
        compiler_params=_cparams("parallel"),
        name="out_ln",
    )(y_abc, y_d, x2d, w_out_bf16, g.reshape(1, d), b.reshape(1, d))


NEVER = 1e9


def _top_k_rows(vals, order, payload, k):
    out_v, out_p = [], []
    for _ in range(k):
        best = jnp.max(vals, axis=0, keepdims=True)
        first = jnp.min(jnp.where(vals == best, order, NEVER), axis=0, keepdims=True)
        hit = order == first
        out_v.append(best)
        if payload is None:
            out_p.append(first)
        else:
            out_p.append(jnp.max(jnp.where(hit, payload, -1.0), axis=0, keepdims=True))
        vals = jnp.where(hit, -jnp.inf, vals)
    return out_v, out_p


def _stack(rows):
    return jnp.concatenate(rows, axis=0)


def _pair_candidates(s0, i0, s1, i1, tm):
    k = PEER_TOPK
    sub = 8
    b_all = lax.broadcasted_iota(I32, (k, tm), 0).astype(F32)
    b_low = lax.broadcasted_iota(I32, (sub, tm), 0).astype(F32)
    s1_low, i1_low = _stack(s1[:sub]), _stack(i1[:sub])
    s0_low, i0_low = _stack(s0[:sub]), _stack(i0[:sub])
    vals, order, ident = [], [], []

    def add(v, o, e, keep=None):
        if keep is not None:
            v = jnp.where(keep, v, -jnp.inf)
            o = jnp.where(keep, o, NEVER)
        vals.append(v)
        order.append(o)
        ident.append(e)

    add(s0[0] + _stack(s1), b_all, i0[0] * PEER_NKEYS + _stack(i1))
    full_rows = 4
    for a in range(1, full_rows + 1):
        add(s0[a] + s1_low, a * k + b_low, i0[a] * PEER_NKEYS + i1_low)
    add(_stack(s0[sub:]) + s1[0], (b_low + sub) * k, _stack(i0[sub:]) * PEER_NKEYS + i1[0])
    rest = b_low > full_rows
    for b in range(2):
        add(s0_low + s1[b], b_low * k + b, i0_low * PEER_NKEYS + i1[b], rest)
    return _stack(vals), _stack(order), _stack(ident)


def _route_kernel(x_ref, wq_ref, keys_ref, idx_ref, gate_ref, *, tm):
    half = PEER_NKEYS
    qt = _dot_nt(wq_ref[...], x_ref[...].astype(BF16)).astype(BF16)
    key_id = lax.broadcasted_iota(I32, (PEER_NKEYS, tm), 0).astype(F32)
    idx_rows, gate_rows = [], []
    for h in range(PEER_HEADS):
        tops = []
        for p in range(2):
            hp = 2 * h + p
            scores = jnp.dot(keys_ref[hp], qt[hp * half:(hp + 1) * half, :], preferred_element_type=F32)
            tops.append(_top_k_rows(scores, key_id, None, PEER_TOPK))
        (s0, i0), (s1, i1) = tops
        cand_s, cand_order, cand_e = _pair_candidates(s0, i0, s1, i1, tm)
        best_s, best_e = _top_k_rows(cand_s, cand_order, cand_e, PEER_TOPK)
        e = jnp.exp(_stack(best_s) - best_s[0])
        gate_rows.append(e / jnp.sum(e, axis=0, keepdims=True))
        idx_rows.extend(best_e)
    idx_ref[...] = _stack(idx_rows).astype(I32).T
    gate_ref[...] = _stack(gate_rows).T


def route(x2d, wq_t_bf16, keys_bf16, tm):
    t, d = x2d.shape
    nq = wq_t_bf16.shape[0]
    sel = PEER_HEADS * PEER_TOPK
    return pl.pallas_call(
        functools.partial(_route_kernel, tm=tm),
        grid=(t // tm,),
        in_specs=[pl.BlockSpec((tm, d), lambda i: (i, 0)),
                  pl.BlockSpec((nq, d), lambda i: (0, 0)),
                  pl.BlockSpec(keys_bf16.shape, lambda i: (0, 0, 0))],
        out_specs=[pl.BlockSpec((tm, sel), lambda i: (i, 0)),
                   pl.BlockSpec((tm, sel), lambda i: (i, 0))],
        out_shape=[jax.ShapeDtypeStruct((t, sel), I32), jax.ShapeDtypeStruct((t, sel), F32)],
        compiler_params=_cparams("parallel"),
        name="route",
    )(x2d, wq_t_bf16, keys_bf16)


PEER_PIECES = 4
MXU_ROWS = 8


def _peer_kernel(idx_now, idx_next, x_ref, gate_ref, g_ref, b_ref, tab_ref, o_ref, rows, sems,
                 *, c, sel, d, alpha):
    i = pl.program_id(0)
    n = pl.num_programs(0)
    piece = d // PEER_PIECES
    per = sel // (2 * PEER_PIECES)

    def start_rows(idx_ref, src_row, half, tk, first):
        for j in range(first, first + per):
            pltpu.make_async_copy(tab_ref.at[idx_ref[src_row, j]], rows.at[half, tk, pl.ds(j, 1), :],
                                  sems.at[half, tk]).start(priority=j % 2)

    def wait_token(half, tk):
        pltpu.make_async_copy(rows.at[1 - half, tk], rows.at[half, tk], sems.at[half, tk]).wait()

    @pl.when(i == 0)
    def _():
        for tk in range(c):
            for first in range(0, sel, per):
                start_rows(idx_now, tk, 0, tk, first)

    def score_piece(tok, xb, q):
        half, tk = divmod(tok, c)
        cols = slice(q * piece, (q + 1) * piece)
        u = lax.bitcast_convert_type(rows[half, tk, :, cols] << 16, F32).astype(BF16)
        return _dot_nt(xb[:, cols], u)

    def mix_piece(tok, act, q):
        half, tk = divmod(tok, c)
        cols = slice(q * piece, (q + 1) * piece)
        v = lax.bitcast_convert_type(rows[half, tk, :, cols] & jnp.uint32(0xFFFF0000), F32).astype(BF16)
        return jnp.dot(act, v, preferred_element_type=F32)

    groups = 2 * PEER_PIECES
    scored = None
    mixed = None
    for k in range(2 * c + 2):
        tok = k if k < 2 * c else None
        if tok is not None:
            half, tk = divmod(tok, c)
            ahead_ref, ahead_row = (idx_now, tok + c) if half == 0 else (idx_next, tk)
            wait_token(half, tk)
            x = jnp.broadcast_to(x_ref[tok:tok + 1, :], (MXU_ROWS, d))
            xb = x.astype(BF16)
        y_done = None
        if mixed is not None:
            m_tok, m_x, m_f = mixed
            y_done = (m_tok, _layer_norm(alpha * m_x + m_f, g_ref[...], b_ref[...]))
        act = None
        if scored is not None:
            s_tok, s_x, s_score = scored
            act = (jax.nn.gelu(s_score) * gate_ref[s_tok:s_tok + 1, :]).astype(BF16)
        score = jnp.zeros((MXU_ROWS, sel), F32)
        parts = []
        issued = 0
        for batch in range(2):
            for q in range(batch * PEER_PIECES // 2, (batch + 1) * PEER_PIECES // 2):
                if act is not None:
                    parts.append(mix_piece(s_tok, act, q))
                if tok is not None:
                    score = score + score_piece(tok, xb, q)
            if tok is not None:
                for _ in range(groups // 4 if batch == 0 else groups - groups // 4):
                    start_rows(ahead_ref, ahead_row, 1 - half, tk, issued * per)
                    issued += 1
        if y_done is not None:
            o_ref[y_done[0]:y_done[0] + 1, :] = y_done[1][0:1, :]
        mixed = (s_tok, s_x, jnp.concatenate(parts, axis=1)) if act is not None else None
        scored = (tok, x, score) if tok is not None else None

    @pl.when(i == n - 1)
    def _():
        for tk in range(c):
            wait_token(0, tk)


def peer(x2d, idx, gates, table, g, b, alpha, c):
    t, d = x2d.shape
    sel = idx.shape[1]
    n = t // (2 * c)
    return pl.pallas_call(
        functools.partial(_peer_kernel, c=c, sel=sel, d=d, alpha=alpha),
        grid=(n,),
        in_specs=[pl.BlockSpec((2 * c, sel), lambda i: (i, 0), memory_space=pltpu.SMEM),
                  pl.BlockSpec((2 * c, sel), lambda i: (jnp.minimum(i + 1, n - 1), 0), memory_space=pltpu.SMEM),
                  pl.BlockSpec((2 * c, d), lambda i: (i, 0)),
                  pl.BlockSpec((2 * c, sel), lambda i: (i, 0)),
                  pl.BlockSpec((1, d), lambda i: (0, 0)),
                  pl.BlockSpec((1, d), lambda i: (0, 0)),
                  pl.BlockSpec(memory_space=pl.ANY)],
        out_specs=pl.BlockSpec((2 * c, d), lambda i: (i, 0)),
        out_shape=jax.ShapeDtypeStruct((t, d), F32),
        scratch_shapes=[pltpu.VMEM((2, c, sel, d), jnp.uint32), pltpu.SemaphoreType.DMA((2, c))],
        compiler_params=_cparams("arbitrary"),
        name="peer",
    )(idx, idx, x2d, gates, g.reshape(1, d), b.reshape(1, d), table)


SC_LANES = 16
SC_WORKERS = 32


def _sc_peer_body(x_hbm, idx_hbm, gate_hbm, tab_hbm, f_hbm, x_v, idx_v, gate_v, out_v, rows_v, sems,
                  *, per_worker, d, sel):
    heads = sel // PEER_TOPK
    chunks = d // SC_LANES
    wid = lax.axis_index("s") * 2 + lax.axis_index("c")
    lane = lax.iota(I32, SC_LANES)

    def gather(h, slot):
        return pltpu.make_async_copy(tab_hbm.at[idx_v.at[pl.ds(h * PEER_TOPK, PEER_TOPK)]],
                                     rows_v.at[slot], sems.at[slot])

    def token(t, carry):
        g = wid * per_worker + t
        pltpu.sync_copy(idx_hbm.at[pl.ds(g * sel, sel)], idx_v)
        pltpu.sync_copy(gate_hbm.at[pl.ds(g * sel, sel)], gate_v)
        pltpu.sync_copy(x_hbm.at[pl.ds(g * d, d)], x_v)

        def clear(ci, c2):
            out_v[pl.ds(ci * SC_LANES, SC_LANES)] = jnp.zeros((SC_LANES,), F32)
            return c2
        lax.fori_loop(0, chunks, clear, 0)

        gather(0, 0).start()
        for h in range(heads):
            slot = h % 2
            if h + 1 < heads:
                gather(h + 1, 1 - slot).start()
            gather(h, slot).wait()

            def dots(ci, accs):
                xc = x_v[pl.ds(ci * SC_LANES, SC_LANES)]
                out = []
                for r in range(PEER_TOPK):
                    w = rows_v[slot, r, pl.ds(ci * SC_LANES, SC_LANES)]
                    out.append(accs[r] + lax.bitcast_convert_type(w << 16, F32) * xc)
                return tuple(out)
            accs = lax.fori_loop(0, chunks, dots, tuple(jnp.zeros((SC_LANES,), F32) for _ in range(PEER_TOPK)))
            score = jnp.zeros((SC_LANES,), F32)
            for r in range(PEER_TOPK):
                score = jnp.where(lane == r, jnp.sum(accs[r]), score)
            inner = 0.7978845608028654 * (score + 0.044715 * score * score * score)
            tanh = 1.0 - 2.0 / (jnp.exp(2.0 * inner) + 1.0)
            act = 0.5 * score * (1.0 + tanh) * gate_v[pl.ds(h * PEER_TOPK, PEER_TOPK)]
            weights = [jnp.sum(jnp.where(lane == r, act, 0.0)) for r in range(PEER_TOPK)]

            def mix(ci, c2):
                part = [out_v[pl.ds(ci * SC_LANES, SC_LANES)]] + [jnp.zeros((SC_LANES,), F32)] * 3
                for r in range(PEER_TOPK):
                    w = rows_v[slot, r, pl.ds(ci * SC_LANES, SC_LANES)]
                    part[r % 4] = part[r % 4] + weights[r] * lax.bitcast_convert_type(
                        w & jnp.uint32(0xFFFF0000), F32)
                out_v[pl.ds(ci * SC_LANES, SC_LANES)] = (part[0] + part[1]) + (part[2] + part[3])
                return c2
            lax.fori_loop(0, chunks, mix, 0)
        pltpu.sync_copy(out_v, f_hbm.at[pl.ds(g * d, d)])
        return carry

    lax.fori_loop(0, per_worker, token, 0)


def sc_peer(x_flat, idx_flat, gate_flat, table2d):
    from jax.experimental.pallas import tpu_sc as plsc
    d = table2d.shape[1]
    tokens = x_flat.shape[0] // d
    sel = idx_flat.shape[0] // tokens
    per_worker = tokens // SC_WORKERS
    mesh = plsc.VectorSubcoreMesh(core_axis_name="c", subcore_axis_name="s")
    return pl.kernel(
        functools.partial(_sc_peer_body, per_worker=per_worker, d=d, sel=sel),
        out_type=jax.ShapeDtypeStruct((tokens * d,), F32),
        mesh=mesh,
        scratch_types=[pltpu.VMEM((d,), F32), pltpu.VMEM((sel,), I32), pltpu.VMEM((sel,), F32),
                       pltpu.VMEM((d,), F32), pltpu.VMEM((2, PEER_TOPK, d), jnp.uint32),
                       pltpu.SemaphoreType.DMA((2,))],
        compiler_params=pltpu.CompilerParams(needs_layout_passes=False),
        name="sc_peer",
    )(x_flat, idx_flat, gate_flat, table2d)


def _block_diag(pool_w):
    g, c, _ = pool_w.shape
    out = jnp.zeros((g * c, g * c), pool_w.dtype)
    for k in range(g):
        out = out.at[k * c:(k + 1) * c, k * c:(k + 1) * c].set(pool_w[k])
    return out


def _pack_rows(u, v):
    bits = lambda a: lax.bitcast_convert_type(a.astype(BF16), jnp.uint16).astype(jnp.uint32)
    return bits(u) | (bits(v) << 16)


def _residual_ln_kernel(x_ref, f_ref, g_ref, b_ref, o_ref, *, alpha):
    o_ref[...] = _layer_norm(alpha * x_ref[...] + f_ref[...], g_ref[...], b_ref[...])


def residual_ln(x2d, f2d, g, b, alpha, tm):
    t, d = x2d.shape
    return pl.pallas_call(
        functools.partial(_residual_ln_kernel, alpha=alpha),
        grid=(t // tm,),
        in_specs=[pl.BlockSpec((tm, d), lambda i: (i, 0)),
                  pl.BlockSpec((tm, d), lambda i: (i, 0)),
                  pl.BlockSpec((1, d), lambda i: (0, 0)),
                  pl.BlockSpec((1, d), lambda i: (0, 0))],
        out_specs=pl.BlockSpec((tm, d), lambda i: (i, 0)),
        out_shape=jax.ShapeDtypeStruct((t, d), F32),
        compiler_params=_cparams("parallel"),
        name="residual_ln",
    )(x2d, f2d, g.reshape(1, d), b.reshape(1, d))


def _sc_share(t):
    return (t * 13 // 32) // (2 * SC_WORKERS) * (2 * SC_WORKERS)


def kernel(x, w_in, conv_a_w, conv_a_b, norm_a_g, norm_a_b, pool_w, pool_scale, conv_c_w, w_out,
           ln1_g, ln1_b, peer_wq, peer_keys, peer_u, peer_v, ln2_g, ln2_b):
    depth = w_in.shape[0]
    b, s, d = x.shape
    t = b * s
    alpha = float((2 * depth) ** 0.25)
    w = conv_a_w.shape[2]
    tq = min(256, s)
    ts = min(512, s)
    tm = min(512, t)
    row = lax.broadcasted_iota(I32, (tq, tq), 0)
    col = lax.broadcasted_iota(I32, (tq, tq), 1)
    later = (row > col).astype(BF16)
    xf = x.reshape(t, d)
    for l in range(depth):
        proj = in_proj(xf, w_in[l].astype(BF16), tm)
        proj3 = proj.reshape(b, s, proj.shape[1])
        y_abc = local_mix(proj3, conv_a_w[l], conv_a_b[l], norm_a_g[l], norm_a_b[l],
                          _block_diag(pool_w[l]).astype(BF16), pool_scale[l], conv_c_w[l], ts)
        y_d = attention(proj3, later, tq, (6 * w) // LANES)
        x1 = out_ln(y_abc.reshape(t, 3 * w), y_d.reshape(t, -1), xf, w_out[l].astype(BF16),
                    ln1_g[l], ln1_b[l], alpha, tm)
        keys = peer_keys[l].reshape(2 * PEER_HEADS, PEER_NKEYS, -1).astype(BF16)
        idx, gates = route(x1, peer_wq[l].T.astype(BF16), keys, min(256, t))
        words = _pack_rows(peer_u[l], peer_v[l])
        t_tc = t - _sc_share(t)
        f_sc = sc_peer(x1[t_tc:].reshape(-1), idx[t_tc:].reshape(-1), gates[t_tc:].reshape(-1), words)
        x_tc = peer(x1[:t_tc], idx[:t_tc], gates[:t_tc], words[:, None, :], ln2_g[l], ln2_b[l], alpha, 16)
        x_sc = residual_ln(x1[t_tc:], f_sc.reshape(t - t_tc, d), ln2_g[l], ln2_b[l], alpha, tm)
        xf = jnp.concatenate([x_tc, x_sc], axis=0)
    return xf.reshape(b, s, d)
```

```python
import functools

import jax
import jax.numpy as jnp
from jax import lax
from jax.experimental import pallas as pl
from jax.experimental.pallas import tpu as pltpu

F32 = jnp.float32
BF16 = jnp.bfloat16
I32 = jnp.int32

LN_EPS = 1e-5
LANES = 128
CONF_WIDTH = 31
POOL_WINDOWS = (2, 4, 8, 16)
SCONV_WIDTH = 3
HEAD_DIM = 64
PEER_HEADS = 8
PEER_NKEYS = 128
PEER_TOPK = 16
HALO = 32
ATTN_UNDERFLOW = -100.0
VMEM_LIMIT = 56 * 1024 * 1024


def _cparams(*sem):
    return pltpu.CompilerParams(dimension_semantics=sem, vmem_limit_bytes=VMEM_LIMIT)


def _layer_norm(h, g, b):
    mu = jnp.mean(h, axis=-1, keepdims=True)
    hc = h - mu
    var = jnp.mean(hc * hc, axis=-1, keepdims=True)
    return hc * lax.rsqrt(var + LN_EPS) * g + b


def _dot_nt(a, b):
    return lax.dot_general(a, b, (((1,), (1,)), ((), ())), preferred_element_type=F32)


def _in_proj_kernel(x_ref, w_ref, o_ref):
    o_ref[...] = jnp.dot(x_ref[...].astype(BF16), w_ref[...], preferred_element_type=F32)


def in_proj(x2d, w_bf16, tm):
    t, d = x2d.shape
    n = w_bf16.shape[1]
    return pl.pallas_call(
        _in_proj_kernel,
        grid=(t // tm,),
        in_specs=[pl.BlockSpec((tm, d), lambda i: (i, 0)),
                  pl.BlockSpec((d, n), lambda i: (0, 0))],
        out_specs=pl.BlockSpec((tm, n), lambda i: (i, 0)),
        out_shape=jax.ShapeDtypeStruct((t, n), F32),
        compiler_params=_cparams("parallel"),
        name="in_proj",
    )(x2d, w_bf16)


def _local_mix_kernel(p_ref, caw_ref, cab_ref, nag_ref, nab_ref, pw_ref, ps_ref, ccw_ref,
                      o_ref, hist, *, ts, w):
    s = pl.program_id(1)

    @pl.when(s == 0)
    def _():
        hist[0:HALO, :] = jnp.zeros((HALO, 3 * w), F32)

    a_val = p_ref[:, 0:w]
    a_gate = p_ref[:, w:2 * w]
    c_h = p_ref[:, 3 * w:4 * w]
    c_gate_b = p_ref[:, 4 * w:5 * w]
    c_gate_c = p_ref[:, 5 * w:6 * w]
    hist[HALO:HALO + ts, 0:w] = a_val * jax.nn.sigmoid(a_gate)
    hist[HALO:HALO + ts, w:2 * w] = p_ref[:, 2 * w:3 * w]
    hist[HALO:HALO + ts, 2 * w:3 * w] = c_gate_c * c_h

    def past(col, back):
        return hist[HALO - back:HALO - back + ts, col * w:(col + 1) * w]

    acc = jnp.zeros((ts, w), F32)
    for k in range(CONF_WIDTH):
        acc = acc + caw_ref[k:k + 1, :] * past(0, CONF_WIDTH - 1 - k)
    h = _layer_norm(acc + cab_ref[...], nag_ref[...], nab_ref[...])
    o_ref[:, 0:w] = h * jax.nn.sigmoid(h)

    t_pos = s * ts + lax.broadcasted_iota(I32, (ts, 1), 0)
    chan = lax.broadcasted_iota(I32, (ts, w), 1)
    group = w // len(POOL_WINDOWS)
    x0 = past(1, 0)
    run = x0
    pooled = jnp.zeros((ts, w), F32)
    done = 1
    for g, win in enumerate(POOL_WINDOWS):
        for back in range(done, win):
            run = run + past(1, back)
        done = win
        inv = 1.0 / jnp.minimum(t_pos + 1, win).astype(F32)
        in_group = (chan >= g * group) & (chan < (g + 1) * group)
        pooled = jnp.where(in_group, run * inv, pooled)
    pooled = pooled - x0
    mixed = jnp.dot(pooled.astype(BF16), pw_ref[...], preferred_element_type=F32)
    o_ref[:, w:2 * w] = mixed * ps_ref[...]

    conv = jnp.zeros((ts, w), F32)
    for k in range(SCONV_WIDTH):
        conv = conv + ccw_ref[k:k + 1, :] * past(2, SCONV_WIDTH - 1 - k)
    o_ref[:, 2 * w:3 * w] = c_gate_b * conv

    hist[0:HALO, :] = hist[ts:ts + HALO, :]


def local_mix(proj3, conv_a_w, conv_a_b, norm_a_g, norm_a_b, pool_bd, pool_scale, conv_c_w, ts):
    b, s, _ = proj3.shape
    w = conv_a_w.shape[1]
    row = lambda a: a.reshape(1, w)
    full = lambda shape: pl.BlockSpec(shape, lambda bi, si: (0,) * len(shape))
    return pl.pallas_call(
        functools.partial(_local_mix_kernel, ts=ts, w=w),
        grid=(b, s // ts),
        in_specs=[pl.BlockSpec((None, ts, 6 * w), lambda bi, si: (bi, si, 0)),
                  full((CONF_WIDTH, w)), full((1, w)), full((1, w)), full((1, w)),
                  full((w, w)), full((1, w)), full((SCONV_WIDTH, w))],
        out_specs=pl.BlockSpec((None, ts, 3 * w), lambda bi, si: (bi, si, 0)),
        out_shape=jax.ShapeDtypeStruct((b, s, 3 * w), F32),
        scratch_shapes=[pltpu.VMEM((HALO + ts, 3 * w), F32)],
        compiler_params=_cparams("arbitrary", "arbitrary"),
        name="local_mix",
    )(proj3, conv_a_w, row(conv_a_b), row(norm_a_g), row(norm_a_b), pool_bd, row(pool_scale), conv_c_w)


def _attention_kernel(q_ref, k_ref, v_ref, m_ref, o_ref, acc_ref, car_ref, *, tq, scale):
    i = pl.program_id(2)
    q = q_ref[...]
    lane = lax.broadcasted_iota(I32, (tq, LANES), 1)
    q_heads = (jnp.where(lane < HEAD_DIM, q, 0.0).astype(BF16),
               jnp.where(lane >= HEAD_DIM, q, 0.0).astype(BF16))
    acc_ref[...] = jnp.zeros(acc_ref.shape, F32)
    car_ref[...] = jnp.zeros(car_ref.shape, F32)
    row = lax.broadcasted_iota(I32, (tq, tq), 0)
    col = lax.broadcasted_iota(I32, (tq, tq), 1)
    causal = col < row
    later = m_ref[...]

    def visit(j, diagonal):
        start = pl.multiple_of(j * tq, tq)
        kb = k_ref[pl.ds(start, tq), :].astype(BF16)
        vb = v_ref[pl.ds(start, tq), :].astype(BF16)
        for h in range(2):
            z = _dot_nt(q_heads[h], kb) * scale
            softplus = jnp.maximum(z, 0.0) + jnp.log1p(jnp.exp(-jnp.abs(z)))
            log_stay = -softplus
            log_att = z - softplus
            if diagonal:
                log_stay = jnp.where(causal, log_stay, 0.0)
            hi = log_stay.astype(BF16)
            lo = (log_stay - hi.astype(F32)).astype(BF16)
            inside = (jnp.dot(hi, later, preferred_element_type=F32)
                      + jnp.dot(lo, later, preferred_element_type=F32))
            carried = car_ref[h]
            wgt = jnp.exp(log_att + inside + carried)
            if diagonal:
                wgt = jnp.where(causal, wgt, 0.0)
            acc_ref[h] += jnp.dot(wgt.astype(BF16), vb, preferred_element_type=F32)
            car_ref[h] = carried + jnp.sum(log_stay, axis=1, keepdims=True)

    visit(i, True)

    def cond(c):
        j, worst = c
        return (j >= 0) & (worst > ATTN_UNDERFLOW)

    def body(c):
        j, _ = c
        visit(j, False)
        return j - 1, jnp.max(car_ref[...])

    lax.while_loop(cond, body, (i - 1, jnp.max(car_ref[...])))
    o_ref[...] = jnp.where(lane < HEAD_DIM, acc_ref[0], acc_ref[1])


def attention(proj3, later, tq, col0):
    b, s, _ = proj3.shape
    pairs = 2
    scale = HEAD_DIM ** -0.5
    return pl.pallas_call(
        functools.partial(_attention_kernel, tq=tq, scale=scale),
        grid=(b, pairs, s // tq),
        in_specs=[pl.BlockSpec((None, tq, LANES), lambda bi, p, i: (bi, i, col0 + p)),
                  pl.BlockSpec((None, s, LANES), lambda bi, p, i: (bi, 0, col0 + pairs + p)),
                  pl.BlockSpec((None, s, LANES), lambda bi, p, i: (bi, 0, col0 + 2 * pairs + p)),
                  pl.BlockSpec((tq, tq), lambda bi, p, i: (0, 0))],
        out_specs=pl.BlockSpec((None, tq, LANES), lambda bi, p, i: (bi, i, p)),
        out_shape=jax.ShapeDtypeStruct((b, s, pairs * LANES), F32),
        scratch_shapes=[pltpu.VMEM((2, tq, LANES), F32), pltpu.VMEM((2, tq, 1), F32)],
        compiler_params=_cparams("parallel", "parallel", "arbitrary"),
        name="attention",
    )(proj3, proj3, proj3, later)


def _out_ln_kernel(ya_ref, yd_ref, x_ref, w_ref, g_ref, b_ref, o_ref, *, alpha, wa):
    m = jnp.dot(ya_ref[...].astype(BF16), w_ref[0:wa, :], preferred_element_type=F32)
    m = m + jnp.dot(yd_ref[...].astype(BF16), w_ref[wa:, :], preferred_element_type=F32)
    o_ref[...] = _layer_norm(alpha * x_ref[...] + m, g_ref[...], b_ref[...])


def out_ln(y_abc, y_d, x2d, w_out_bf16, g, b, alpha, tm):
    t, d = x2d.shape
    wa, wd = y_abc.shape[1], y_d.shape[1]
    return pl.pallas_call(
        functools.partial(_out_ln_kernel, alpha=alpha, wa=wa),
        grid=(t // tm,),
        in_specs=[pl.BlockSpec((tm, wa), lambda i: (i, 0)),
                  pl.BlockSpec((tm, wd), lambda i: (i, 0)),
                  pl.BlockSpec((tm, d), lambda i: (i, 0)),
                  pl.BlockSpec((wa + wd, d), lambda i: (0, 0)),
                  pl.BlockSpec((1, d), lambda i: (0, 0)),
                  pl.BlockSpec((1, d), lambda i: (0, 0))],
        out_specs=pl.BlockSpec((tm, d), lambda i: (i, 0)),
        out_shape=jax.ShapeDtypeStruct((t, d), F32),
        compiler_params=_cparams("parallel"),
        name="out_ln",
    )(y_abc, y_d, x2d, w_out_bf16, g.reshape(1, d), b.reshape(1, d))


NEVER = 1e9


def _top_k_rows(vals, order, payload, k):
    out_v, out_p = [], []
    for _ in range(k):
        best = jnp.max(vals, axis=0, keepdims=True)
        first = jnp.min(jnp.where(vals == best, order, NEVER), axis=0, keepdims=True)
        hit = order == first
        out_v.append(best)
        if payload is None:
            out_p.append(first)
        else:
            out_p.append(jnp.max(jnp.where(hit, payload, -1.0), axis=0, keepdims=True))
        vals = jnp.where(hit, -jnp.inf, vals)
    return out_v, out_p


def _stack(rows):
    return jnp.concatenate(rows, axis=0)


def _pair_candidates(s0, i0, s1, i1, tm):
    k = PEER_TOPK
    sub = 8
    b_all = lax.broadcasted_iota(I32, (k, tm), 0).astype(F32)
    b_low = lax.broadcasted_iota(I32, (sub, tm), 0).astype(F32)
    s1_low, i1_low = _stack(s1[:sub]), _stack(i1[:sub])
    s0_low, i0_low = _stack(s0[:sub]), _stack(i0[:sub])
    vals, order, ident = [], [], []

    def add(v, o, e, keep=None):
        if keep is not None:
            v = jnp.where(keep, v, -jnp.inf)
            o = jnp.where(keep, o, NEVER)
        vals.append(v)
        order.append(o)
        ident.append(e)

    add(s0[0] + _stack(s1), b_all, i0[0] * PEER_NKEYS + _stack(i1))
    full_rows = 4
    for a in range(1, full_rows + 1):
        add(s0[a] + s1_low, a * k + b_low, i0[a] * PEER_NKEYS + i1_low)
    add(_stack(s0[sub:]) + s1[0], (b_low + sub) * k, _stack(i0[sub:]) * PEER_NKEYS + i1[0])
    rest = b_low > full_rows
    for b in range(2):
        add(s0_low + s1[b], b_low * k + b, i0_low * PEER_NKEYS + i1[b], rest)
    return _stack(vals), _stack(order), _stack(ident)


def _route_kernel(x_ref, wq_ref, keys_ref, idx_ref, gate_ref, *, tm):
    half = PEER_NKEYS
    qt = _dot_nt(wq_ref[...], x_ref[...].astype(BF16)).astype(BF16)
    key_id = lax.broadcasted_iota(I32, (PEER_NKEYS, tm), 0).astype(F32)
    idx_rows, gate_rows = [], []
    for h in range(PEER_HEADS):
        tops = []
        for p in range(2):
            hp = 2 * h + p
            scores = jnp.dot(keys_ref[hp], qt[hp * half:(hp + 1) * half, :], preferred_element_type=F32)
            tops.append(_top_k_rows(scores, key_id, None, PEER_TOPK))
        (s0, i0), (s1, i1) = tops
        cand_s, cand_order, cand_e = _pair_candidates(s0, i0, s1, i1, tm)
        best_s, best_e = _top_k_rows(cand_s, cand_order, cand_e, PEER_TOPK)
        e = jnp.exp(_stack(best_s) - best_s[0])
        gate_rows.append(e / jnp.sum(e, axis=0, keepdims=True))
        idx_rows.extend(best_e)
    idx_ref[...] = _stack(idx_rows).astype(I32).T
    gate_ref[...] = _stack(gate_rows).T


def route(x2d, wq_t_bf16, keys_bf16, tm):
    t, d = x2d.shape
    nq = wq_t_bf16.shape[0]
    sel = PEER_HEADS * PEER_TOPK
    return pl.pallas_call(
        functools.partial(_route_kernel, tm=tm),
        grid=(t // tm,),
        in_specs=[pl.BlockSpec((tm, d), lambda i: (i, 0)),
                  pl.BlockSpec((nq, d), lambda i: (0, 0)),
                  pl.BlockSpec(keys_bf16.shape, lambda i: (0, 0, 0))],
        out_specs=[pl.BlockSpec((tm, sel), lambda i: (i, 0)),
                   pl.BlockSpec((tm, sel), lambda i: (i, 0))],
        out_shape=[jax.ShapeDtypeStruct((t, sel), I32), jax.ShapeDtypeStruct((t, sel), F32)],
        compiler_params=_cparams("parallel"),
        name="route",
    )(x2d, wq_t_bf16, keys_bf16)


PEER_PIECES = 4
MXU_ROWS = 8


def _peer_kernel(idx_now, idx_next, x_ref, gate_ref, g_ref, b_ref, tab_ref, o_ref, rows, sems,
                 *, c, sel, d, alpha):
    i = pl.program_id(0)
    n = pl.num_programs(0)
    piece = d // PEER_PIECES
    per = sel // (2 * PEER_PIECES)

    def start_rows(idx_ref, src_row, half, tk, first):
        for j in range(first, first + per):
            pltpu.make_async_copy(tab_ref.at[idx_ref[src_row, j]], rows.at[half, tk, pl.ds(j, 1), :],
                                  sems.at[half, tk]).start(priority=j % 2)

    def wait_token(half, tk):
        pltpu.make_async_copy(rows.at[1 - half, tk], rows.at[half, tk], sems.at[half, tk]).wait()

    @pl.when(i == 0)
    def _():
        for tk in range(c):
            for first in range(0, sel, per):
                start_rows(idx_now, tk, 0, tk, first)

    def score_piece(tok, xb, q):
        half, tk = divmod(tok, c)
        cols = slice(q * piece, (q + 1) * piece)
        u = lax.bitcast_convert_type(rows[half, tk, :, cols] << 16, F32).astype(BF16)
        return _dot_nt(xb[:, cols], u)

    def mix_piece(tok, act, q):
        half, tk = divmod(tok, c)
        cols = slice(q * piece, (q + 1) * piece)
        v = lax.bitcast_convert_type(rows[half, tk, :, cols] & jnp.uint32(0xFFFF0000), F32).astype(BF16)
        return jnp.dot(act, v, preferred_element_type=F32)

    groups = 2 * PEER_PIECES
    scored = None
    mixed = None
    for k in range(2 * c + 2):
        tok = k if k < 2 * c else None
        if tok is not None:
            half, tk = divmod(tok, c)
            ahead_ref, ahead_row = (idx_now, tok + c) if half == 0 else (idx_next, tk)
            wait_token(half, tk)
            x = jnp.broadcast_to(x_ref[tok:tok + 1, :], (MXU_ROWS, d))
            xb = x.astype(BF16)
        y_done = None
        if mixed is not None:
            m_tok, m_x, m_f = mixed
            y_done = (m_tok, _layer_norm(alpha * m_x + m_f, g_ref[...], b_ref[...]))
        act = None
        if scored is not None:
            s_tok, s_x, s_score = scored
            act = (jax.nn.gelu(s_score) * gate_ref[s_tok:s_tok + 1, :]).astype(BF16)
        score = jnp.zeros((MXU_ROWS, sel), F32)
        parts = []
        issued = 0
        for batch in range(2):
            for q in range(batch * PEER_PIECES // 2, (batch + 1) * PEER_PIECES // 2):
                if act is not None:
                    parts.append(mix_piece(s_tok, act, q))
                if tok is not None:
                    score = score + score_piece(tok, xb, q)
            if tok is not None:
                for _ in range(groups // 4 if batch == 0 else groups - groups // 4):
                    start_rows(ahead_ref, ahead_row, 1 - half, tk, issued * per)
                    issued += 1
        if y_done is not None:
            o_ref[y_done[0]:y_done[0] + 1, :] = y_done[1][0:1, :]
        mixed = (s_tok, s_x, jnp.concatenate(parts, axis=1)) if act is not None else None
        scored = (tok, x, score) if tok is not None else None

    @pl.when(i == n - 1)
    def _():
        for tk in range(c):
            wait_token(0, tk)


def peer(x2d, idx, gates, table, g, b, alpha, c):
    t, d = x2d.shape
    sel = idx.shape[1]
    n = t // (2 * c)
    return pl.pallas_call(
        functools.partial(_peer_kernel, c=c, sel=sel, d=d, alpha=alpha),
        grid=(n,),
        in_specs=[pl.BlockSpec((2 * c, sel), lambda i: (i, 0), memory_space=pltpu.SMEM),
                  pl.BlockSpec((2 * c, sel), lambda i: (jnp.minimum(i + 1, n - 1), 0), memory_space=pltpu.SMEM),
                  pl.BlockSpec((2 * c, d), lambda i: (i, 0)),
                  pl.BlockSpec((2 * c, sel), lambda i: (i, 0)),
                  pl.BlockSpec((1, d), lambda i: (0, 0)),
                  pl.BlockSpec((1, d), lambda i: (0, 0)),
                  pl.BlockSpec(memory_space=pl.ANY)],
        out_specs=pl.BlockSpec((2 * c, d), lambda i: (i, 0)),
        out_shape=jax.ShapeDtypeStruct((t, d), F32),
        scratch_shapes=[pltpu.VMEM((2, c, sel, d), jnp.uint32), pltpu.SemaphoreType.DMA((2, c))],
        compiler_params=_cparams("arbitrary"),
        name="peer",
    )(idx, idx, x2d, gates, g.reshape(1, d), b.reshape(1, d), table)


SC_LANES = 16
SC_WORKERS = 32


def _sc_peer_body(x_hbm, idx_hbm, gate_hbm, tab_hbm, f_hbm, x_v, idx_v, gate_v, out_v, rows_v,
                  row_sems, in_sems, out_sems, *, per_worker, d, sel):
    heads = sel // PEER_TOPK
    chunks = d // SC_LANES
    wid = lax.axis_index("s") * 2 + lax.axis_index("c")
    first = wid * per_worker
    last = first + per_worker - 1

    def gather(p, h):
        return pltpu.make_async_copy(tab_hbm.at[idx_v.at[p, pl.ds(h * PEER_TOPK, PEER_TOPK)]],
                                     rows_v.at[h % 2], row_sems.at[h % 2])

    def inputs(g, p):
        return (pltpu.make_async_copy(idx_hbm.at[pl.ds(g * sel, sel)], idx_v.at[p], in_sems.at[p, 0]),
                pltpu.make_async_copy(gate_hbm.at[pl.ds(g * sel, sel)], gate_v.at[p], in_sems.at[p, 1]),
                pltpu.make_async_copy(x_hbm.at[pl.ds(g * d, d)], x_v.at[p], in_sems.at[p, 2]))

    def result(g, p):
        return pltpu.make_async_copy(out_v.at[p], f_hbm.at[pl.ds(g * d, d)], out_sems.at[p])

    def token(g, p, have_older):
        following = jnp.minimum(g + 1, last)
        for copy in inputs(following, 1 - p):
            copy.start()

        @pl.when(have_older)
        def _():
            result(g, p).wait()

        def clear(ci, c2):
            out_v[p, pl.ds(ci * SC_LANES, SC_LANES)] = jnp.zeros((SC_LANES,), F32)
            return c2
        lax.fori_loop(0, chunks, clear, 0)

        for h in range(heads):
            slot = h % 2
            if h + 1 < heads:
                gather(p, h + 1).start()
            else:
                for copy in inputs(following, 1 - p):
                    copy.wait()
                gather(1 - p, 0).start()
            gather(p, h).wait()

            def dots(ci, accs):
                xc = x_v[p, pl.ds(ci * SC_LANES, SC_LANES)]
                out = []
                for r in range(PEER_TOPK):
                    w = rows_v[slot, r, pl.ds(ci * SC_LANES, SC_LANES)]
                    out.append(accs[r] + lax.bitcast_convert_type(w << 16, F32) * xc)
                return tuple(out)
            accs = lax.fori_loop(0, chunks, dots, tuple(jnp.zeros((SC_LANES,), F32) for _ in range(PEER_TOPK)))
            lane = lax.iota(I32, SC_LANES)
            score = jnp.zeros((SC_LANES,), F32)
            for r in range(PEER_TOPK):
                score = jnp.where(lane == r, jnp.sum(accs[r]), score)
            inner = 0.7978845608028654 * (score + 0.044715 * score * score * score)
            tanh = 1.0 - 2.0 / (jnp.exp(2.0 * inner) + 1.0)
            act = 0.5 * score * (1.0 + tanh) * gate_v[p, pl.ds(h * PEER_TOPK, PEER_TOPK)]
            weights = [jnp.sum(jnp.where(lane == r, act, 0.0)) for r in range(PEER_TOPK)]

            def mix(ci, c2):
                part = [out_v[p, pl.ds(ci * SC_LANES, SC_LANES)]] + [jnp.zeros((SC_LANES,), F32)] * 3
                for r in range(PEER_TOPK):
                    w = rows_v[slot, r, pl.ds(ci * SC_LANES, SC_LANES)]
                    part[r % 4] = part[r % 4] + weights[r] * lax.bitcast_convert_type(
                        w & jnp.uint32(0xFFFF0000), F32)
                out_v[p, pl.ds(ci * SC_LANES, SC_LANES)] = (part[0] + part[1]) + (part[2] + part[3])
                return c2
            lax.fori_loop(0, chunks, mix, 0)
        result(g, p).start()

    for copy in inputs(first, 0):
        copy.start()
    for copy in inputs(first, 0):
        copy.wait()
    gather(0, 0).start()

    def pair(k, carry):
        token(first + 2 * k, 0, k > 0)
        token(first + 2 * k + 1, 1, k > 0)
        return carry
    lax.fori_loop(0, per_worker // 2, pair, 0)

    gather(0, 0).wait()
    result(last, 0).wait()
    result(last, 1).wait()


def sc_peer(x_flat, idx_flat, gate_flat, table2d):
    from jax.experimental.pallas import tpu_sc as plsc
    d = table2d.shape[1]
    tokens = x_flat.shape[0] // d
    sel = idx_flat.shape[0] // tokens
    per_worker = tokens // SC_WORKERS
    mesh = plsc.VectorSubcoreMesh(core_axis_name="c", subcore_axis_name="s")
    return pl.kernel(
        functools.partial(_sc_peer_body, per_worker=per_worker, d=d, sel=sel),
        out_type=jax.ShapeDtypeStruct((tokens * d,), F32),
        mesh=mesh,
        scratch_types=[pltpu.VMEM((2, d), F32), pltpu.VMEM((2, sel), I32), pltpu.VMEM((2, sel), F32),
                       pltpu.VMEM((2, d), F32), pltpu.VMEM((2, PEER_TOPK, d), jnp.uint32),
                       pltpu.SemaphoreType.DMA((2,)), pltpu.SemaphoreType.DMA((2, 3)),
                       pltpu.SemaphoreType.DMA((2,))],
        compiler_params=pltpu.CompilerParams(needs_layout_passes=False),
        name="sc_peer",
    )(x_flat, idx_flat, gate_flat, table2d)


def _block_diag(pool_w):
    g, c, _ = pool_w.shape
    out = jnp.zeros((g * c, g * c), pool_w.dtype)
    for k in range(g):
        out = out.at[k * c:(k + 1) * c, k * c:(k + 1) * c].set(pool_w[k])
    return out


def _pack_rows(u, v):
    bits = lambda a: lax.bitcast_convert_type(a.astype(BF16), jnp.uint16).astype(jnp.uint32)
    return bits(u) | (bits(v) << 16)


def _residual_ln_kernel(x_ref, f_ref, g_ref, b_ref, o_ref, *, alpha):
    o_ref[...] = _layer_norm(alpha * x_ref[...] + f_ref[...], g_ref[...], b_ref[...])


def residual_ln(x2d, f2d, g, b, alpha, tm):
    t, d = x2d.shape
    return pl.pallas_call(
        functools.partial(_residual_ln_kernel, alpha=alpha),
        grid=(t // tm,),
        in_specs=[pl.BlockSpec((tm, d), lambda i: (i, 0)),
                  pl.BlockSpec((tm, d), lambda i: (i, 0)),
                  pl.BlockSpec((1, d), lambda i: (0, 0)),
                  pl.BlockSpec((1, d), lambda i: (0, 0))],
        out_specs=pl.BlockSpec((tm, d), lambda i: (i, 0)),
        out_shape=jax.ShapeDtypeStruct((t, d), F32),
        compiler_params=_cparams("parallel"),
        name="residual_ln",
    )(x2d, f2d, g.reshape(1, d), b.reshape(1, d))


def _sc_share(t):
    return (t * 14 // 32) // (2 * SC_WORKERS) * (2 * SC_WORKERS)


def kernel(x, w_in, conv_a_w, conv_a_b, norm_a_g, norm_a_b, pool_w, pool_scale, conv_c_w, w_out,
           ln1_g, ln1_b, peer_wq, peer_keys, peer_u, peer_v, ln2_g, ln2_b):
    depth = w_in.shape[0]
    b, s, d = x.shape
    t = b * s
    alpha = float((2 * depth) ** 0.25)
    w = conv_a_w.shape[2]
    tq = min(256, s)
    ts = min(512, s)
    tm = min(512, t)
    row = lax.broadcasted_iota(I32, (tq, tq), 0)
    col = lax.broadcasted_iota(I32, (tq, tq), 1)
    later = (row > col).astype(BF16)
    xf = x.reshape(t, d)
    for l in range(depth):
        proj = in_proj(xf, w_in[l].astype(BF16), tm)
        proj3 = proj.reshape(b, s, proj.shape[1])
        y_abc = local_mix(proj3, conv_a_w[l], conv_a_b[l], norm_a_g[l], norm_a_b[l],
                          _block_diag(pool_w[l]).astype(BF16), pool_scale[l], conv_c_w[l], ts)
        y_d = attention(proj3, later, tq, (6 * w) // LANES)
        x1 = out_ln(y_abc.reshape(t, 3 * w), y_d.reshape(t, -1), xf, w_out[l].astype(BF16),
                    ln1_g[l], ln1_b[l], alpha, tm)
        keys = peer_keys[l].reshape(2 * PEER_HEADS, PEER_NKEYS, -1).astype(BF16)
        idx, gates = route(x1, peer_wq[l].T.astype(BF16), keys, min(256, t))
        words = _pack_rows(peer_u[l], peer_v[l])
        t_tc = t - _sc_share(t)
        f_sc = sc_peer(x1[t_tc:].reshape(-1), idx[t_tc:].reshape(-1), gates[t_tc:].reshape(-1), words)
        x_tc = peer(x1[:t_tc], idx[:t_tc], gates[:t_tc], words[:, None, :], ln2_g[l], ln2_b[l], alpha, 16)
        x_sc = residual_ln(x1[t_tc:], f_sc.reshape(t - t_tc, d), ln2_g[l], ln2_b[l], alpha, tm)
        xf = jnp.concatenate([x_tc, x_sc], axis=0)
    return xf.reshape(b, s, d)
```

```python
import functools

import jax
import jax.numpy as jnp
from jax import lax
from jax.experimental import pallas as pl
from jax.experimental.pallas import tpu as pltpu

F32 = jnp.float32
BF16 = jnp.bfloat16
I32 = jnp.int32

LN_EPS = 1e-5
LANES = 128
CONF_WIDTH = 31
POOL_WINDOWS = (2, 4, 8, 16)
SCONV_WIDTH = 3
HEAD_DIM = 64
PEER_HEADS = 8
PEER_NKEYS = 128
PEER_TOPK = 16
HALO = 32
ATTN_UNDERFLOW = -100.0
VMEM_LIMIT = 56 * 1024 * 1024


def _cparams(*sem):
    return pltpu.CompilerParams(dimension_semantics=sem, vmem_limit_bytes=VMEM_LIMIT)


def _layer_norm(h, g, b):
    mu = jnp.mean(h, axis=-1, keepdims=True)
    hc = h - mu
    var = jnp.mean(hc * hc, axis=-1, keepdims=True)
    return hc * lax.rsqrt(var + LN_EPS) * g + b


def _dot_nt(a, b):
    return lax.dot_general(a, b, (((1,), (1,)), ((), ())), preferred_element_type=F32)


def _in_proj_kernel(x_ref, w_ref, o_ref):
    o_ref[...] = jnp.dot(x_ref[...].astype(BF16), w_ref[...], preferred_element_type=F32)


def in_proj(x2d, w_bf16, tm):
    t, d = x2d.shape
    n = w_bf16.shape[1]
    return pl.pallas_call(
        _in_proj_kernel,
        grid=(t // tm,),
        in_specs=[pl.BlockSpec((tm, d), lambda i: (i, 0)),
                  pl.BlockSpec((d, n), lambda i: (0, 0))],
        out_specs=pl.BlockSpec((tm, n), lambda i: (i, 0)),
        out_shape=jax.ShapeDtypeStruct((t, n), F32),
        compiler_params=_cparams("parallel"),
        name="in_proj",
    )(x2d, w_bf16)


def _local_mix_kernel(p_ref, caw_ref, cab_ref, nag_ref, nab_ref, pw_ref, ps_ref, ccw_ref,
                      o_ref, hist, *, ts, w):
    s = pl.program_id(1)

    @pl.when(s == 0)
    def _():
        hist[0:HALO, :] = jnp.zeros((HALO, 3 * w), F32)

    a_val = p_ref[:, 0:w]
    a_gate = p_ref[:, w:2 * w]
    c_h = p_ref[:, 3 * w:4 * w]
    c_gate_b = p_ref[:, 4 * w:5 * w]
    c_gate_c = p_ref[:, 5 * w:6 * w]
    hist[HALO:HALO + ts, 0:w] = a_val * jax.nn.sigmoid(a_gate)
    hist[HALO:HALO + ts, w:2 * w] = p_ref[:, 2 * w:3 * w]
    hist[HALO:HALO + ts, 2 * w:3 * w] = c_gate_c * c_h

    def past(col, back):
        return hist[HALO - back:HALO - back + ts, col * w:(col + 1) * w]

    acc = jnp.zeros((ts, w), F32)
    for k in range(CONF_WIDTH):
        acc = acc + caw_ref[k:k + 1, :] * past(0, CONF_WIDTH - 1 - k)
    h = _layer_norm(acc + cab_ref[...], nag_ref[...], nab_ref[...])
    o_ref[:, 0:w] = h * jax.nn.sigmoid(h)

    t_pos = s * ts + lax.broadcasted_iota(I32, (ts, 1), 0)
    chan = lax.broadcasted_iota(I32, (ts, w), 1)
    group = w // len(POOL_WINDOWS)
    x0 = past(1, 0)
    run = x0
    pooled = jnp.zeros((ts, w), F32)
    done = 1
    for g, win in enumerate(POOL_WINDOWS):
        for back in range(done, win):
            run = run + past(1, back)
        done = win
        inv = 1.0 / jnp.minimum(t_pos + 1, win).astype(F32)
        in_group = (chan >= g * group) & (chan < (g + 1) * group)
        pooled = jnp.where(in_group, run * inv, pooled)
    pooled = pooled - x0
    mixed = jnp.dot(pooled.astype(BF16), pw_ref[...], preferred_element_type=F32)
    o_ref[:, w:2 * w] = mixed * ps_ref[...]

    conv = jnp.zeros((ts, w), F32)
    for k in range(SCONV_WIDTH):
        conv = conv + ccw_ref[k:k + 1, :] * past(2, SCONV_WIDTH - 1 - k)
    o_ref[:, 2 * w:3 * w] = c_gate_b * conv

    hist[0:HALO, :] = hist[ts:ts + HALO, :]


def local_mix(proj3, conv_a_w, conv_a_b, norm_a_g, norm_a_b, pool_bd, pool_scale, conv_c_w, ts):
    b, s, _ = proj3.shape
    w = conv_a_w.shape[1]
    row = lambda a: a.reshape(1, w)
    full = lambda shape: pl.BlockSpec(shape, lambda bi, si: (0,) * len(shape))
    return pl.pallas_call(
        functools.partial(_local_mix_kernel, ts=ts, w=w),
        grid=(b, s // ts),
        in_specs=[pl.BlockSpec((None, ts, 6 * w), lambda bi, si: (bi, si, 0)),
                  full((CONF_WIDTH, w)), full((1, w)), full((1, w)), full((1, w)),
                  full((w, w)), full((1, w)), full((SCONV_WIDTH, w))],
        out_specs=pl.BlockSpec((None, ts, 3 * w), lambda bi, si: (bi, si, 0)),
        out_shape=jax.ShapeDtypeStruct((b, s, 3 * w), F32),
        scratch_shapes=[pltpu.VMEM((HALO + ts, 3 * w), F32)],
        compiler_params=_cparams("arbitrary", "arbitrary"),
        name="local_mix",
    )(proj3, conv_a_w, row(conv_a_b), row(norm_a_g), row(norm_a_b), pool_bd, row(pool_scale), conv_c_w)


def _attention_kernel(q_ref, k_ref, v_ref, m_ref, o_ref, acc_ref, car_ref, *, tq, scale):
    i = pl.program_id(2)
    q = q_ref[...]
    lane = lax.broadcasted_iota(I32, (tq, LANES), 1)
    q_heads = (jnp.where(lane < HEAD_DIM, q, 0.0).astype(BF16),
               jnp.where(lane >= HEAD_DIM, q, 0.0).astype(BF16))
    acc_ref[...] = jnp.zeros(acc_ref.shape, F32)
    car_ref[...] = jnp.zeros(car_ref.shape, F32)
    row = lax.broadcasted_iota(I32, (tq, tq), 0)
    col = lax.broadcasted_iota(I32, (tq, tq), 1)
    causal = col < row
    later = m_ref[...]

    def visit(j, diagonal):
        start = pl.multiple_of(j * tq, tq)
        kb = k_ref[pl.ds(start, tq), :].astype(BF16)
        vb = v_ref[pl.ds(start, tq), :].astype(BF16)
        for h in range(2):
            z = _dot_nt(q_heads[h], kb) * scale
            softplus = jnp.maximum(z, 0.0) + jnp.log1p(jnp.exp(-jnp.abs(z)))
            log_stay = -softplus
            log_att = z - softplus
            if diagonal:
                log_stay = jnp.where(causal, log_stay, 0.0)
            hi = log_stay.astype(BF16)
            lo = (log_stay - hi.astype(F32)).astype(BF16)
            inside = (jnp.dot(hi, later, preferred_element_type=F32)
                      + jnp.dot(lo, later, preferred_element_type=F32))
            carried = car_ref[h]
            wgt = jnp.exp(log_att + inside + carried)
            if diagonal:
                wgt = jnp.where(causal, wgt, 0.0)
            acc_ref[h] += jnp.dot(wgt.astype(BF16), vb, preferred_element_type=F32)
            car_ref[h] = carried + jnp.sum(log_stay, axis=1, keepdims=True)

    visit(i, True)

    def cond(c):
        j, worst = c
        return (j >= 0) & (worst > ATTN_UNDERFLOW)

    def body(c):
        j, _ = c
        visit(j, False)
        return j - 1, jnp.max(car_ref[...])

    lax.while_loop(cond, body, (i - 1, jnp.max(car_ref[...])))
    o_ref[...] = jnp.where(lane < HEAD_DIM, acc_ref[0], acc_ref[1])


def attention(proj3, later, tq, col0):
    b, s, _ = proj3.shape
    pairs = 2
    scale = HEAD_DIM ** -0.5
    return pl.pallas_call(
        functools.partial(_attention_kernel, tq=tq, scale=scale),
        grid=(b, pairs, s // tq),
        in_specs=[pl.BlockSpec((None, tq, LANES), lambda bi, p, i: (bi, i, col0 + p)),
                  pl.BlockSpec((None, s, LANES), lambda bi, p, i: (bi, 0, col0 + pairs + p)),
                  pl.BlockSpec((None, s, LANES), lambda bi, p, i: (bi, 0, col0 + 2 * pairs + p)),
                  pl.BlockSpec((tq, tq), lambda bi, p, i: (0, 0))],
        out_specs=pl.BlockSpec((None, tq, LANES), lambda bi, p, i: (bi, i, p)),
        out_shape=jax.ShapeDtypeStruct((b, s, pairs * LANES), F32),
        scratch_shapes=[pltpu.VMEM((2, tq, LANES), F32), pltpu.VMEM((2, tq, 1), F32)],
        compiler_params=_cparams("parallel", "parallel", "arbitrary"),
        name="attention",
    )(proj3, proj3, proj3, later)


def _out_ln_kernel(ya_ref, yd_ref, x_ref, w_ref, g_ref, b_ref, o_ref, *, alpha, wa):
    m = jnp.dot(ya_ref[...].astype(BF16), w_ref[0:wa, :], preferred_element_type=F32)
    m = m + jnp.dot(yd_ref[...].astype(BF16), w_ref[wa:, :], preferred_element_type=F32)
    o_ref[...] = _layer_norm(alpha * x_ref[...] + m, g_ref[...], b_ref[...])


def out_ln(y_abc, y_d, x2d, w_out_bf16, g, b, alpha, tm):
    t, d = x2d.shape
    wa, wd = y_abc.shape[1], y_d.shape[1]
    return pl.pallas_call(
        functools.partial(_out_ln_kernel, alpha=alpha, wa=wa),
        grid=(t // tm,),
        in_specs=[pl.BlockSpec((tm, wa), lambda i: (i, 0)),
                  pl.BlockSpec((tm, wd), lambda i: (i, 0)),
                  pl.BlockSpec((tm, d), lambda i: (i, 0)),
                  pl.BlockSpec((wa + wd, d), lambda i: (0, 0)),
                  pl.BlockSpec((1, d), lambda i: (0, 0)),
                  pl.BlockSpec((1, d), lambda i: (0, 0))],
        out_specs=pl.BlockSpec((tm, d), lambda i: (i, 0)),
        out_shape=jax.ShapeDtypeStruct((t, d), F32),
        compiler_params=_cparams("parallel"),
        name="out_ln",
    )(y_abc, y_d, x2d, w_out_bf16, g.reshape(1, d), b.reshape(1, d))


NEVER = 1e9


def _top_k_rows(vals, order, payload, k):
    out_v, out_p = [], []
    for _ in range(k):
        best = jnp.max(vals, axis=0, keepdims=True)
        first = jnp.min(jnp.where(vals == best, order, NEVER), axis=0, keepdims=True)
        hit = order == first
        out_v.append(best)
        if payload is None:
            out_p.append(first)
        else:
            out_p.append(jnp.max(jnp.where(hit, payload, -1.0), axis=0, keepdims=True))
        vals = jnp.where(hit, -jnp.inf, vals)
    return out_v, out_p


def _stack(rows):
    return jnp.concatenate(rows, axis=0)


def _pair_candidates(s0, i0, s1, i1, tm):
    k = PEER_TOPK
    sub = 8
    b_all = lax.broadcasted_iota(I32, (k, tm), 0).astype(F32)
    b_low = lax.broadcasted_iota(I32, (sub, tm), 0).astype(F32)
    s1_low, i1_low = _stack(s1[:sub]), _stack(i1[:sub])
    s0_low, i0_low = _stack(s0[:sub]), _stack(i0[:sub])
    vals, order, ident = [], [], []

    def add(v, o, e, keep=None):
        if keep is not None:
            v = jnp.where(keep, v, -jnp.inf)
            o = jnp.where(keep, o, NEVER)
        vals.append(v)
        order.append(o)
        ident.append(e)

    add(s0[0] + _stack(s1), b_all, i0[0] * PEER_NKEYS + _stack(i1))
    full_rows = 4
    for a in range(1, full_rows + 1):
        add(s0[a] + s1_low, a * k + b_low, i0[a] * PEER_NKEYS + i1_low)
    add(_stack(s0[sub:]) + s1[0], (b_low + sub) * k, _stack(i0[sub:]) * PEER_NKEYS + i1[0])
    rest = b_low > full_rows
    for b in range(2):
        add(s0_low + s1[b], b_low * k + b, i0_low * PEER_NKEYS + i1[b], rest)
    return _stack(vals), _stack(order), _stack(ident)


def _route_kernel(x_ref, wq_ref, keys_ref, idx_ref, gate_ref, *, tm):
    half = PEER_NKEYS
    qt = _dot_nt(wq_ref[...], x_ref[...].astype(BF16)).astype(BF16)
    key_id = lax.broadcasted_iota(I32, (PEER_NKEYS, tm), 0).astype(F32)
    idx_rows, gate_rows = [], []
    for h in range(PEER_HEADS):
        tops = []
        for p in range(2):
            hp = 2 * h + p
            scores = jnp.dot(keys_ref[hp], qt[hp * half:(hp + 1) * half, :], preferred_element_type=F32)
            tops.append(_top_k_rows(scores, key_id, None, PEER_TOPK))
        (s0, i0), (s1, i1) = tops
        cand_s, cand_order, cand_e = _pair_candidates(s0, i0, s1, i1, tm)
        best_s, best_e = _top_k_rows(cand_s, cand_order, cand_e, PEER_TOPK)
        e = jnp.exp(_stack(best_s) - best_s[0])
        gate_rows.append(e / jnp.sum(e, axis=0, keepdims=True))
        idx_rows.extend(best_e)
    idx_ref[...] = _stack(idx_rows).astype(I32).T
    gate_ref[...] = _stack(gate_rows).T


def route(x2d, wq_t_bf16, keys_bf16, tm):
    t, d = x2d.shape
    nq = wq_t_bf16.shape[0]
    sel = PEER_HEADS * PEER_TOPK
    return pl.pallas_call(
        functools.partial(_route_kernel, tm=tm),
        grid=(t // tm,),
        in_specs=[pl.BlockSpec((tm, d), lambda i: (i, 0)),
                  pl.BlockSpec((nq, d), lambda i: (0, 0)),
                  pl.BlockSpec(keys_bf16.shape, lambda i: (0, 0, 0))],
        out_specs=[pl.BlockSpec((tm, sel), lambda i: (i, 0)),
                   pl.BlockSpec((tm, sel), lambda i: (i, 0))],
        out_shape=[jax.ShapeDtypeStruct((t, sel), I32), jax.ShapeDtypeStruct((t, sel), F32)],
        compiler_params=_cparams("parallel"),
        name="route",
    )(x2d, wq_t_bf16, keys_bf16)


PEER_PIECES = 4
MXU_ROWS = 8


def _peer_kernel(idx_now, idx_next, x_ref, gate_ref, g_ref, b_ref, tab_ref, o_ref, rows, sems,
                 *, c, sel, d, alpha):
    i = pl.program_id(0)
    n = pl.num_programs(0)
    piece = d // PEER_PIECES
    per = sel // (2 * PEER_PIECES)

    def start_rows(idx_ref, src_row, half, tk, first):
        for j in range(first, first + per):
            pltpu.make_async_copy(tab_ref.at[idx_ref[src_row, j]], rows.at[half, tk, pl.ds(j, 1), :],
                                  sems.at[half, tk]).start(priority=j % 2)

    def wait_token(half, tk):
        pltpu.make_async_copy(rows.at[1 - half, tk], rows.at[half, tk], sems.at[half, tk]).wait()

    @pl.when(i == 0)
    def _():
        for tk in range(c):
            for first in range(0, sel, per):
                start_rows(idx_now, tk, 0, tk, first)

    def score_piece(tok, xb, q):
        half, tk = divmod(tok, c)
        cols = slice(q * piece, (q + 1) * piece)
        u = lax.bitcast_convert_type(rows[half, tk, :, cols] << 16, F32).astype(BF16)
        return _dot_nt(xb[:, cols], u)

    def mix_piece(tok, act, q):
        half, tk = divmod(tok, c)
        cols = slice(q * piece, (q + 1) * piece)
        v = lax.bitcast_convert_type(rows[half, tk, :, cols] & jnp.uint32(0xFFFF0000), F32).astype(BF16)
        return jnp.dot(act, v, preferred_element_type=F32)

    groups = 2 * PEER_PIECES
    scored = None
    mixed = None
    for k in range(2 * c + 2):
        tok = k if k < 2 * c else None
        if tok is not None:
            half, tk = divmod(tok, c)
            ahead_ref, ahead_row = (idx_now, tok + c) if half == 0 else (idx_next, tk)
            wait_token(half, tk)
            x = jnp.broadcast_to(x_ref[tok:tok + 1, :], (MXU_ROWS, d))
            xb = x.astype(BF16)
        y_done = None
        if mixed is not None:
            m_tok, m_x, m_f = mixed
            y_done = (m_tok, _layer_norm(alpha * m_x + m_f, g_ref[...], b_ref[...]))
        act = None
        if scored is not None:
            s_tok, s_x, s_score = scored
            act = (jax.nn.gelu(s_score) * gate_ref[s_tok:s_tok + 1, :]).astype(BF16)
        score = jnp.zeros((MXU_ROWS, sel), F32)
        parts = []
        issued = 0
        for batch in range(2):
            for q in range(batch * PEER_PIECES // 2, (batch + 1) * PEER_PIECES // 2):
                if act is not None:
                    parts.append(mix_piece(s_tok, act, q))
                if tok is not None:
                    score = score + score_piece(tok, xb, q)
            if tok is not None:
                for _ in range(groups // 4 if batch == 0 else groups - groups // 4):
                    start_rows(ahead_ref, ahead_row, 1 - half, tk, issued * per)
                    issued += 1
        if y_done is not None:
            o_ref[y_done[0]:y_done[0] + 1, :] = y_done[1][0:1, :]
        mixed = (s_tok, s_x, jnp.concatenate(parts, axis=1)) if act is not None else None
        scored = (tok, x, score) if tok is not None else None

    @pl.when(i == n - 1)
    def _():
        for tk in range(c):
            wait_token(0, tk)


def peer(x2d, idx, gates, table, g, b, alpha, c):
    t, d = x2d.shape
    sel = idx.shape[1]
    n = t // (2 * c)
    return pl.pallas_call(
        functools.partial(_peer_kernel, c=c, sel=sel, d=d, alpha=alpha),
        grid=(n,),
        in_specs=[pl.BlockSpec((2 * c, sel), lambda i: (i, 0), memory_space=pltpu.SMEM),
                  pl.BlockSpec((2 * c, sel), lambda i: (jnp.minimum(i + 1, n - 1), 0), memory_space=pltpu.SMEM),
                  pl.BlockSpec((2 * c, d), lambda i: (i, 0)),
                  pl.BlockSpec((2 * c, sel), lambda i: (i, 0)),
                  pl.BlockSpec((1, d), lambda i: (0, 0)),
                  pl.BlockSpec((1, d), lambda i: (0, 0)),
                  pl.BlockSpec(memory_space=pl.ANY)],
        out_specs=pl.BlockSpec((2 * c, d), lambda i: (i, 0)),
        out_shape=jax.ShapeDtypeStruct((t, d), F32),
        scratch_shapes=[pltpu.VMEM((2, c, sel, d), jnp.uint32), pltpu.SemaphoreType.DMA((2, c))],
        compiler_params=_cparams("arbitrary"),
        name="peer",
    )(idx, idx, x2d, gates, g.reshape(1, d), b.reshape(1, d), table)


SC_LANES = 16
SC_WORKERS = 32


def _sc_peer_body(x_hbm, idx_hbm, gate_hbm, tab_hbm, f_hbm, x_v, idx_v, gate_v, out_v, rows_v,
                  row_sems, in_sems, out_sems, *, per_worker, d, sel):
    heads = sel // PEER_TOPK
    chunks = d // SC_LANES
    wid = lax.axis_index("s") * 2 + lax.axis_index("c")
    first = wid * per_worker
    last = first + per_worker - 1

    def gather(p, h):
        return pltpu.make_async_copy(tab_hbm.at[idx_v.at[p, pl.ds(h * PEER_TOPK, PEER_TOPK)]],
                                     rows_v.at[h % 2], row_sems.at[h % 2])

    def inputs(g, p):
        return (pltpu.make_async_copy(idx_hbm.at[pl.ds(g * sel, sel)], idx_v.at[p], in_sems.at[p, 0]),
                pltpu.make_async_copy(gate_hbm.at[pl.ds(g * sel, sel)], gate_v.at[p], in_sems.at[p, 1]),
                pltpu.make_async_copy(x_hbm.at[pl.ds(g * d, d)], x_v.at[p], in_sems.at[p, 2]))

    def result(g, p):
        return pltpu.make_async_copy(out_v.at[p], f_hbm.at[pl.ds(g * d, d)], out_sems.at[p])

    def token(g, p, have_older):
        following = jnp.minimum(g + 1, last)
        for copy in inputs(following, 1 - p):
            copy.start()

        @pl.when(have_older)
        def _():
            result(g, p).wait()

        def clear(ci, c2):
            out_v[p, pl.ds(ci * SC_LANES, SC_LANES)] = jnp.zeros((SC_LANES,), F32)
            return c2
        lax.fori_loop(0, chunks, clear, 0)

        for h in range(heads):
            slot = h % 2
            if h + 1 < heads:
                gather(p, h + 1).start()
            else:
                for copy in inputs(following, 1 - p):
                    copy.wait()
                gather(1 - p, 0).start()
            gather(p, h).wait()

            def dots(ci, accs):
                xc = x_v[p, pl.ds(ci * SC_LANES, SC_LANES)]
                out = []
                for r in range(PEER_TOPK):
                    w = rows_v[slot, r, pl.ds(ci * SC_LANES, SC_LANES)]
                    out.append(accs[r] + lax.bitcast_convert_type(w << 16, F32) * xc)
                return tuple(out)
            accs = lax.fori_loop(0, chunks, dots, tuple(jnp.zeros((SC_LANES,), F32) for _ in range(PEER_TOPK)))
            lane = lax.iota(I32, SC_LANES)
            score = jnp.zeros((SC_LANES,), F32)
            for r in range(PEER_TOPK):
                score = jnp.where(lane == r, jnp.sum(accs[r]), score)
            inner = 0.7978845608028654 * (score + 0.044715 * score * score * score)
            tanh = 1.0 - 2.0 / (jnp.exp(2.0 * inner) + 1.0)
            act = 0.5 * score * (1.0 + tanh) * gate_v[p, pl.ds(h * PEER_TOPK, PEER_TOPK)]
            weights = [jnp.sum(jnp.where(lane == r, act, 0.0)) for r in range(PEER_TOPK)]

            def mix(ci, c2):
                part = [out_v[p, pl.ds(ci * SC_LANES, SC_LANES)]] + [jnp.zeros((SC_LANES,), F32)] * 3
                for r in range(PEER_TOPK):
                    w = rows_v[slot, r, pl.ds(ci * SC_LANES, SC_LANES)]
                    part[r % 4] = part[r % 4] + weights[r] * lax.bitcast_convert_type(
                        w & jnp.uint32(0xFFFF0000), F32)
                out_v[p, pl.ds(ci * SC_LANES, SC_LANES)] = (part[0] + part[1]) + (part[2] + part[3])
                return c2
            lax.fori_loop(0, chunks, mix, 0)
        result(g, p).start()

    for copy in inputs(first, 0):
        copy.start()
    for copy in inputs(first, 0):
        copy.wait()
    gather(0, 0).start()

    def pair(k, carry):
        token(first + 2 * k, 0, k > 0)
        token(first + 2 * k + 1, 1, k > 0)
        return carry
    lax.fori_loop(0, per_worker // 2, pair, 0)

    gather(0, 0).wait()
    result(last, 0).wait()
    result(last, 1).wait()


def sc_peer(x_flat, idx_flat, gate_flat, table2d):
    from jax.experimental.pallas import tpu_sc as plsc
    d = table2d.shape[1]
    tokens = x_flat.shape[0] // d
    sel = idx_flat.shape[0] // tokens
    per_worker = tokens // SC_WORKERS
    mesh = plsc.VectorSubcoreMesh(core_axis_name="c", subcore_axis_name="s")
    return pl.kernel(
        functools.partial(_sc_peer_body, per_worker=per_worker, d=d, sel=sel),
        out_type=jax.ShapeDtypeStruct((tokens * d,), F32),
        mesh=mesh,
        scratch_types=[pltpu.VMEM((2, d), F32), pltpu.VMEM((2, sel), I32), pltpu.VMEM((2, sel), F32),
                       pltpu.VMEM((2, d), F32), pltpu.VMEM((2, PEER_TOPK, d), jnp.uint32),
                       pltpu.SemaphoreType.DMA((2,)), pltpu.SemaphoreType.DMA((2, 3)),
                       pltpu.SemaphoreType.DMA((2,))],
        compiler_params=pltpu.CompilerParams(needs_layout_passes=False),
        name="sc_peer",
    )(x_flat, idx_flat, gate_flat, table2d)


def _block_diag(pool_w):
    g, c, _ = pool_w.shape
    out = jnp.zeros((g * c, g * c), pool_w.dtype)
    for k in range(g):
        out = out.at[k * c:(k + 1) * c, k * c:(k + 1) * c].set(pool_w[k])
    return out


def _pack_rows(u, v):
    bits = lambda a: lax.bitcast_convert_type(a.astype(BF16), jnp.uint16).astype(jnp.uint32)
    return bits(u) | (bits(v) << 16)


def _residual_ln_kernel(x_ref, f_ref, g_ref, b_ref, o_ref, *, alpha):
    o_ref[...] = _layer_norm(alpha * x_ref[...] + f_ref[...], g_ref[...], b_ref[...])


def residual_ln(x2d, f2d, g, b, alpha, tm):
    t, d = x2d.shape
    return pl.pallas_call(
        functools.partial(_residual_ln_kernel, alpha=alpha),
        grid=(t // tm,),
        in_specs=[pl.BlockSpec((tm, d), lambda i: (i, 0)),
                  pl.BlockSpec((tm, d), lambda i: (i, 0)),
                  pl.BlockSpec((1, d), lambda i: (0, 0)),
                  pl.BlockSpec((1, d), lambda i: (0, 0))],
        out_specs=pl.BlockSpec((tm, d), lambda i: (i, 0)),
        out_shape=jax.ShapeDtypeStruct((t, d), F32),
        compiler_params=_cparams("parallel"),
        name="residual_ln",
    )(x2d, f2d, g.reshape(1, d), b.reshape(1, d))


def _sc_batches(b):
    return b // 2


def kernel(x, w_in, conv_a_w, conv_a_b, norm_a_g, norm_a_b, pool_w, pool_scale, conv_c_w, w_out,
           ln1_g, ln1_b, peer_wq, peer_keys, peer_u, peer_v, ln2_g, ln2_b):
    depth = w_in.shape[0]
    b, s, d = x.shape
    alpha = float((2 * depth) ** 0.25)
    w = conv_a_w.shape[2]
    tq = min(256, s)
    ts = min(512, s)
    tm = min(512, s)
    row = lax.broadcasted_iota(I32, (tq, tq), 0)
    col = lax.broadcasted_iota(I32, (tq, tq), 1)
    later = (row > col).astype(BF16)
    b_tc = b - _sc_batches(b)
    for l in range(depth):
        w_in_l, w_out_l = w_in[l].astype(BF16), w_out[l].astype(BF16)
        pool_bd = _block_diag(pool_w[l]).astype(BF16)
        keys = peer_keys[l].reshape(2 * PEER_HEADS, PEER_NKEYS, -1).astype(BF16)
        wq_t = peer_wq[l].T.astype(BF16)
        words = _pack_rows(peer_u[l], peer_v[l])

        def front(xp):
            tp = xp.shape[0] * s
            xpf = xp.reshape(tp, d)
            proj3 = in_proj(xpf, w_in_l, tm).reshape(xp.shape[0], s, -1)
            y_abc = local_mix(proj3, conv_a_w[l], conv_a_b[l], norm_a_g[l], norm_a_b[l],
                              pool_bd, pool_scale[l], conv_c_w[l], ts)
            y_d = attention(proj3, later, tq, (6 * w) // LANES)
            x1 = out_ln(y_abc.reshape(tp, 3 * w), y_d.reshape(tp, -1), xpf, w_out_l,
                        ln1_g[l], ln1_b[l], alpha, tm)
            idx, gates = route(x1, wq_t, keys, min(256, s))
            return x1, idx, gates

        outs = []
        if b_tc < b:
            x1_sc, idx_sc, gates_sc = front(x[b_tc:])
            f_sc = sc_peer(x1_sc.reshape(-1), idx_sc.reshape(-1), gates_sc.reshape(-1), words)
        x1_tc, idx_tc, gates_tc = front(x[:b_tc])
        outs.append(peer(x1_tc, idx_tc, gates_tc, words[:, None, :], ln2_g[l], ln2_b[l], alpha, 16))
        if b_tc < b:
            outs.append(residual_ln(x1_sc, f_sc.reshape(-1, d), ln2_g[l], ln2_b[l], alpha, tm))
        x = jnp.concatenate(outs, axis=0).reshape(b, s, d)
    return x
```

```python
import functools

import jax
import jax.numpy as jnp
from jax import lax
from jax.experimental import pallas as pl
from jax.experimental.pallas import tpu as pltpu

F32 = jnp.float32
BF16 = jnp.bfloat16
I32 = jnp.int32

LN_EPS = 1e-5
LANES = 128
CONF_WIDTH = 31
POOL_WINDOWS = (2, 4, 8, 16)
SCONV_WIDTH = 3
HEAD_DIM = 64
PEER_HEADS = 8
PEER_NKEYS = 128
PEER_TOPK = 16
HALO = 32
ATTN_UNDERFLOW = -100.0
VMEM_LIMIT = 56 * 1024 * 1024


def _cparams(*sem):
    return pltpu.CompilerParams(dimension_semantics=sem, vmem_limit_bytes=VMEM_LIMIT)


def _layer_norm(h, g, b):
    mu = jnp.mean(h, axis=-1, keepdims=True)
    hc = h - mu
    var = jnp.mean(hc * hc, axis=-1, keepdims=True)
    return hc * lax.rsqrt(var + LN_EPS) * g + b


def _dot_nt(a, b):
    return lax.dot_general(a, b, (((1,), (1,)), ((), ())), preferred_element_type=F32)


def _in_proj_kernel(x_ref, w_ref, o_ref):
    o_ref[...] = jnp.dot(x_ref[...].astype(BF16), w_ref[...], preferred_element_type=F32)


def in_proj(x2d, w_bf16, tm):
    t, d = x2d.shape
    n = w_bf16.shape[1]
    return pl.pallas_call(
        _in_proj_kernel,
        grid=(t // tm,),
        in_specs=[pl.BlockSpec((tm, d), lambda i: (i, 0)),
                  pl.BlockSpec((d, n), lambda i: (0, 0))],
        out_specs=pl.BlockSpec((tm, n), lambda i: (i, 0)),
        out_shape=jax.ShapeDtypeStruct((t, n), F32),
        compiler_params=_cparams("parallel"),
        name="in_proj",
    )(x2d, w_bf16)


def _local_mix_kernel(p_ref, caw_ref, cab_ref, nag_ref, nab_ref, pw_ref, ps_ref, ccw_ref,
                      o_ref, hist, *, ts, w):
    s = pl.program_id(1)

    @pl.when(s == 0)
    def _():
        hist[0:HALO, :] = jnp.zeros((HALO, 3 * w), F32)

    a_val = p_ref[:, 0:w]
    a_gate = p_ref[:, w:2 * w]
    c_h = p_ref[:, 3 * w:4 * w]
    c_gate_b = p_ref[:, 4 * w:5 * w]
    c_gate_c = p_ref[:, 5 * w:6 * w]
    hist[HALO:HALO + ts, 0:w] = a_val * jax.nn.sigmoid(a_gate)
    hist[HALO:HALO + ts, w:2 * w] = p_ref[:, 2 * w:3 * w]
    hist[HALO:HALO + ts, 2 * w:3 * w] = c_gate_c * c_h

    def past(col, back):
        return hist[HALO - back:HALO - back + ts, col * w:(col + 1) * w]

    acc = jnp.zeros((ts, w), F32)
    for k in range(CONF_WIDTH):
        acc = acc + caw_ref[k:k + 1, :] * past(0, CONF_WIDTH - 1 - k)
    h = _layer_norm(acc + cab_ref[...], nag_ref[...], nab_ref[...])
    o_ref[:, 0:w] = h * jax.nn.sigmoid(h)

    t_pos = s * ts + lax.broadcasted_iota(I32, (ts, 1), 0)
    chan = lax.broadcasted_iota(I32, (ts, w), 1)
    group = w // len(POOL_WINDOWS)
    x0 = past(1, 0)
    run = x0
    pooled = jnp.zeros((ts, w), F32)
    done = 1
    for g, win in enumerate(POOL_WINDOWS):
        for back in range(done, win):
            run = run + past(1, back)
        done = win
        inv = 1.0 / jnp.minimum(t_pos + 1, win).astype(F32)
        in_group = (chan >= g * group) & (chan < (g + 1) * group)
        pooled = jnp.where(in_group, run * inv, pooled)
    pooled = pooled - x0
    mixed = jnp.dot(pooled.astype(BF16), pw_ref[...], preferred_element_type=F32)
    o_ref[:, w:2 * w] = mixed * ps_ref[...]

    conv = jnp.zeros((ts, w), F32)
    for k in range(SCONV_WIDTH):
        conv = conv + ccw_ref[k:k + 1, :] * past(2, SCONV_WIDTH - 1 - k)
    o_ref[:, 2 * w:3 * w] = c_gate_b * conv

    hist[0:HALO, :] = hist[ts:ts + HALO, :]


def local_mix(proj3, conv_a_w, conv_a_b, norm_a_g, norm_a_b, pool_bd, pool_scale, conv_c_w, ts):
    b, s, _ = proj3.shape
    w = conv_a_w.shape[1]
    row = lambda a: a.reshape(1, w)
    full = lambda shape: pl.BlockSpec(shape, lambda bi, si: (0,) * len(shape))
    return pl.pallas_call(
        functools.partial(_local_mix_kernel, ts=ts, w=w),
        grid=(b, s // ts),
        in_specs=[pl.BlockSpec((None, ts, 6 * w), lambda bi, si: (bi, si, 0)),
                  full((CONF_WIDTH, w)), full((1, w)), full((1, w)), full((1, w)),
                  full((w, w)), full((1, w)), full((SCONV_WIDTH, w))],
        out_specs=pl.BlockSpec((None, ts, 3 * w), lambda bi, si: (bi, si, 0)),
        out_shape=jax.ShapeDtypeStruct((b, s, 3 * w), F32),
        scratch_shapes=[pltpu.VMEM((HALO + ts, 3 * w), F32)],
        compiler_params=_cparams("arbitrary", "arbitrary"),
        name="local_mix",
    )(proj3, conv_a_w, row(conv_a_b), row(norm_a_g), row(norm_a_b), pool_bd, row(pool_scale), conv_c_w)


def _attention_kernel(q_ref, k_ref, v_ref, m_ref, o_ref, acc_ref, car_ref, *, tq, scale):
    i = pl.program_id(2)
    q = q_ref[...]
    lane = lax.broadcasted_iota(I32, (tq, LANES), 1)
    q_heads = (jnp.where(lane < HEAD_DIM, q, 0.0).astype(BF16),
               jnp.where(lane >= HEAD_DIM, q, 0.0).astype(BF16))
    acc_ref[...] = jnp.zeros(acc_ref.shape, F32)
    car_ref[...] = jnp.zeros(car_ref.shape, F32)
    row = lax.broadcasted_iota(I32, (tq, tq), 0)
    col = lax.broadcasted_iota(I32, (tq, tq), 1)
    causal = col < row
    later = m_ref[...]

    def visit(j, diagonal):
        start = pl.multiple_of(j * tq, tq)
        kb = k_ref[pl.ds(start, tq), :].astype(BF16)
        vb = v_ref[pl.ds(start, tq), :].astype(BF16)
        for h in range(2):
            z = _dot_nt(q_heads[h], kb) * scale
            softplus = jnp.maximum(z, 0.0) + jnp.log1p(jnp.exp(-jnp.abs(z)))
            log_stay = -softplus
            log_att = z - softplus
            if diagonal:
                log_stay = jnp.where(causal, log_stay, 0.0)
            hi = log_stay.astype(BF16)
            lo = (log_stay - hi.astype(F32)).astype(BF16)
            inside = (jnp.dot(hi, later, preferred_element_type=F32)
                      + jnp.dot(lo, later, preferred_element_type=F32))
            carried = car_ref[h]
            wgt = jnp.exp(log_att + inside + carried)
            if diagonal:
                wgt = jnp.where(causal, wgt, 0.0)
            acc_ref[h] += jnp.dot(wgt.astype(BF16), vb, preferred_element_type=F32)
            car_ref[h] = carried + jnp.sum(log_stay, axis=1, keepdims=True)

    visit(i, True)

    def cond(c):
        j, worst = c
        return (j >= 0) & (worst > ATTN_UNDERFLOW)

    def body(c):
        j, _ = c
        visit(j, False)
        return j - 1, jnp.max(car_ref[...])

    lax.while_loop(cond, body, (i - 1, jnp.max(car_ref[...])))
    o_ref[...] = jnp.where(lane < HEAD_DIM, acc_ref[0], acc_ref[1])


def attention(proj3, later, tq, col0):
    b, s, _ = proj3.shape
    pairs = 2
    scale = HEAD_DIM ** -0.5
    return pl.pallas_call(
        functools.partial(_attention_kernel, tq=tq, scale=scale),
        grid=(b, pairs, s // tq),
        in_specs=[pl.BlockSpec((None, tq, LANES), lambda bi, p, i: (bi, i, col0 + p)),
                  pl.BlockSpec((None, s, LANES), lambda bi, p, i: (bi, 0, col0 + pairs + p)),
                  pl.BlockSpec((None, s, LANES), lambda bi, p, i: (bi, 0, col0 + 2 * pairs + p)),
                  pl.BlockSpec((tq, tq), lambda bi, p, i: (0, 0))],
        out_specs=pl.BlockSpec((None, tq, LANES), lambda bi, p, i: (bi, i, p)),
        out_shape=jax.ShapeDtypeStruct((b, s, pairs * LANES), F32),
        scratch_shapes=[pltpu.VMEM((2, tq, LANES), F32), pltpu.VMEM((2, tq, 1), F32)],
        compiler_params=_cparams("parallel", "parallel", "arbitrary"),
        name="attention",
    )(proj3, proj3, proj3, later)


def _out_ln_kernel(ya_ref, yd_ref, x_ref, w_ref, g_ref, b_ref, o_ref, *, alpha, wa):
    m = jnp.dot(ya_ref[...].astype(BF16), w_ref[0:wa, :], preferred_element_type=F32)
    m = m + jnp.dot(yd_ref[...].astype(BF16), w_ref[wa:, :], preferred_element_type=F32)
    o_ref[...] = _layer_norm(alpha * x_ref[...] + m, g_ref[...], b_ref[...])


def out_ln(y_abc, y_d, x2d, w_out_bf16, g, b, alpha, tm):
    t, d = x2d.shape
    wa, wd = y_abc.shape[1], y_d.shape[1]
    return pl.pallas_call(
        functools.partial(_out_ln_kernel, alpha=alpha, wa=wa),
        grid=(t // tm,),
        in_specs=[pl.BlockSpec((tm, wa), lambda i: (i, 0)),
                  pl.BlockSpec((tm, wd), lambda i: (i, 0)),
                  pl.BlockSpec((tm, d), lambda i: (i, 0)),
                  pl.BlockSpec((wa + wd, d), lambda i: (0, 0)),
                  pl.BlockSpec((1, d), lambda i: (0, 0)),
                  pl.BlockSpec((1, d), lambda i: (0, 0))],
        out_specs=pl.BlockSpec((tm, d), lambda i: (i, 0)),
        out_shape=jax.ShapeDtypeStruct((t, d), F32),
        compiler_params=_cparams("parallel"),
        name="out_ln",
    )(y_abc, y_d, x2d, w_out_bf16, g.reshape(1, d), b.reshape(1, d))


NEVER = 1e9


def _top_k_rows(vals, order, payload, k):
    out_v, out_p = [], []
    for _ in range(k):
        best = jnp.max(vals, axis=0, keepdims=True)
        first = jnp.min(jnp.where(vals == best, order, NEVER), axis=0, keepdims=True)
        hit = order == first
        out_v.append(best)
        if payload is None:
            out_p.append(first)
        else:
            out_p.append(jnp.max(jnp.where(hit, payload, -1.0), axis=0, keepdims=True))
        vals = jnp.where(hit, -jnp.inf, vals)
    return out_v, out_p


def _stack(rows):
    return jnp.concatenate(rows, axis=0)


def _pair_candidates(s0, i0, s1, i1, tm):
    k = PEER_TOPK
    sub = 8
    b_all = lax.broadcasted_iota(I32, (k, tm), 0).astype(F32)
    b_low = lax.broadcasted_iota(I32, (sub, tm), 0).astype(F32)
    s1_low, i1_low = _stack(s1[:sub]), _stack(i1[:sub])
    s0_low, i0_low = _stack(s0[:sub]), _stack(i0[:sub])
    vals, order, ident = [], [], []

    def add(v, o, e, keep=None):
        if keep is not None:
            v = jnp.where(keep, v, -jnp.inf)
            o = jnp.where(keep, o, NEVER)
        vals.append(v)
        order.append(o)
        ident.append(e)

    add(s0[0] + _stack(s1), b_all, i0[0] * PEER_NKEYS + _stack(i1))
    full_rows = 4
    for a in range(1, full_rows + 1):
        add(s0[a] + s1_low, a * k + b_low, i0[a] * PEER_NKEYS + i1_low)
    add(_stack(s0[sub:]) + s1[0], (b_low + sub) * k, _stack(i0[sub:]) * PEER_NKEYS + i1[0])
    rest = b_low > full_rows
    for b in range(2):
        add(s0_low + s1[b], b_low * k + b, i0_low * PEER_NKEYS + i1[b], rest)
    return _stack(vals), _stack(order), _stack(ident)


def _route_kernel(x_ref, wq_ref, keys_ref, idx_ref, gate_ref, *, tm):
    half = PEER_NKEYS
    qt = _dot_nt(wq_ref[...], x_ref[...].astype(BF16)).astype(BF16)
    key_id = lax.broadcasted_iota(I32, (PEER_NKEYS, tm), 0).astype(F32)
    idx_rows, gate_rows = [], []
    for h in range(PEER_HEADS):
        tops = []
        for p in range(2):
            hp = 2 * h + p
            scores = jnp.dot(keys_ref[hp], qt[hp * half:(hp + 1) * half, :], preferred_element_type=F32)
            tops.append(_top_k_rows(scores, key_id, None, PEER_TOPK))
        (s0, i0), (s1, i1) = tops
        cand_s, cand_order, cand_e = _pair_candidates(s0, i0, s1, i1, tm)
        best_s, best_e = _top_k_rows(cand_s, cand_order, cand_e, PEER_TOPK)
        e = jnp.exp(_stack(best_s) - best_s[0])
        gate_rows.append(e / jnp.sum(e, axis=0, keepdims=True))
        idx_rows.extend(best_e)
    idx_ref[...] = _stack(idx_rows).astype(I32).T
    gate_ref[...] = _stack(gate_rows).T


def route(x2d, wq_t_bf16, keys_bf16, tm):
    t, d = x2d.shape
    nq = wq_t_bf16.shape[0]
    sel = PEER_HEADS * PEER_TOPK
    return pl.pallas_call(
        functools.partial(_route_kernel, tm=tm),
        grid=(t // tm,),
        in_specs=[pl.BlockSpec((tm, d), lambda i: (i, 0)),
                  pl.BlockSpec((nq, d), lambda i: (0, 0)),
                  pl.BlockSpec(keys_bf16.shape, lambda i: (0, 0, 0))],
        out_specs=[pl.BlockSpec((tm, sel), lambda i: (i, 0)),
                   pl.BlockSpec((tm, sel), lambda i: (i, 0))],
        out_shape=[jax.ShapeDtypeStruct((t, sel), I32), jax.ShapeDtypeStruct((t, sel), F32)],
        compiler_params=_cparams("parallel"),
        name="route",
    )(x2d, wq_t_bf16, keys_bf16)


PEER_PIECES = 4
MXU_ROWS = 8


def _peer_kernel(idx_now, idx_next, x_ref, gate_ref, g_ref, b_ref, tab_ref, o_ref, rows, sems,
                 *, c, sel, d, alpha):
    i = pl.program_id(0)
    n = pl.num_programs(0)
    piece = d // PEER_PIECES
    per = sel // (2 * PEER_PIECES)

    def start_rows(idx_ref, src_row, half, tk, first):
        for j in range(first, first + per):
            pltpu.make_async_copy(tab_ref.at[idx_ref[src_row, j]], rows.at[half, tk, pl.ds(j, 1), :],
                                  sems.at[half, tk]).start(priority=j % 2)

    def wait_token(half, tk):
        pltpu.make_async_copy(rows.at[1 - half, tk], rows.at[half, tk], sems.at[half, tk]).wait()

    @pl.when(i == 0)
    def _():
        for tk in range(c):
            for first in range(0, sel, per):
                start_rows(idx_now, tk, 0, tk, first)

    def score_piece(tok, xb, q):
        half, tk = divmod(tok, c)
        cols = slice(q * piece, (q + 1) * piece)
        u = lax.bitcast_convert_type(rows[half, tk, :, cols] << 16, F32).astype(BF16)
        return _dot_nt(xb[:, cols], u)

    def mix_piece(tok, act, q):
        half, tk = divmod(tok, c)
        cols = slice(q * piece, (q + 1) * piece)
        v = lax.bitcast_convert_type(rows[half, tk, :, cols] & jnp.uint32(0xFFFF0000), F32).astype(BF16)
        return jnp.dot(act, v, preferred_element_type=F32)

    groups = 2 * PEER_PIECES
    scored = None
    mixed = None
    for k in range(2 * c + 2):
        tok = k if k < 2 * c else None
        if tok is not None:
            half, tk = divmod(tok, c)
            ahead_ref, ahead_row = (idx_now, tok + c) if half == 0 else (idx_next, tk)
            wait_token(half, tk)
            x = jnp.broadcast_to(x_ref[tok:tok + 1, :], (MXU_ROWS, d))
            xb = x.astype(BF16)
        y_done = None
        if mixed is not None:
            m_tok, m_x, m_f = mixed
            y_done = (m_tok, _layer_norm(alpha * m_x + m_f, g_ref[...], b_ref[...]))
        act = None
        if scored is not None:
            s_tok, s_x, s_score = scored
            act = (jax.nn.gelu(s_score) * gate_ref[s_tok:s_tok + 1, :]).astype(BF16)
        score = jnp.zeros((MXU_ROWS, sel), F32)
        parts = []
        issued = 0
        for batch in range(2):
            for q in range(batch * PEER_PIECES // 2, (batch + 1) * PEER_PIECES // 2):
                if act is not None:
                    parts.append(mix_piece(s_tok, act, q))
                if tok is not None:
                    score = score + score_piece(tok, xb, q)
            if tok is not None:
                for _ in range(groups // 4 if batch == 0 else groups - groups // 4):
                    start_rows(ahead_ref, ahead_row, 1 - half, tk, issued * per)
                    issued += 1
        if y_done is not None:
            o_ref[y_done[0]:y_done[0] + 1, :] = y_done[1][0:1, :]
        mixed = (s_tok, s_x, jnp.concatenate(parts, axis=1)) if act is not None else None
        scored = (tok, x, score) if tok is not None else None

    @pl.when(i == n - 1)
    def _():
        for tk in range(c):
            wait_token(0, tk)


def peer(x2d, idx, gates, table, g, b, alpha, c):
    t, d = x2d.shape
    sel = idx.shape[1]
    n = t // (2 * c)
    return pl.pallas_call(
        functools.partial(_peer_kernel, c=c, sel=sel, d=d, alpha=alpha),
        grid=(n,),
        in_specs=[pl.BlockSpec((2 * c, sel), lambda i: (i, 0), memory_space=pltpu.SMEM),
                  pl.BlockSpec((2 * c, sel), lambda i: (jnp.minimum(i + 1, n - 1), 0), memory_space=pltpu.SMEM),
                  pl.BlockSpec((2 * c, d), lambda i: (i, 0)),
                  pl.BlockSpec((2 * c, sel), lambda i: (i, 0)),
                  pl.BlockSpec((1, d), lambda i: (0, 0)),
                  pl.BlockSpec((1, d), lambda i: (0, 0)),
                  pl.BlockSpec(memory_space=pl.ANY)],
        out_specs=pl.BlockSpec((2 * c, d), lambda i: (i, 0)),
        out_shape=jax.ShapeDtypeStruct((t, d), F32),
        scratch_shapes=[pltpu.VMEM((2, c, sel, d), jnp.uint32), pltpu.SemaphoreType.DMA((2, c))],
        compiler_params=_cparams("arbitrary"),
        name="peer",
    )(idx, idx, x2d, gates, g.reshape(1, d), b.reshape(1, d), table)


SC_LANES = 16
SC_WORKERS = 32


def _sc_peer_body(x_hbm, idx_hbm, gate_hbm, tab_hbm, f_hbm, x_v, idx_v, gate_v, out_v, rows_v,
                  row_sems, in_sems, out_sems, *, per_worker, d, sel):
    heads = sel // PEER_TOPK
    chunks = d // SC_LANES
    wid = lax.axis_index("s") * 2 + lax.axis_index("c")
    first = wid * per_worker
    last = first + per_worker - 1

    def gather(p, h):
        return pltpu.make_async_copy(tab_hbm.at[idx_v.at[p, pl.ds(h * PEER_TOPK, PEER_TOPK)]],
                                     rows_v.at[h % 2], row_sems.at[h % 2])

    def inputs(g, p):
        return (pltpu.make_async_copy(idx_hbm.at[pl.ds(g * sel, sel)], idx_v.at[p], in_sems.at[p, 0]),
                pltpu.make_async_copy(gate_hbm.at[pl.ds(g * sel, sel)], gate_v.at[p], in_sems.at[p, 1]),
                pltpu.make_async_copy(x_hbm.at[pl.ds(g * d, d)], x_v.at[p], in_sems.at[p, 2]))

    def result(g, p):
        return pltpu.make_async_copy(out_v.at[p], f_hbm.at[pl.ds(g * d, d)], out_sems.at[p])

    def token(g, p, have_older):
        following = jnp.minimum(g + 1, last)
        for copy in inputs(following, 1 - p):
            copy.start()

        @pl.when(have_older)
        def _():
            result(g, p).wait()

        def clear(ci, c2):
            out_v[p, pl.ds(ci * SC_LANES, SC_LANES)] = jnp.zeros((SC_LANES,), F32)
            return c2
        lax.fori_loop(0, chunks, clear, 0)

        for h in range(heads):
            slot = h % 2
            if h + 1 < heads:
                gather(p, h + 1).start()
            else:
                for copy in inputs(following, 1 - p):
                    copy.wait()
                gather(1 - p, 0).start()
            gather(p, h).wait()

            def dots(ci, accs):
                xc = x_v[p, pl.ds(ci * SC_LANES, SC_LANES)]
                out = []
                for r in range(PEER_TOPK):
                    w = rows_v[slot, r, pl.ds(ci * SC_LANES, SC_LANES)]
                    out.append(accs[r] + lax.bitcast_convert_type(w << 16, F32) * xc)
                return tuple(out)
            accs = lax.fori_loop(0, chunks, dots, tuple(jnp.zeros((SC_LANES,), F32) for _ in range(PEER_TOPK)))
            lane = lax.iota(I32, SC_LANES)
            score = jnp.zeros((SC_LANES,), F32)
            for r in range(PEER_TOPK):
                score = jnp.where(lane == r, jnp.sum(accs[r]), score)
            inner = 0.7978845608028654 * (score + 0.044715 * score * score * score)
            tanh = 1.0 - 2.0 / (jnp.exp(2.0 * inner) + 1.0)
            act = 0.5 * score * (1.0 + tanh) * gate_v[p, pl.ds(h * PEER_TOPK, PEER_TOPK)]
            weights = [jnp.sum(jnp.where(lane == r, act, 0.0)) for r in range(PEER_TOPK)]

            def mix(ci, c2):
                part = [out_v[p, pl.ds(ci * SC_LANES, SC_LANES)]] + [jnp.zeros((SC_LANES,), F32)] * 3
                for r in range(PEER_TOPK):
                    w = rows_v[slot, r, pl.ds(ci * SC_LANES, SC_LANES)]
                    part[r % 4] = part[r % 4] + weights[r] * lax.bitcast_convert_type(
                        w & jnp.uint32(0xFFFF0000), F32)
                out_v[p, pl.ds(ci * SC_LANES, SC_LANES)] = (part[0] + part[1]) + (part[2] + part[3])
                return c2
            lax.fori_loop(0, chunks, mix, 0)
        result(g, p).start()

    for copy in inputs(first, 0):
        copy.start()
    for copy in inputs(first, 0):
        copy.wait()
    gather(0, 0).start()

    def pair(k, carry):
        token(first + 2 * k, 0, k > 0)
        token(first + 2 * k + 1, 1, k > 0)
        return carry
    lax.fori_loop(0, per_worker // 2, pair, 0)

    gather(0, 0).wait()
    result(last, 0).wait()
    result(last, 1).wait()


def sc_peer(x_flat, idx_flat, gate_flat, table2d):
    from jax.experimental.pallas import tpu_sc as plsc
    d = table2d.shape[1]
    tokens = x_flat.shape[0] // d
    sel = idx_flat.shape[0] // tokens
    per_worker = tokens // SC_WORKERS
    mesh = plsc.VectorSubcoreMesh(core_axis_name="c", subcore_axis_name="s")
    return pl.kernel(
        functools.partial(_sc_peer_body, per_worker=per_worker, d=d, sel=sel),
        out_type=jax.ShapeDtypeStruct((tokens * d,), F32),
        mesh=mesh,
        scratch_types=[pltpu.VMEM((2, d), F32), pltpu.VMEM((2, sel), I32), pltpu.VMEM((2, sel), F32),
                       pltpu.VMEM((2, d), F32), pltpu.VMEM((2, PEER_TOPK, d), jnp.uint32),
                       pltpu.SemaphoreType.DMA((2,)), pltpu.SemaphoreType.DMA((2, 3)),
                       pltpu.SemaphoreType.DMA((2,))],
        compiler_params=pltpu.CompilerParams(needs_layout_passes=False),
        name="sc_peer",
    )(x_flat, idx_flat, gate_flat, table2d)


def _block_diag(pool_w):
    g, c, _ = pool_w.shape
    out = jnp.zeros((g * c, g * c), pool_w.dtype)
    for k in range(g):
        out = out.at[k * c:(k + 1) * c, k * c:(k + 1) * c].set(pool_w[k])
    return out


def _pack_rows(u, v):
    bits = lambda a: lax.bitcast_convert_type(a.astype(BF16), jnp.uint16).astype(jnp.uint32)
    return bits(u) | (bits(v) << 16)


def _residual_ln_kernel(x_ref, f_ref, g_ref, b_ref, o_ref, *, alpha):
    o_ref[...] = _layer_norm(alpha * x_ref[...] + f_ref[...], g_ref[...], b_ref[...])


def residual_ln(x2d, f2d, g, b, alpha, tm):
    t, d = x2d.shape
    return pl.pallas_call(
        functools.partial(_residual_ln_kernel, alpha=alpha),
        grid=(t // tm,),
        in_specs=[pl.BlockSpec((tm, d), lambda i: (i, 0)),
                  pl.BlockSpec((tm, d), lambda i: (i, 0)),
                  pl.BlockSpec((1, d), lambda i: (0, 0)),
                  pl.BlockSpec((1, d), lambda i: (0, 0))],
        out_specs=pl.BlockSpec((tm, d), lambda i: (i, 0)),
        out_shape=jax.ShapeDtypeStruct((t, d), F32),
        compiler_params=_cparams("parallel"),
        name="residual_ln",
    )(x2d, f2d, g.reshape(1, d), b.reshape(1, d))


def _sc_batches(b):
    return b // 2


def _sc_spill(tokens):
    step = 2 * SC_WORKERS
    return tokens * 19 // 256 // step * step


def _row_tile(t):
    return next(tile for tile in (512, 256, 128, 64, 32, 16, 8) if t % tile == 0)


def kernel(x, w_in, conv_a_w, conv_a_b, norm_a_g, norm_a_b, pool_w, pool_scale, conv_c_w, w_out,
           ln1_g, ln1_b, peer_wq, peer_keys, peer_u, peer_v, ln2_g, ln2_b):
    depth = w_in.shape[0]
    b, s, d = x.shape
    alpha = float((2 * depth) ** 0.25)
    w = conv_a_w.shape[2]
    tq = min(256, s)
    ts = min(512, s)
    tm = min(512, s)
    row = lax.broadcasted_iota(I32, (tq, tq), 0)
    col = lax.broadcasted_iota(I32, (tq, tq), 1)
    later = (row > col).astype(BF16)
    b_tc = b - _sc_batches(b)
    for l in range(depth):
        w_in_l, w_out_l = w_in[l].astype(BF16), w_out[l].astype(BF16)
        pool_bd = _block_diag(pool_w[l]).astype(BF16)
        keys = peer_keys[l].reshape(2 * PEER_HEADS, PEER_NKEYS, -1).astype(BF16)
        wq_t = peer_wq[l].T.astype(BF16)
        words = _pack_rows(peer_u[l], peer_v[l])

        def front(xp):
            tp = xp.shape[0] * s
            xpf = xp.reshape(tp, d)
            proj3 = in_proj(xpf, w_in_l, tm).reshape(xp.shape[0], s, -1)
            y_abc = local_mix(proj3, conv_a_w[l], conv_a_b[l], norm_a_g[l], norm_a_b[l],
                              pool_bd, pool_scale[l], conv_c_w[l], ts)
            y_d = attention(proj3, later, tq, (6 * w) // LANES)
            x1 = out_ln(y_abc.reshape(tp, 3 * w), y_d.reshape(tp, -1), xpf, w_out_l,
                        ln1_g[l], ln1_b[l], alpha, tm)
            idx, gates = route(x1, wq_t, keys, min(256, s))
            return x1, idx, gates

        def on_sparsecore(x1, idx, gates):
            f = sc_peer(x1.reshape(-1), idx.reshape(-1), gates.reshape(-1), words)
            return x1, f.reshape(-1, d)

        early = [on_sparsecore(*front(x[i:i + 1])) for i in range(b - 1, b_tc - 1, -1)]
        x1, idx, gates = front(x[:b_tc])
        t_tc = x1.shape[0] - (_sc_spill(x1.shape[0]) if early else 0)
        late = [on_sparsecore(x1[t_tc:], idx[t_tc:], gates[t_tc:])] if t_tc < x1.shape[0] else []
        outs = [peer(x1[:t_tc], idx[:t_tc], gates[:t_tc], words[:, None, :], ln2_g[l], ln2_b[l], alpha, 16)]
        for x1_sc, f_sc in late + early[::-1]:
            outs.append(residual_ln(x1_sc, f_sc, ln2_g[l], ln2_b[l], alpha, _row_tile(x1_sc.shape[0])))
        x = jnp.concatenate(outs, axis=0).reshape(b, s, d)
    return x
```

```python
import functools

import jax
import jax.numpy as jnp
from jax import lax
from jax.experimental import pallas as pl
from jax.experimental.pallas import tpu as pltpu

F32 = jnp.float32
BF16 = jnp.bfloat16
I32 = jnp.int32

LN_EPS = 1e-5
LANES = 128
CONF_WIDTH = 31
POOL_WINDOWS = (2, 4, 8, 16)
SCONV_WIDTH = 3
HEAD_DIM = 64
PEER_HEADS = 8
PEER_NKEYS = 128
PEER_TOPK = 16
HALO = 32
ATTN_UNDERFLOW = -100.0
VMEM_LIMIT = 56 * 1024 * 1024


def _cparams(*sem):
    return pltpu.CompilerParams(dimension_semantics=sem, vmem_limit_bytes=VMEM_LIMIT)


def _layer_norm(h, g, b):
    mu = jnp.mean(h, axis=-1, keepdims=True)
    hc = h - mu
    var = jnp.mean(hc * hc, axis=-1, keepdims=True)
    return hc * lax.rsqrt(var + LN_EPS) * g + b


def _dot_nt(a, b):
    return lax.dot_general(a, b, (((1,), (1,)), ((), ())), preferred_element_type=F32)


def _in_proj_kernel(x_ref, w_ref, o_ref):
    o_ref[...] = jnp.dot(x_ref[...].astype(BF16), w_ref[...], preferred_element_type=F32)


def in_proj(x2d, w_bf16, tm):
    t, d = x2d.shape
    n = w_bf16.shape[1]
    return pl.pallas_call(
        _in_proj_kernel,
        grid=(t // tm,),
        in_specs=[pl.BlockSpec((tm, d), lambda i: (i, 0)),
                  pl.BlockSpec((d, n), lambda i: (0, 0))],
        out_specs=pl.BlockSpec((tm, n), lambda i: (i, 0)),
        out_shape=jax.ShapeDtypeStruct((t, n), F32),
        compiler_params=_cparams("parallel"),
        name="in_proj",
    )(x2d, w_bf16)


def _local_mix_kernel(p_ref, caw_ref, cab_ref, nag_ref, nab_ref, pw_ref, ps_ref, ccw_ref,
                      o_ref, hist, *, ts, w):
    s = pl.program_id(1)

    @pl.when(s == 0)
    def _():
        hist[0:HALO, :] = jnp.zeros((HALO, 3 * w), F32)

    a_val = p_ref[:, 0:w]
    a_gate = p_ref[:, w:2 * w]
    c_h = p_ref[:, 3 * w:4 * w]
    c_gate_b = p_ref[:, 4 * w:5 * w]
    c_gate_c = p_ref[:, 5 * w:6 * w]
    hist[HALO:HALO + ts, 0:w] = a_val * jax.nn.sigmoid(a_gate)
    hist[HALO:HALO + ts, w:2 * w] = p_ref[:, 2 * w:3 * w]
    hist[HALO:HALO + ts, 2 * w:3 * w] = c_gate_c * c_h

    def past(col, back):
        return hist[HALO - back:HALO - back + ts, col * w:(col + 1) * w]

    acc = jnp.zeros((ts, w), F32)
    for k in range(CONF_WIDTH):
        acc = acc + caw_ref[k:k + 1, :] * past(0, CONF_WIDTH - 1 - k)
    h = _layer_norm(acc + cab_ref[...], nag_ref[...], nab_ref[...])
    o_ref[:, 0:w] = h * jax.nn.sigmoid(h)

    t_pos = s * ts + lax.broadcasted_iota(I32, (ts, 1), 0)
    chan = lax.broadcasted_iota(I32, (ts, w), 1)
    group = w // len(POOL_WINDOWS)
    x0 = past(1, 0)
    run = x0
    pooled = jnp.zeros((ts, w), F32)
    done = 1
    for g, win in enumerate(POOL_WINDOWS):
        for back in range(done, win):
            run = run + past(1, back)
        done = win
        inv = 1.0 / jnp.minimum(t_pos + 1, win).astype(F32)
        in_group = (chan >= g * group) & (chan < (g + 1) * group)
        pooled = jnp.where(in_group, run * inv, pooled)
    pooled = pooled - x0
    mixed = jnp.dot(pooled.astype(BF16), pw_ref[...], preferred_element_type=F32)
    o_ref[:, w:2 * w] = mixed * ps_ref[...]

    conv = jnp.zeros((ts, w), F32)
    for k in range(SCONV_WIDTH):
        conv = conv + ccw_ref[k:k + 1, :] * past(2, SCONV_WIDTH - 1 - k)
    o_ref[:, 2 * w:3 * w] = c_gate_b * conv

    hist[0:HALO, :] = hist[ts:ts + HALO, :]


def local_mix(proj3, conv_a_w, conv_a_b, norm_a_g, norm_a_b, pool_bd, pool_scale, conv_c_w, ts):
    b, s, _ = proj3.shape
    w = conv_a_w.shape[1]
    row = lambda a: a.reshape(1, w)
    full = lambda shape: pl.BlockSpec(shape, lambda bi, si: (0,) * len(shape))
    return pl.pallas_call(
        functools.partial(_local_mix_kernel, ts=ts, w=w),
        grid=(b, s // ts),
        in_specs=[pl.BlockSpec((None, ts, 6 * w), lambda bi, si: (bi, si, 0)),
                  full((CONF_WIDTH, w)), full((1, w)), full((1, w)), full((1, w)),
                  full((w, w)), full((1, w)), full((SCONV_WIDTH, w))],
        out_specs=pl.BlockSpec((None, ts, 3 * w), lambda bi, si: (bi, si, 0)),
        out_shape=jax.ShapeDtypeStruct((b, s, 3 * w), F32),
        scratch_shapes=[pltpu.VMEM((HALO + ts, 3 * w), F32)],
        compiler_params=_cparams("arbitrary", "arbitrary"),
        name="local_mix",
    )(proj3, conv_a_w, row(conv_a_b), row(norm_a_g), row(norm_a_b), pool_bd, row(pool_scale), conv_c_w)


def _attention_kernel(q_ref, k_ref, v_ref, m_ref, o_ref, acc_ref, car_ref, *, tq, scale):
    i = pl.program_id(2)
    q = q_ref[...]
    lane = lax.broadcasted_iota(I32, (tq, LANES), 1)
    q_heads = (jnp.where(lane < HEAD_DIM, q, 0.0).astype(BF16),
               jnp.where(lane >= HEAD_DIM, q, 0.0).astype(BF16))
    acc_ref[...] = jnp.zeros(acc_ref.shape, F32)
    car_ref[...] = jnp.zeros(car_ref.shape, F32)
    row = lax.broadcasted_iota(I32, (tq, tq), 0)
    col = lax.broadcasted_iota(I32, (tq, tq), 1)
    causal = col < row
    later = m_ref[...]

    def prepare(j, diagonal):
        start = pl.multiple_of(j * tq, tq)
        kb = k_ref[pl.ds(start, tq), :].astype(BF16)
        vb = v_ref[pl.ds(start, tq), :].astype(BF16)
        heads = []
        for h in range(2):
            z = _dot_nt(q_heads[h], kb) * scale
            softplus = jnp.maximum(z, 0.0) + jnp.log1p(jnp.exp(-jnp.abs(z)))
            log_stay = -softplus
            log_att = z - softplus
            if diagonal:
                log_stay = jnp.where(causal, log_stay, 0.0)
            hi = log_stay.astype(BF16)
            lo = (log_stay - hi.astype(F32)).astype(BF16)
            inside = (jnp.dot(hi, later, preferred_element_type=F32)
                      + jnp.dot(lo, later, preferred_element_type=F32))
            heads.append((log_att + inside, jnp.sum(log_stay, axis=1, keepdims=True)))
        return heads, vb

    def apply(prepared, diagonal):
        heads, vb = prepared
        for h, (log_w, stay_sum) in enumerate(heads):
            carried = car_ref[h]
            wgt = jnp.exp(log_w + carried)
            if diagonal:
                wgt = jnp.where(causal, wgt, 0.0)
            acc_ref[h] += jnp.dot(wgt.astype(BF16), vb, preferred_element_type=F32)
            car_ref[h] = carried + stay_sum

    @pl.when(i == 0)
    def _():
        apply(prepare(i, True), True)

    @pl.when(i > 0)
    def _():
        diagonal, neighbour = prepare(i, True), prepare(i - 1, False)
        apply(diagonal, True)
        apply(neighbour, False)

    def cond(c):
        j, worst = c
        return (j >= 0) & (worst > ATTN_UNDERFLOW)

    def body(c):
        j, _ = c
        apply(prepare(j, False), False)
        return j - 1, jnp.max(car_ref[...])

    lax.while_loop(cond, body, (i - 2, jnp.max(car_ref[...])))
    o_ref[...] = jnp.where(lane < HEAD_DIM, acc_ref[0], acc_ref[1])


def attention(proj3, later, tq, col0):
    b, s, _ = proj3.shape
    pairs = 2
    scale = HEAD_DIM ** -0.5
    return pl.pallas_call(
        functools.partial(_attention_kernel, tq=tq, scale=scale),
        grid=(b, pairs, s // tq),
        in_specs=[pl.BlockSpec((None, tq, LANES), lambda bi, p, i: (bi, i, col0 + p)),
                  pl.BlockSpec((None, s, LANES), lambda bi, p, i: (bi, 0, col0 + pairs + p)),
                  pl.BlockSpec((None, s, LANES), lambda bi, p, i: (bi, 0, col0 + 2 * pairs + p)),
                  pl.BlockSpec((tq, tq), lambda bi, p, i: (0, 0))],
        out_specs=pl.BlockSpec((None, tq, LANES), lambda bi, p, i: (bi, i, p)),
        out_shape=jax.ShapeDtypeStruct((b, s, pairs * LANES), F32),
        scratch_shapes=[pltpu.VMEM((2, tq, LANES), F32), pltpu.VMEM((2, tq, 1), F32)],
        compiler_params=_cparams("parallel", "parallel", "arbitrary"),
        name="attention",
    )(proj3, proj3, proj3, later)


def _out_ln_kernel(ya_ref, yd_ref, x_ref, w_ref, g_ref, b_ref, o_ref, *, alpha, wa):
    m = jnp.dot(ya_ref[...].astype(BF16), w_ref[0:wa, :], preferred_element_type=F32)
    m = m + jnp.dot(yd_ref[...].astype(BF16), w_ref[wa:, :], preferred_element_type=F32)
    o_ref[...] = _layer_norm(alpha * x_ref[...] + m, g_ref[...], b_ref[...])


def out_ln(y_abc, y_d, x2d, w_out_bf16, g, b, alpha, tm):
    t, d = x2d.shape
    wa, wd = y_abc.shape[1], y_d.shape[1]
    return pl.pallas_call(
        functools.partial(_out_ln_kernel, alpha=alpha, wa=wa),
        grid=(t // tm,),
        in_specs=[pl.BlockSpec((tm, wa), lambda i: (i, 0)),
                  pl.BlockSpec((tm, wd), lambda i: (i, 0)),
                  pl.BlockSpec((tm, d), lambda i: (i, 0)),
                  pl.BlockSpec((wa + wd, d), lambda i: (0, 0)),
                  pl.BlockSpec((1, d), lambda i: (0, 0)),
                  pl.BlockSpec((1, d), lambda i: (0, 0))],
        out_specs=pl.BlockSpec((tm, d), lambda i: (i, 0)),
        out_shape=jax.ShapeDtypeStruct((t, d), F32),
        compiler_params=_cparams("parallel"),
        name="out_ln",
    )(y_abc, y_d, x2d, w_out_bf16, g.reshape(1, d), b.reshape(1, d))


NEVER = 1e9


def _top_k_rows(vals, order, payload, k):
    out_v, out_p = [], []
    for _ in range(k):
        best = jnp.max(vals, axis=0, keepdims=True)
        first = jnp.min(jnp.where(vals == best, order, NEVER), axis=0, keepdims=True)
        hit = order == first
        out_v.append(best)
        if payload is None:
            out_p.append(first)
        else:
            out_p.append(jnp.max(jnp.where(hit, payload, -1.0), axis=0, keepdims=True))
        vals = jnp.where(hit, -jnp.inf, vals)
    return out_v, out_p


def _stack(rows):
    return jnp.concatenate(rows, axis=0)


def _pair_candidates(s0, i0, s1, i1, tm):
    k = PEER_TOPK
    sub = 8
    b_all = lax.broadcasted_iota(I32, (k, tm), 0).astype(F32)
    b_low = lax.broadcasted_iota(I32, (sub, tm), 0).astype(F32)
    s1_low, i1_low = _stack(s1[:sub]), _stack(i1[:sub])
    s0_low, i0_low = _stack(s0[:sub]), _stack(i0[:sub])
    vals, order, ident = [], [], []

    def add(v, o, e, keep=None):
        if keep is not None:
            v = jnp.where(keep, v, -jnp.inf)
            o = jnp.where(keep, o, NEVER)
        vals.append(v)
        order.append(o)
        ident.append(e)

    add(s0[0] + _stack(s1), b_all, i0[0] * PEER_NKEYS + _stack(i1))
    full_rows = 4
    for a in range(1, full_rows + 1):
        add(s0[a] + s1_low, a * k + b_low, i0[a] * PEER_NKEYS + i1_low)
    add(_stack(s0[sub:]) + s1[0], (b_low + sub) * k, _stack(i0[sub:]) * PEER_NKEYS + i1[0])
    rest = b_low > full_rows
    for b in range(2):
        add(s0_low + s1[b], b_low * k + b, i0_low * PEER_NKEYS + i1[b], rest)
    return _stack(vals), _stack(order), _stack(ident)


def _route_kernel(x_ref, wq_ref, keys_ref, idx_ref, gate_ref, *, tm):
    half = PEER_NKEYS
    qt = _dot_nt(wq_ref[...], x_ref[...].astype(BF16)).astype(BF16)
    key_id = lax.broadcasted_iota(I32, (PEER_NKEYS, tm), 0).astype(F32)
    idx_rows, gate_rows = [], []
    for h in range(PEER_HEADS):
        tops = []
        for p in range(2):
            hp = 2 * h + p
            scores = jnp.dot(keys_ref[hp], qt[hp * half:(hp + 1) * half, :], preferred_element_type=F32)
            tops.append(_top_k_rows(scores, key_id, None, PEER_TOPK))
        (s0, i0), (s1, i1) = tops
        cand_s, cand_order, cand_e = _pair_candidates(s0, i0, s1, i1, tm)
        best_s, best_e = _top_k_rows(cand_s, cand_order, cand_e, PEER_TOPK)
        e = jnp.exp(_stack(best_s) - best_s[0])
        gate_rows.append(e / jnp.sum(e, axis=0, keepdims=True))
        idx_rows.extend(best_e)
    idx_ref[...] = _stack(idx_rows).astype(I32).T
    gate_ref[...] = _stack(gate_rows).T


def route(x2d, wq_t_bf16, keys_bf16, tm):
    t, d = x2d.shape
    nq = wq_t_bf16.shape[0]
    sel = PEER_HEADS * PEER_TOPK
    return pl.pallas_call(
        functools.partial(_route_kernel, tm=tm),
        grid=(t // tm,),
        in_specs=[pl.BlockSpec((tm, d), lambda i: (i, 0)),
                  pl.BlockSpec((nq, d), lambda i: (0, 0)),
                  pl.BlockSpec(keys_bf16.shape, lambda i: (0, 0, 0))],
        out_specs=[pl.BlockSpec((tm, sel), lambda i: (i, 0)),
                   pl.BlockSpec((tm, sel), lambda i: (i, 0))],
        out_shape=[jax.ShapeDtypeStruct((t, sel), I32), jax.ShapeDtypeStruct((t, sel), F32)],
        compiler_params=_cparams("parallel"),
        name="route",
    )(x2d, wq_t_bf16, keys_bf16)


PEER_PIECES = 4
MXU_ROWS = 8


def _peer_kernel(idx_now, idx_next, x_ref, gate_ref, g_ref, b_ref, tab_ref, o_ref, rows, sems,
                 *, c, sel, d, alpha):
    i = pl.program_id(0)
    n = pl.num_programs(0)
    piece = d // PEER_PIECES
    per = sel // (2 * PEER_PIECES)

    def start_rows(idx_ref, src_row, half, tk, first):
        for j in range(first, first + per):
            pltpu.make_async_copy(tab_ref.at[idx_ref[src_row, j]], rows.at[half, tk, pl.ds(j, 1), :],
                                  sems.at[half, tk]).start(priority=j % 2)

    def wait_token(half, tk):
        pltpu.make_async_copy(rows.at[1 - half, tk], rows.at[half, tk], sems.at[half, tk]).wait()

    @pl.when(i == 0)
    def _():
        for tk in range(c):
            for first in range(0, sel, per):
                start_rows(idx_now, tk, 0, tk, first)

    def score_piece(tok, xb, q):
        half, tk = divmod(tok, c)
        cols = slice(q * piece, (q + 1) * piece)
        u = lax.bitcast_convert_type(rows[half, tk, :, cols] << 16, F32).astype(BF16)
        return _dot_nt(xb[:, cols], u)

    def mix_piece(tok, act, q):
        half, tk = divmod(tok, c)
        cols = slice(q * piece, (q + 1) * piece)
        v = lax.bitcast_convert_type(rows[half, tk, :, cols] & jnp.uint32(0xFFFF0000), F32).astype(BF16)
        return jnp.dot(act, v, preferred_element_type=F32)

    groups = 2 * PEER_PIECES
    scored = None
    mixed = None
    for k in range(2 * c + 2):
        tok = k if k < 2 * c else None
        if tok is not None:
            half, tk = divmod(tok, c)
            ahead_ref, ahead_row = (idx_now, tok + c) if half == 0 else (idx_next, tk)
            wait_token(half, tk)
            x = jnp.broadcast_to(x_ref[tok:tok + 1, :], (MXU_ROWS, d))
            xb = x.astype(BF16)
        y_done = None
        if mixed is not None:
            m_tok, m_x, m_f = mixed
            y_done = (m_tok, _layer_norm(alpha * m_x + m_f, g_ref[...], b_ref[...]))
        act = None
        if scored is not None:
            s_tok, s_x, s_score = scored
            act = (jax.nn.gelu(s_score) * gate_ref[s_tok:s_tok + 1, :]).astype(BF16)
        score = jnp.zeros((MXU_ROWS, sel), F32)
        parts = []
        issued = 0
        for batch in range(2):
            for q in range(batch * PEER_PIECES // 2, (batch + 1) * PEER_PIECES // 2):
                if act is not None:
                    parts.append(mix_piece(s_tok, act, q))
                if tok is not None:
                    score = score + score_piece(tok, xb, q)
            if tok is not None:
                for _ in range(groups // 4 if batch == 0 else groups - groups // 4):
                    start_rows(ahead_ref, ahead_row, 1 - half, tk, issued * per)
                    issued += 1
        if y_done is not None:
            o_ref[y_done[0]:y_done[0] + 1, :] = y_done[1][0:1, :]
        mixed = (s_tok, s_x, jnp.concatenate(parts, axis=1)) if act is not None else None
        scored = (tok, x, score) if tok is not None else None

    @pl.when(i == n - 1)
    def _():
        for tk in range(c):
            wait_token(0, tk)


def peer(x2d, idx, gates, table, g, b, alpha, c):
    t, d = x2d.shape
    sel = idx.shape[1]
    n = t // (2 * c)
    return pl.pallas_call(
        functools.partial(_peer_kernel, c=c, sel=sel, d=d, alpha=alpha),
        grid=(n,),
        in_specs=[pl.BlockSpec((2 * c, sel), lambda i: (i, 0), memory_space=pltpu.SMEM),
                  pl.BlockSpec((2 * c, sel), lambda i: (jnp.minimum(i + 1, n - 1), 0), memory_space=pltpu.SMEM),
                  pl.BlockSpec((2 * c, d), lambda i: (i, 0)),
                  pl.BlockSpec((2 * c, sel), lambda i: (i, 0)),
                  pl.BlockSpec((1, d), lambda i: (0, 0)),
                  pl.BlockSpec((1, d), lambda i: (0, 0)),
                  pl.BlockSpec(memory_space=pl.ANY)],
        out_specs=pl.BlockSpec((2 * c, d), lambda i: (i, 0)),
        out_shape=jax.ShapeDtypeStruct((t, d), F32),
        scratch_shapes=[pltpu.VMEM((2, c, sel, d), jnp.uint32), pltpu.SemaphoreType.DMA((2, c))],
        compiler_params=_cparams("arbitrary"),
        name="peer",
    )(idx, idx, x2d, gates, g.reshape(1, d), b.reshape(1, d), table)


SC_LANES = 16
SC_WORKERS = 32


def _sc_peer_body(x_hbm, idx_hbm, gate_hbm, tab_hbm, f_hbm, x_v, idx_v, gate_v, out_v, rows_v,
                  row_sems, in_sems, out_sems, *, per_worker, d, sel):
    heads = sel // PEER_TOPK
    chunks = d // SC_LANES
    wid = lax.axis_index("s") * 2 + lax.axis_index("c")
    first = wid * per_worker
    last = first + per_worker - 1

    def gather(p, h):
        return pltpu.make_async_copy(tab_hbm.at[idx_v.at[p, pl.ds(h * PEER_TOPK, PEER_TOPK)]],
                                     rows_v.at[h % 2], row_sems.at[h % 2])

    def inputs(g, p):
        return (pltpu.make_async_copy(idx_hbm.at[pl.ds(g * sel, sel)], idx_v.at[p], in_sems.at[p, 0]),
                pltpu.make_async_copy(gate_hbm.at[pl.ds(g * sel, sel)], gate_v.at[p], in_sems.at[p, 1]),
                pltpu.make_async_copy(x_hbm.at[pl.ds(g * d, d)], x_v.at[p], in_sems.at[p, 2]))

    def result(g, p):
        return pltpu.make_async_copy(out_v.at[p], f_hbm.at[pl.ds(g * d, d)], out_sems.at[p])

    def token(g, p, have_older):
        following = jnp.minimum(g + 1, last)
        for copy in inputs(following, 1 - p):
            copy.start()

        @pl.when(have_older)
        def _():
            result(g, p).wait()

        def clear(ci, c2):
            out_v[p, pl.ds(ci * SC_LANES, SC_LANES)] = jnp.zeros((SC_LANES,), F32)
            return c2
        lax.fori_loop(0, chunks, clear, 0)

        for h in range(heads):
            slot = h % 2
            if h + 1 < heads:
                gather(p, h + 1).start()
            else:
                for copy in inputs(following, 1 - p):
                    copy.wait()
                gather(1 - p, 0).start()
            gather(p, h).wait()

            def dots(ci, accs):
                xc = x_v[p, pl.ds(ci * SC_LANES, SC_LANES)]
                out = []
                for r in range(PEER_TOPK):
                    w = rows_v[slot, r, pl.ds(ci * SC_LANES, SC_LANES)]
                    out.append(accs[r] + lax.bitcast_convert_type(w << 16, F32) * xc)
                return tuple(out)
            accs = lax.fori_loop(0, chunks, dots, tuple(jnp.zeros((SC_LANES,), F32) for _ in range(PEER_TOPK)))
            lane = lax.iota(I32, SC_LANES)
            score = jnp.zeros((SC_LANES,), F32)
            for r in range(PEER_TOPK):
                score = jnp.where(lane == r, jnp.sum(accs[r]), score)
            inner = 0.7978845608028654 * (score + 0.044715 * score * score * score)
            tanh = 1.0 - 2.0 / (jnp.exp(2.0 * inner) + 1.0)
            act = 0.5 * score * (1.0 + tanh) * gate_v[p, pl.ds(h * PEER_TOPK, PEER_TOPK)]
            weights = [jnp.sum(jnp.where(lane == r, act, 0.0)) for r in range(PEER_TOPK)]

            def mix(ci, c2):
                part = [out_v[p, pl.ds(ci * SC_LANES, SC_LANES)]] + [jnp.zeros((SC_LANES,), F32)] * 3
                for r in range(PEER_TOPK):
                    w = rows_v[slot, r, pl.ds(ci * SC_LANES, SC_LANES)]
                    part[r % 4] = part[r % 4] + weights[r] * lax.bitcast_convert_type(
                        w & jnp.uint32(0xFFFF0000), F32)
                out_v[p, pl.ds(ci * SC_LANES, SC_LANES)] = (part[0] + part[1]) + (part[2] + part[3])
                return c2
            lax.fori_loop(0, chunks, mix, 0)
        result(g, p).start()

    for copy in inputs(first, 0):
        copy.start()
    for copy in inputs(first, 0):
        copy.wait()
    gather(0, 0).start()

    def pair(k, carry):
        token(first + 2 * k, 0, k > 0)
        token(first + 2 * k + 1, 1, k > 0)
        return carry
    lax.fori_loop(0, per_worker // 2, pair, 0)

    gather(0, 0).wait()
    result(last, 0).wait()
    result(last, 1).wait()


def sc_peer(x_flat, idx_flat, gate_flat, table2d):
    from jax.experimental.pallas import tpu_sc as plsc
    d = table2d.shape[1]
    tokens = x_flat.shape[0] // d
    sel = idx_flat.shape[0] // tokens
    per_worker = tokens // SC_WORKERS
    mesh = plsc.VectorSubcoreMesh(core_axis_name="c", subcore_axis_name="s")
    return pl.kernel(
        functools.partial(_sc_peer_body, per_worker=per_worker, d=d, sel=sel),
        out_type=jax.ShapeDtypeStruct((tokens * d,), F32),
        mesh=mesh,
        scratch_types=[pltpu.VMEM((2, d), F32), pltpu.VMEM((2, sel), I32), pltpu.VMEM((2, sel), F32),
                       pltpu.VMEM((2, d), F32), pltpu.VMEM((2, PEER_TOPK, d), jnp.uint32),
                       pltpu.SemaphoreType.DMA((2,)), pltpu.SemaphoreType.DMA((2, 3)),
                       pltpu.SemaphoreType.DMA((2,))],
        compiler_params=pltpu.CompilerParams(needs_layout_passes=False),
        name="sc_peer",
    )(x_flat, idx_flat, gate_flat, table2d)


def _block_diag(pool_w):
    g, c, _ = pool_w.shape
    out = jnp.zeros((g * c, g * c), pool_w.dtype)
    for k in range(g):
        out = out.at[k * c:(k + 1) * c, k * c:(k + 1) * c].set(pool_w[k])
    return out


def _pack_rows(u, v):
    bits = lambda a: lax.bitcast_convert_type(a.astype(BF16), jnp.uint16).astype(jnp.uint32)
    return bits(u) | (bits(v) << 16)


def _residual_ln_kernel(x_ref, f_ref, g_ref, b_ref, o_ref, *, alpha):
    o_ref[...] = _layer_norm(alpha * x_ref[...] + f_ref[...], g_ref[...], b_ref[...])


def residual_ln(x2d, f2d, g, b, alpha, tm):
    t, d = x2d.shape
    return pl.pallas_call(
        functools.partial(_residual_ln_kernel, alpha=alpha),
        grid=(t // tm,),
        in_specs=[pl.BlockSpec((tm, d), lambda i: (i, 0)),
                  pl.BlockSpec((tm, d), lambda i: (i, 0)),
                  pl.BlockSpec((1, d), lambda i: (0, 0)),
                  pl.BlockSpec((1, d), lambda i: (0, 0))],
        out_specs=pl.BlockSpec((tm, d), lambda i: (i, 0)),
        out_shape=jax.ShapeDtypeStruct((t, d), F32),
        compiler_params=_cparams("parallel"),
        name="residual_ln",
    )(x2d, f2d, g.reshape(1, d), b.reshape(1, d))


def _sc_batches(b):
    return b // 2


def kernel(x, w_in, conv_a_w, conv_a_b, norm_a_g, norm_a_b, pool_w, pool_scale, conv_c_w, w_out,
           ln1_g, ln1_b, peer_wq, peer_keys, peer_u, peer_v, ln2_g, ln2_b):
    depth = w_in.shape[0]
    b, s, d = x.shape
    alpha = float((2 * depth) ** 0.25)
    w = conv_a_w.shape[2]
    tq = min(256, s)
    ts = min(512, s)
    tm = min(512, s)
    row = lax.broadcasted_iota(I32, (tq, tq), 0)
    col = lax.broadcasted_iota(I32, (tq, tq), 1)
    later = (row > col).astype(BF16)
    b_tc = b - _sc_batches(b)
    for l in range(depth):
        w_in_l, w_out_l = w_in[l].astype(BF16), w_out[l].astype(BF16)
        pool_bd = _block_diag(pool_w[l]).astype(BF16)
        keys = peer_keys[l].reshape(2 * PEER_HEADS, PEER_NKEYS, -1).astype(BF16)
        wq_t = peer_wq[l].T.astype(BF16)
        words = _pack_rows(peer_u[l], peer_v[l])

        def front(xp):
            tp = xp.shape[0] * s
            xpf = xp.reshape(tp, d)
            proj3 = in_proj(xpf, w_in_l, tm).reshape(xp.shape[0], s, -1)
            y_abc = local_mix(proj3, conv_a_w[l], conv_a_b[l], norm_a_g[l], norm_a_b[l],
                              pool_bd, pool_scale[l], conv_c_w[l], ts)
            y_d = attention(proj3, later, tq, (6 * w) // LANES)
            x1 = out_ln(y_abc.reshape(tp, 3 * w), y_d.reshape(tp, -1), xpf, w_out_l,
                        ln1_g[l], ln1_b[l], alpha, tm)
            idx, gates = route(x1, wq_t, keys, min(256, s))
            return x1, idx, gates

        outs = []
        if b_tc < b:
            x1_sc, idx_sc, gates_sc = front(x[b_tc:])
            f_sc = sc_peer(x1_sc.reshape(-1), idx_sc.reshape(-1), gates_sc.reshape(-1), words)
        x1_tc, idx_tc, gates_tc = front(x[:b_tc])
        outs.append(peer(x1_tc, idx_tc, gates_tc, words[:, None, :], ln2_g[l], ln2_b[l], alpha, 16))
        if b_tc < b:
            outs.append(residual_ln(x1_sc, f_sc.reshape(-1, d), ln2_g[l], ln2_b[l], alpha, tm))
        x = jnp.concatenate(outs, axis=0).reshape(b, s, d)
    return x
```

```python
import functools

import jax
import jax.numpy as jnp
from jax import lax
from jax.experimental import pallas as pl
from jax.experimental.pallas import tpu as pltpu

F32 = jnp.float32
BF16 = jnp.bfloat16
I32 = jnp.int32

LN_EPS = 1e-5
LANES = 128
CONF_WIDTH = 31
POOL_WINDOWS = (2, 4, 8, 16)
SCONV_WIDTH = 3
HEAD_DIM = 64
PEER_HEADS = 8
PEER_NKEYS = 128
PEER_TOPK = 16
HALO = 32
ATTN_UNDERFLOW = -100.0
VMEM_LIMIT = 56 * 1024 * 1024


def _cparams(*sem):
    return pltpu.CompilerParams(dimension_semantics=sem, vmem_limit_bytes=VMEM_LIMIT)


def _layer_norm(h, g, b):
    mu = jnp.mean(h, axis=-1, keepdims=True)
    hc = h - mu
    var = jnp.mean(hc * hc, axis=-1, keepdims=True)
    return hc * lax.rsqrt(var + LN_EPS) * g + b


def _dot_nt(a, b):
    return lax.dot_general(a, b, (((1,), (1,)), ((), ())), preferred_element_type=F32)


def _in_proj_kernel(x_ref, w_ref, o_ref):
    o_ref[...] = jnp.dot(x_ref[...].astype(BF16), w_ref[...], preferred_element_type=F32)


def in_proj(x2d, w_bf16, tm):
    t, d = x2d.shape
    n = w_bf16.shape[1]
    return pl.pallas_call(
        _in_proj_kernel,
        grid=(t // tm,),
        in_specs=[pl.BlockSpec((tm, d), lambda i: (i, 0)),
                  pl.BlockSpec((d, n), lambda i: (0, 0))],
        out_specs=pl.BlockSpec((tm, n), lambda i: (i, 0)),
        out_shape=jax.ShapeDtypeStruct((t, n), F32),
        compiler_params=_cparams("parallel"),
        name="in_proj",
    )(x2d, w_bf16)


def _local_mix_kernel(p_ref, caw_ref, cab_ref, nag_ref, nab_ref, pw_ref, ps_ref, ccw_ref,
                      o_ref, hist, *, ts, w):
    s = pl.program_id(1)

    @pl.when(s == 0)
    def _():
        hist[0:HALO, :] = jnp.zeros((HALO, 3 * w), F32)

    a_val = p_ref[:, 0:w]
    a_gate = p_ref[:, w:2 * w]
    c_h = p_ref[:, 3 * w:4 * w]
    c_gate_b = p_ref[:, 4 * w:5 * w]
    c_gate_c = p_ref[:, 5 * w:6 * w]
    hist[HALO:HALO + ts, 0:w] = a_val * jax.nn.sigmoid(a_gate)
    hist[HALO:HALO + ts, w:2 * w] = p_ref[:, 2 * w:3 * w]
    hist[HALO:HALO + ts, 2 * w:3 * w] = c_gate_c * c_h

    def past(col, back):
        return hist[HALO - back:HALO - back + ts, col * w:(col + 1) * w]

    acc = jnp.zeros((ts, w), F32)
    for k in range(CONF_WIDTH):
        acc = acc + caw_ref[k:k + 1, :] * past(0, CONF_WIDTH - 1 - k)
    h = _layer_norm(acc + cab_ref[...], nag_ref[...], nab_ref[...])
    o_ref[:, 0:w] = h * jax.nn.sigmoid(h)

    t_pos = s * ts + lax.broadcasted_iota(I32, (ts, 1), 0)
    chan = lax.broadcasted_iota(I32, (ts, w), 1)
    group = w // len(POOL_WINDOWS)
    x0 = past(1, 0)
    run = x0
    pooled = jnp.zeros((ts, w), F32)
    done = 1
    for g, win in enumerate(POOL_WINDOWS):
        for back in range(done, win):
            run = run + past(1, back)
        done = win
        inv = 1.0 / jnp.minimum(t_pos + 1, win).astype(F32)
        in_group = (chan >= g * group) & (chan < (g + 1) * group)
        pooled = jnp.where(in_group, run * inv, pooled)
    pooled = pooled - x0
    mixed = jnp.dot(pooled.astype(BF16), pw_ref[...], preferred_element_type=F32)
    o_ref[:, w:2 * w] = mixed * ps_ref[...]

    conv = jnp.zeros((ts, w), F32)
    for k in range(SCONV_WIDTH):
        conv = conv + ccw_ref[k:k + 1, :] * past(2, SCONV_WIDTH - 1 - k)
    o_ref[:, 2 * w:3 * w] = c_gate_b * conv

    hist[0:HALO, :] = hist[ts:ts + HALO, :]


def local_mix(proj3, conv_a_w, conv_a_b, norm_a_g, norm_a_b, pool_bd, pool_scale, conv_c_w, ts):
    b, s, _ = proj3.shape
    w = conv_a_w.shape[1]
    row = lambda a: a.reshape(1, w)
    full = lambda shape: pl.BlockSpec(shape, lambda bi, si: (0,) * len(shape))
    return pl.pallas_call(
        functools.partial(_local_mix_kernel, ts=ts, w=w),
        grid=(b, s // ts),
        in_specs=[pl.BlockSpec((None, ts, 6 * w), lambda bi, si: (bi, si, 0)),
                  full((CONF_WIDTH, w)), full((1, w)), full((1, w)), full((1, w)),
                  full((w, w)), full((1, w)), full((SCONV_WIDTH, w))],
        out_specs=pl.BlockSpec((None, ts, 3 * w), lambda bi, si: (bi, si, 0)),
        out_shape=jax.ShapeDtypeStruct((b, s, 3 * w), F32),
        scratch_shapes=[pltpu.VMEM((HALO + ts, 3 * w), F32)],
        compiler_params=_cparams("arbitrary", "arbitrary"),
        name="local_mix",
    )(proj3, conv_a_w, row(conv_a_b), row(norm_a_g), row(norm_a_b), pool_bd, row(pool_scale), conv_c_w)


def _attention_kernel(q_ref, k_ref, v_ref, m_ref, o_ref, acc_ref, car_ref, *, tq, scale):
    i = pl.program_id(2)
    q = q_ref[...]
    lane = lax.broadcasted_iota(I32, (tq, LANES), 1)
    q_heads = (jnp.where(lane < HEAD_DIM, q, 0.0).astype(BF16),
               jnp.where(lane >= HEAD_DIM, q, 0.0).astype(BF16))
    acc_ref[...] = jnp.zeros(acc_ref.shape, F32)
    car_ref[...] = jnp.zeros(car_ref.shape, F32)
    row = lax.broadcasted_iota(I32, (tq, tq), 0)
    col = lax.broadcasted_iota(I32, (tq, tq), 1)
    causal = col < row
    later = m_ref[...]

    def prepare(j, diagonal):
        start = pl.multiple_of(j * tq, tq)
        kb = k_ref[pl.ds(start, tq), :].astype(BF16)
        vb = v_ref[pl.ds(start, tq), :].astype(BF16)
        heads = []
        for h in range(2):
            z = _dot_nt(q_heads[h], kb) * scale
            softplus = jnp.maximum(z, 0.0) + jnp.log1p(jnp.exp(-jnp.abs(z)))
            log_stay = -softplus
            log_att = z - softplus
            if diagonal:
                log_stay = jnp.where(causal, log_stay, 0.0)
            hi = log_stay.astype(BF16)
            lo = (log_stay - hi.astype(F32)).astype(BF16)
            inside = (jnp.dot(hi, later, preferred_element_type=F32)
                      + jnp.dot(lo, later, preferred_element_type=F32))
            heads.append((log_att + inside, jnp.sum(log_stay, axis=1, keepdims=True)))
        return heads, vb

    def apply(prepared, diagonal):
        heads, vb = prepared
        for h, (log_w, stay_sum) in enumerate(heads):
            carried = car_ref[h]
            wgt = jnp.exp(log_w + carried)
            if diagonal:
                wgt = jnp.where(causal, wgt, 0.0)
            acc_ref[h] += jnp.dot(wgt.astype(BF16), vb, preferred_element_type=F32)
            car_ref[h] = carried + stay_sum

    @pl.when(i == 0)
    def _():
        apply(prepare(i, True), True)

    @pl.when(i > 0)
    def _():
        diagonal, neighbour = prepare(i, True), prepare(i - 1, False)
        apply(diagonal, True)
        apply(neighbour, False)

    def cond(c):
        j, worst = c
        return (j >= 0) & (worst > ATTN_UNDERFLOW)

    def body(c):
        j, _ = c
        apply(prepare(j, False), False)
        return j - 1, jnp.max(car_ref[...])

    lax.while_loop(cond, body, (i - 2, jnp.max(car_ref[...])))
    o_ref[...] = jnp.where(lane < HEAD_DIM, acc_ref[0], acc_ref[1])


def attention(proj3, later, tq, col0):
    b, s, _ = proj3.shape
    pairs = 2
    scale = HEAD_DIM ** -0.5
    return pl.pallas_call(
        functools.partial(_attention_kernel, tq=tq, scale=scale),
        grid=(b, pairs, s // tq),
        in_specs=[pl.BlockSpec((None, tq, LANES), lambda bi, p, i: (bi, i, col0 + p)),
                  pl.BlockSpec((None, s, LANES), lambda bi, p, i: (bi, 0, col0 + pairs + p)),
                  pl.BlockSpec((None, s, LANES), lambda bi, p, i: (bi, 0, col0 + 2 * pairs + p)),
                  pl.BlockSpec((tq, tq), lambda bi, p, i: (0, 0))],
        out_specs=pl.BlockSpec((None, tq, LANES), lambda bi, p, i: (bi, i, p)),
        out_shape=jax.ShapeDtypeStruct((b, s, pairs * LANES), F32),
        scratch_shapes=[pltpu.VMEM((2, tq, LANES), F32), pltpu.VMEM((2, tq, 1), F32)],
        compiler_params=_cparams("parallel", "parallel", "arbitrary"),
        name="attention",
    )(proj3, proj3, proj3, later)


def _out_ln_kernel(ya_ref, yd_ref, x_ref, w_ref, g_ref, b_ref, o_ref, *, alpha, wa):
    m = jnp.dot(ya_ref[...].astype(BF16), w_ref[0:wa, :], preferred_element_type=F32)
    m = m + jnp.dot(yd_ref[...].astype(BF16), w_ref[wa:, :], preferred_element_type=F32)
    o_ref[...] = _layer_norm(alpha * x_ref[...] + m, g_ref[...], b_ref[...])


def out_ln(y_abc, y_d, x2d, w_out_bf16, g, b, alpha, tm):
    t, d = x2d.shape
    wa, wd = y_abc.shape[1], y_d.shape[1]
    return pl.pallas_call(
        functools.partial(_out_ln_kernel, alpha=alpha, wa=wa),
        grid=(t // tm,),
        in_specs=[pl.BlockSpec((tm, wa), lambda i: (i, 0)),
                  pl.BlockSpec((tm, wd), lambda i: (i, 0)),
                  pl.BlockSpec((tm, d), lambda i: (i, 0)),
                  pl.BlockSpec((wa + wd, d), lambda i: (0, 0)),
                  pl.BlockSpec((1, d), lambda i: (0, 0)),
                  pl.BlockSpec((1, d), lambda i: (0, 0))],
        out_specs=pl.BlockSpec((tm, d), lambda i: (i, 0)),
        out_shape=jax.ShapeDtypeStruct((t, d), F32),
        compiler_params=_cparams("parallel"),
        name="out_ln",
    )(y_abc, y_d, x2d, w_out_bf16, g.reshape(1, d), b.reshape(1, d))


NEVER = 1e9


def _top_k_rows(vals, order, payload, k):
    out_v, out_p = [], []
    for _ in range(k):
        best = jnp.max(vals, axis=0, keepdims=True)
        first = jnp.min(jnp.where(vals == best, order, NEVER), axis=0, keepdims=True)
        hit = order == first
        out_v.append(best)
        if payload is None:
            out_p.append(first)
        else:
            out_p.append(jnp.max(jnp.where(hit, payload, -1.0), axis=0, keepdims=True))
        vals = jnp.where(hit, -jnp.inf, vals)
    return out_v, out_p


def _stack(rows):
    return jnp.concatenate(rows, axis=0)


def _pair_candidates(s0, i0, s1, i1, tm):
    k = PEER_TOPK
    sub = 8
    b_all = lax.broadcasted_iota(I32, (k, tm), 0).astype(F32)
    b_low = lax.broadcasted_iota(I32, (sub, tm), 0).astype(F32)
    s1_low, i1_low = _stack(s1[:sub]), _stack(i1[:sub])
    s0_low, i0_low = _stack(s0[:sub]), _stack(i0[:sub])
    vals, order, ident = [], [], []

    def add(v, o, e, keep=None):
        if keep is not None:
            v = jnp.where(keep, v, -jnp.inf)
            o = jnp.where(keep, o, NEVER)
        vals.append(v)
        order.append(o)
        ident.append(e)

    add(s0[0] + _stack(s1), b_all, i0[0] * PEER_NKEYS + _stack(i1))
    full_rows = 4
    for a in range(1, full_rows + 1):
        add(s0[a] + s1_low, a * k + b_low, i0[a] * PEER_NKEYS + i1_low)
    add(_stack(s0[sub:]) + s1[0], (b_low + sub) * k, _stack(i0[sub:]) * PEER_NKEYS + i1[0])
    rest = b_low > full_rows
    for b in range(2):
        add(s0_low + s1[b], b_low * k + b, i0_low * PEER_NKEYS + i1[b], rest)
    return _stack(vals), _stack(order), _stack(ident)


def _route_kernel(x_ref, wq_ref, keys_ref, idx_ref, gate_ref, *, tm):
    half = PEER_NKEYS
    qt = _dot_nt(wq_ref[...], x_ref[...].astype(BF16)).astype(BF16)
    key_id = lax.broadcasted_iota(I32, (PEER_NKEYS, tm), 0).astype(F32)
    idx_rows, gate_rows = [], []
    for h in range(PEER_HEADS):
        tops = []
        for p in range(2):
            hp = 2 * h + p
            scores = jnp.dot(keys_ref[hp], qt[hp * half:(hp + 1) * half, :], preferred_element_type=F32)
            tops.append(_top_k_rows(scores, key_id, None, PEER_TOPK))
        (s0, i0), (s1, i1) = tops
        cand_s, cand_order, cand_e = _pair_candidates(s0, i0, s1, i1, tm)
        best_s, best_e = _top_k_rows(cand_s, cand_order, cand_e, PEER_TOPK)
        e = jnp.exp(_stack(best_s) - best_s[0])
        gate_rows.append(e / jnp.sum(e, axis=0, keepdims=True))
        idx_rows.extend(best_e)
    idx_ref[...] = _stack(idx_rows).astype(I32).T
    gate_ref[...] = _stack(gate_rows).T


def route(x2d, wq_t_bf16, keys_bf16, tm):
    t, d = x2d.shape
    nq = wq_t_bf16.shape[0]
    sel = PEER_HEADS * PEER_TOPK
    return pl.pallas_call(
        functools.partial(_route_kernel, tm=tm),
        grid=(t // tm,),
        in_specs=[pl.BlockSpec((tm, d), lambda i: (i, 0)),
                  pl.BlockSpec((nq, d), lambda i: (0, 0)),
                  pl.BlockSpec(keys_bf16.shape, lambda i: (0, 0, 0))],
        out_specs=[pl.BlockSpec((tm, sel), lambda i: (i, 0)),
                   pl.BlockSpec((tm, sel), lambda i: (i, 0))],
        out_shape=[jax.ShapeDtypeStruct((t, sel), I32), jax.ShapeDtypeStruct((t, sel), F32)],
        compiler_params=_cparams("parallel"),
        name="route",
    )(x2d, wq_t_bf16, keys_bf16)


PEER_PIECES = 4
MXU_ROWS = 8


def _peer_kernel(idx_now, idx_next, x_ref, gate_ref, g_ref, b_ref, tab_ref, o_ref, rows, sems,
                 *, c, sel, d, alpha):
    i = pl.program_id(0)
    n = pl.num_programs(0)
    piece = d // PEER_PIECES
    per = sel // (2 * PEER_PIECES)

    def start_rows(idx_ref, src_row, half, tk, first):
        for j in range(first, first + per):
            pltpu.make_async_copy(tab_ref.at[idx_ref[src_row, j]], rows.at[half, tk, pl.ds(j, 1), :],
                                  sems.at[half, tk]).start(priority=j % 2)

    def wait_token(half, tk):
        pltpu.make_async_copy(rows.at[1 - half, tk], rows.at[half, tk], sems.at[half, tk]).wait()

    @pl.when(i == 0)
    def _():
        for tk in range(c):
            for first in range(0, sel, per):
                start_rows(idx_now, tk, 0, tk, first)

    def score_piece(tok, xb, q):
        half, tk = divmod(tok, c)
        cols = slice(q * piece, (q + 1) * piece)
        u = lax.bitcast_convert_type(rows[half, tk, :, cols] << 16, F32).astype(BF16)
        return _dot_nt(xb[:, cols], u)

    def mix_piece(tok, act, q):
        half, tk = divmod(tok, c)
        cols = slice(q * piece, (q + 1) * piece)
        v = lax.bitcast_convert_type(rows[half, tk, :, cols] & jnp.uint32(0xFFFF0000), F32).astype(BF16)
        return jnp.dot(act, v, preferred_element_type=F32)

    groups = 2 * PEER_PIECES
    scored = None
    mixed = None
    for k in range(2 * c + 2):
        tok = k if k < 2 * c else None
        if tok is not None:
            half, tk = divmod(tok, c)
            ahead_ref, ahead_row = (idx_now, tok + c) if half == 0 else (idx_next, tk)
            wait_token(half, tk)
            x = jnp.broadcast_to(x_ref[tok:tok + 1, :], (MXU_ROWS, d))
            xb = x.astype(BF16)
        y_done = None
        if mixed is not None:
            m_tok, m_x, m_f = mixed
            y_done = (m_tok, _layer_norm(alpha * m_x + m_f, g_ref[...], b_ref[...]))
        act = None
        if scored is not None:
            s_tok, s_x, s_score = scored
            act = (jax.nn.gelu(s_score) * gate_ref[s_tok:s_tok + 1, :]).astype(BF16)
        score = jnp.zeros((MXU_ROWS, sel), F32)
        parts = []
        issued = 0
        for batch in range(2):
            for q in range(batch * PEER_PIECES // 2, (batch + 1) * PEER_PIECES // 2):
                if act is not None:
                    parts.append(mix_piece(s_tok, act, q))
                if tok is not None:
                    score = score + score_piece(tok, xb, q)
            if tok is not None:
                for _ in range(groups // 4 if batch == 0 else groups - groups // 4):
                    start_rows(ahead_ref, ahead_row, 1 - half, tk, issued * per)
                    issued += 1
        if y_done is not None:
            o_ref[y_done[0]:y_done[0] + 1, :] = y_done[1][0:1, :]
        mixed = (s_tok, s_x, jnp.concatenate(parts, axis=1)) if act is not None else None
        scored = (tok, x, score) if tok is not None else None

    @pl.when(i == n - 1)
    def _():
        for tk in range(c):
            wait_token(0, tk)


def peer(x2d, idx, gates, table, g, b, alpha, c):
    t, d = x2d.shape
    sel = idx.shape[1]
    n = t // (2 * c)
    return pl.pallas_call(
        functools.partial(_peer_kernel, c=c, sel=sel, d=d, alpha=alpha),
        grid=(n,),
        in_specs=[pl.BlockSpec((2 * c, sel), lambda i: (i, 0), memory_space=pltpu.SMEM),
                  pl.BlockSpec((2 * c, sel), lambda i: (jnp.minimum(i + 1, n - 1), 0), memory_space=pltpu.SMEM),
                  pl.BlockSpec((2 * c, d), lambda i: (i, 0)),
                  pl.BlockSpec((2 * c, sel), lambda i: (i, 0)),
                  pl.BlockSpec((1, d), lambda i: (0, 0)),
                  pl.BlockSpec((1, d), lambda i: (0, 0)),
                  pl.BlockSpec(memory_space=pl.ANY)],
        out_specs=pl.BlockSpec((2 * c, d), lambda i: (i, 0)),
        out_shape=jax.ShapeDtypeStruct((t, d), F32),
        scratch_shapes=[pltpu.VMEM((2, c, sel, d), jnp.uint32), pltpu.SemaphoreType.DMA((2, c))],
        compiler_params=_cparams("arbitrary"),
        name="peer",
    )(idx, idx, x2d, gates, g.reshape(1, d), b.reshape(1, d), table)


SC_LANES = 16
SC_WORKERS = 32


def _sc_peer_body(x_hbm, idx_hbm, gate_hbm, u_hbm, v_hbm, f_hbm, x_v, idx_v, gate_v, out_v, u_rows, v_rows,
                  row_sems, in_sems, out_sems, *, per_worker, d, sel):
    heads = sel // PEER_TOPK
    chunks = d // SC_LANES
    wid = lax.axis_index("s") * 2 + lax.axis_index("c")
    first = wid * per_worker
    last = first + per_worker - 1

    class gather:
        def __init__(self, p, h):
            picked = idx_v.at[p, pl.ds(h * PEER_TOPK, PEER_TOPK)]
            self.copies = (pltpu.make_async_copy(u_hbm.at[picked], u_rows.at[h % 2], row_sems.at[h % 2, 0]),
                           pltpu.make_async_copy(v_hbm.at[picked], v_rows.at[h % 2], row_sems.at[h % 2, 1]))

        def start(self):
            for copy in self.copies:
                copy.start()

        def wait(self):
            for copy in self.copies:
                copy.wait()

    def inputs(g, p):
        return (pltpu.make_async_copy(idx_hbm.at[pl.ds(g * sel, sel)], idx_v.at[p], in_sems.at[p, 0]),
                pltpu.make_async_copy(gate_hbm.at[pl.ds(g * sel, sel)], gate_v.at[p], in_sems.at[p, 1]),
                pltpu.make_async_copy(x_hbm.at[pl.ds(g * d, d)], x_v.at[p], in_sems.at[p, 2]))

    def result(g, p):
        return pltpu.make_async_copy(out_v.at[p], f_hbm.at[pl.ds(g * d, d)], out_sems.at[p])

    def token(g, p, have_older):
        following = jnp.minimum(g + 1, last)
        for copy in inputs(following, 1 - p):
            copy.start()

        @pl.when(have_older)
        def _():
            result(g, p).wait()

        def clear(ci, c2):
            out_v[p, pl.ds(ci * SC_LANES, SC_LANES)] = jnp.zeros((SC_LANES,), F32)
            return c2
        lax.fori_loop(0, chunks, clear, 0)

        for h in range(heads):
            slot = h % 2
            if h + 1 < heads:
                gather(p, h + 1).start()
            else:
                for copy in inputs(following, 1 - p):
                    copy.wait()
                gather(1 - p, 0).start()
            gather(p, h).wait()

            def dots(ci, accs):
                xc = x_v[p, pl.ds(ci * SC_LANES, SC_LANES)]
                out = []
                for r in range(PEER_TOPK):
                    out.append(accs[r] + u_rows[slot, r, pl.ds(ci * SC_LANES, SC_LANES)] * xc)
                return tuple(out)
            accs = lax.fori_loop(0, chunks, dots, tuple(jnp.zeros((SC_LANES,), F32) for _ in range(PEER_TOPK)))
            lane = lax.iota(I32, SC_LANES)
            score = jnp.zeros((SC_LANES,), F32)
            for r in range(PEER_TOPK):
                score = jnp.where(lane == r, jnp.sum(accs[r]), score)
            inner = 0.7978845608028654 * (score + 0.044715 * score * score * score)
            tanh = 1.0 - 2.0 / (jnp.exp(2.0 * inner) + 1.0)
            act = 0.5 * score * (1.0 + tanh) * gate_v[p, pl.ds(h * PEER_TOPK, PEER_TOPK)]
            weights = [jnp.sum(jnp.where(lane == r, act, 0.0)) for r in range(PEER_TOPK)]

            def mix(ci, c2):
                part = [out_v[p, pl.ds(ci * SC_LANES, SC_LANES)]] + [jnp.zeros((SC_LANES,), F32)] * 3
                for r in range(PEER_TOPK):
                    part[r % 4] = part[r % 4] + weights[r] * v_rows[slot, r, pl.ds(ci * SC_LANES, SC_LANES)]
                out_v[p, pl.ds(ci * SC_LANES, SC_LANES)] = (part[0] + part[1]) + (part[2] + part[3])
                return c2
            lax.fori_loop(0, chunks, mix, 0)
        result(g, p).start()

    for copy in inputs(first, 0):
        copy.start()
    for copy in inputs(first, 0):
        copy.wait()
    gather(0, 0).start()

    def pair(k, carry):
        token(first + 2 * k, 0, k > 0)
        token(first + 2 * k + 1, 1, k > 0)
        return carry
    lax.fori_loop(0, per_worker // 2, pair, 0)

    gather(0, 0).wait()
    result(last, 0).wait()
    result(last, 1).wait()


def sc_peer(x_flat, idx_flat, gate_flat, u_table, v_table):
    from jax.experimental.pallas import tpu_sc as plsc
    d = u_table.shape[1]
    tokens = x_flat.shape[0] // d
    sel = idx_flat.shape[0] // tokens
    per_worker = tokens // SC_WORKERS
    mesh = plsc.VectorSubcoreMesh(core_axis_name="c", subcore_axis_name="s")
    return pl.kernel(
        functools.partial(_sc_peer_body, per_worker=per_worker, d=d, sel=sel),
        out_type=jax.ShapeDtypeStruct((tokens * d,), F32),
        mesh=mesh,
        scratch_types=[pltpu.VMEM((2, d), F32), pltpu.VMEM((2, sel), I32), pltpu.VMEM((2, sel), F32),
                       pltpu.VMEM((2, d), F32), pltpu.VMEM((2, PEER_TOPK, d), F32),
                       pltpu.VMEM((2, PEER_TOPK, d), F32),
                       pltpu.SemaphoreType.DMA((2, 2)), pltpu.SemaphoreType.DMA((2, 3)),
                       pltpu.SemaphoreType.DMA((2,))],
        compiler_params=pltpu.CompilerParams(needs_layout_passes=False),
        name="sc_peer",
    )(x_flat, idx_flat, gate_flat, u_table, v_table)


def _block_diag(pool_w):
    g, c, _ = pool_w.shape
    out = jnp.zeros((g * c, g * c), pool_w.dtype)
    for k in range(g):
        out = out.at[k * c:(k + 1) * c, k * c:(k + 1) * c].set(pool_w[k])
    return out


def _pack_rows(u, v):
    bits = lambda a: lax.bitcast_convert_type(a.astype(BF16), jnp.uint16).astype(jnp.uint32)
    return bits(u) | (bits(v) << 16)


def _residual_ln_kernel(x_ref, f_ref, g_ref, b_ref, o_ref, *, alpha):
    o_ref[...] = _layer_norm(alpha * x_ref[...] + f_ref[...], g_ref[...], b_ref[...])


def residual_ln(x2d, f2d, g, b, alpha, tm):
    t, d = x2d.shape
    return pl.pallas_call(
        functools.partial(_residual_ln_kernel, alpha=alpha),
        grid=(t // tm,),
        in_specs=[pl.BlockSpec((tm, d), lambda i: (i, 0)),
                  pl.BlockSpec((tm, d), lambda i: (i, 0)),
                  pl.BlockSpec((1, d), lambda i: (0, 0)),
                  pl.BlockSpec((1, d), lambda i: (0, 0))],
        out_specs=pl.BlockSpec((tm, d), lambda i: (i, 0)),
        out_shape=jax.ShapeDtypeStruct((t, d), F32),
        compiler_params=_cparams("parallel"),
        name="residual_ln",
    )(x2d, f2d, g.reshape(1, d), b.reshape(1, d))


def _sc_batches(b):
    return b // 2


def kernel(x, w_in, conv_a_w, conv_a_b, norm_a_g, norm_a_b, pool_w, pool_scale, conv_c_w, w_out,
           ln1_g, ln1_b, peer_wq, peer_keys, peer_u, peer_v, ln2_g, ln2_b):
    depth = w_in.shape[0]
    b, s, d = x.shape
    alpha = float((2 * depth) ** 0.25)
    w = conv_a_w.shape[2]
    tq = min(256, s)
    ts = min(512, s)
    tm = min(512, s)
    row = lax.broadcasted_iota(I32, (tq, tq), 0)
    col = lax.broadcasted_iota(I32, (tq, tq), 1)
    later = (row > col).astype(BF16)
    b_tc = b - _sc_batches(b)
    for l in range(depth):
        w_in_l, w_out_l = w_in[l].astype(BF16), w_out[l].astype(BF16)
        pool_bd = _block_diag(pool_w[l]).astype(BF16)
        keys = peer_keys[l].reshape(2 * PEER_HEADS, PEER_NKEYS, -1).astype(BF16)
        wq_t = peer_wq[l].T.astype(BF16)
        words = _pack_rows(peer_u[l], peer_v[l])

        def front(xp):
            tp = xp.shape[0] * s
            xpf = xp.reshape(tp, d)
            proj3 = in_proj(xpf, w_in_l, tm).reshape(xp.shape[0], s, -1)
            y_abc = local_mix(proj3, conv_a_w[l], conv_a_b[l], norm_a_g[l], norm_a_b[l],
                              pool_bd, pool_scale[l], conv_c_w[l], ts)
            y_d = attention(proj3, later, tq, (6 * w) // LANES)
            x1 = out_ln(y_abc.reshape(tp, 3 * w), y_d.reshape(tp, -1), xpf, w_out_l,
                        ln1_g[l], ln1_b[l], alpha, tm)
            idx, gates = route(x1, wq_t, keys, min(256, s))
            return x1, idx, gates

        outs = []
        if b_tc < b:
            x1_sc, idx_sc, gates_sc = front(x[b_tc:])
            f_sc = sc_peer(x1_sc.reshape(-1), idx_sc.reshape(-1), gates_sc.reshape(-1), peer_u[l], peer_v[l])
        x1_tc, idx_tc, gates_tc = front(x[:b_tc])
        outs.append(peer(x1_tc, idx_tc, gates_tc, words[:, None, :], ln2_g[l], ln2_b[l], alpha, 16))
        if b_tc < b:
            outs.append(residual_ln(x1_sc, f_sc.reshape(-1, d), ln2_g[l], ln2_b[l], alpha, tm))
        x = jnp.concatenate(outs, axis=0).reshape(b, s, d)
    return x
```

```python
import functools

import jax
import jax.numpy as jnp
from jax import lax
from jax.experimental import pallas as pl
from jax.experimental.pallas import tpu as pltpu

F32 = jnp.float32
BF16 = jnp.bfloat16
I32 = jnp.int32

LN_EPS = 1e-5
LANES = 128
CONF_WIDTH = 31
POOL_WINDOWS = (2, 4, 8, 16)
SCONV_WIDTH = 3
HEAD_DIM = 64
PEER_HEADS = 8
PEER_NKEYS = 128
PEER_TOPK = 16
HALO = 32
ATTN_UNDERFLOW = -100.0
VMEM_LIMIT = 56 * 1024 * 1024


def _cparams(*sem):
    return pltpu.CompilerParams(dimension_semantics=sem, vmem_limit_bytes=VMEM_LIMIT)


def _layer_norm(h, g, b):
    mu = jnp.mean(h, axis=-1, keepdims=True)
    hc = h - mu
    var = jnp.mean(hc * hc, axis=-1, keepdims=True)
    return hc * lax.rsqrt(var + LN_EPS) * g + b


def _dot_nt(a, b):
    return lax.dot_general(a, b, (((1,), (1,)), ((), ())), preferred_element_type=F32)


def _in_proj_kernel(x_ref, w_ref, o_ref):
    o_ref[...] = jnp.dot(x_ref[...].astype(BF16), w_ref[...], preferred_element_type=F32)


def in_proj(x2d, w_bf16, tm):
    t, d = x2d.shape
    n = w_bf16.shape[1]
    return pl.pallas_call(
        _in_proj_kernel,
        grid=(t // tm,),
        in_specs=[pl.BlockSpec((tm, d), lambda i: (i, 0)),
                  pl.BlockSpec((d, n), lambda i: (0, 0))],
        out_specs=pl.BlockSpec((tm, n), lambda i: (i, 0)),
        out_shape=jax.ShapeDtypeStruct((t, n), F32),
        compiler_params=_cparams("parallel"),
        name="in_proj",
    )(x2d, w_bf16)


def _local_mix_kernel(p_ref, caw_ref, cab_ref, nag_ref, nab_ref, pw_ref, ps_ref, ccw_ref,
                      o_ref, hist, *, ts, w):
    s = pl.program_id(1)

    @pl.when(s == 0)
    def _():
        hist[0:HALO, :] = jnp.zeros((HALO, 3 * w), F32)

    a_val = p_ref[:, 0:w]
    a_gate = p_ref[:, w:2 * w]
    c_h = p_ref[:, 3 * w:4 * w]
    c_gate_b = p_ref[:, 4 * w:5 * w]
    c_gate_c = p_ref[:, 5 * w:6 * w]
    hist[HALO:HALO + ts, 0:w] = a_val * jax.nn.sigmoid(a_gate)
    hist[HALO:HALO + ts, w:2 * w] = p_ref[:, 2 * w:3 * w]
    hist[HALO:HALO + ts, 2 * w:3 * w] = c_gate_c * c_h

    def past(col, back):
        return hist[HALO - back:HALO - back + ts, col * w:(col + 1) * w]

    acc = jnp.zeros((ts, w), F32)
    for k in range(CONF_WIDTH):
        acc = acc + caw_ref[k:k + 1, :] * past(0, CONF_WIDTH - 1 - k)
    h = _layer_norm(acc + cab_ref[...], nag_ref[...], nab_ref[...])
    o_ref[:, 0:w] = h * jax.nn.sigmoid(h)

    t_pos = s * ts + lax.broadcasted_iota(I32, (ts, 1), 0)
    chan = lax.broadcasted_iota(I32, (ts, w), 1)
    group = w // len(POOL_WINDOWS)
    x0 = past(1, 0)
    run = x0
    pooled = jnp.zeros((ts, w), F32)
    done = 1
    for g, win in enumerate(POOL_WINDOWS):
        for back in range(done, win):
            run = run + past(1, back)
        done = win
        inv = 1.0 / jnp.minimum(t_pos + 1, win).astype(F32)
        in_group = (chan >= g * group) & (chan < (g + 1) * group)
        pooled = jnp.where(in_group, run * inv, pooled)
    pooled = pooled - x0
    mixed = jnp.dot(pooled.astype(BF16), pw_ref[...], preferred_element_type=F32)
    o_ref[:, w:2 * w] = mixed * ps_ref[...]

    conv = jnp.zeros((ts, w), F32)
    for k in range(SCONV_WIDTH):
        conv = conv + ccw_ref[k:k + 1, :] * past(2, SCONV_WIDTH - 1 - k)
    o_ref[:, 2 * w:3 * w] = c_gate_b * conv

    hist[0:HALO, :] = hist[ts:ts + HALO, :]


def local_mix(proj3, conv_a_w, conv_a_b, norm_a_g, norm_a_b, pool_bd, pool_scale, conv_c_w, ts):
    b, s, _ = proj3.shape
    w = conv_a_w.shape[1]
    row = lambda a: a.reshape(1, w)
    full = lambda shape: pl.BlockSpec(shape, lambda bi, si: (0,) * len(shape))
    return pl.pallas_call(
        functools.partial(_local_mix_kernel, ts=ts, w=w),
        grid=(b, s // ts),
        in_specs=[pl.BlockSpec((None, ts, 6 * w), lambda bi, si: (bi, si, 0)),
                  full((CONF_WIDTH, w)), full((1, w)), full((1, w)), full((1, w)),
                  full((w, w)), full((1, w)), full((SCONV_WIDTH, w))],
        out_specs=pl.BlockSpec((None, ts, 3 * w), lambda bi, si: (bi, si, 0)),
        out_shape=jax.ShapeDtypeStruct((b, s, 3 * w), F32),
        scratch_shapes=[pltpu.VMEM((HALO + ts, 3 * w), F32)],
        compiler_params=_cparams("arbitrary", "arbitrary"),
        name="local_mix",
    )(proj3, conv_a_w, row(conv_a_b), row(norm_a_g), row(norm_a_b), pool_bd, row(pool_scale), conv_c_w)


def _attention_kernel(q_ref, k_ref, v_ref, m_ref, o_ref, acc_ref, car_ref, *, tq, scale):
    i = pl.program_id(2)
    q = q_ref[...]
    lane = lax.broadcasted_iota(I32, (tq, LANES), 1)
    q_heads = (jnp.where(lane < HEAD_DIM, q, 0.0).astype(BF16),
               jnp.where(lane >= HEAD_DIM, q, 0.0).astype(BF16))
    acc_ref[...] = jnp.zeros(acc_ref.shape, F32)
    car_ref[...] = jnp.zeros(car_ref.shape, F32)
    row = lax.broadcasted_iota(I32, (tq, tq), 0)
    col = lax.broadcasted_iota(I32, (tq, tq), 1)
    causal = col < row
    later = m_ref[...]

    def prepare(j, diagonal):
        start = pl.multiple_of(j * tq, tq)
        kb = k_ref[pl.ds(start, tq), :].astype(BF16)
        vb = v_ref[pl.ds(start, tq), :].astype(BF16)
        heads = []
        for h in range(2):
            z = _dot_nt(q_heads[h], kb) * scale
            softplus = jnp.maximum(z, 0.0) + jnp.log1p(jnp.exp(-jnp.abs(z)))
            log_stay = -softplus
            log_att = z - softplus
            if diagonal:
                log_stay = jnp.where(causal, log_stay, 0.0)
            hi = log_stay.astype(BF16)
            lo = (log_stay - hi.astype(F32)).astype(BF16)
            inside = (jnp.dot(hi, later, preferred_element_type=F32)
                      + jnp.dot(lo, later, preferred_element_type=F32))
            heads.append((log_att + inside, jnp.sum(log_stay, axis=1, keepdims=True)))
        return heads, vb

    def apply(prepared, diagonal):
        heads, vb = prepared
        for h, (log_w, stay_sum) in enumerate(heads):
            carried = car_ref[h]
            wgt = jnp.exp(log_w + carried)
            if diagonal:
                wgt = jnp.where(causal, wgt, 0.0)
            acc_ref[h] += jnp.dot(wgt.astype(BF16), vb, preferred_element_type=F32)
            car_ref[h] = carried + stay_sum

    @pl.when(i == 0)
    def _():
        apply(prepare(i, True), True)

    @pl.when(i > 0)
    def _():
        diagonal, neighbour = prepare(i, True), prepare(i - 1, False)
        apply(diagonal, True)
        apply(neighbour, False)

    def cond(c):
        j, worst = c
        return (j >= 0) & (worst > ATTN_UNDERFLOW)

    def body(c):
        j, _ = c
        apply(prepare(j, False), False)
        return j - 1, jnp.max(car_ref[...])

    lax.while_loop(cond, body, (i - 2, jnp.max(car_ref[...])))
    o_ref[...] = jnp.where(lane < HEAD_DIM, acc_ref[0], acc_ref[1])


def attention(proj3, later, tq, col0):
    b, s, _ = proj3.shape
    pairs = 2
    scale = HEAD_DIM ** -0.5
    return pl.pallas_call(
        functools.partial(_attention_kernel, tq=tq, scale=scale),
        grid=(b, pairs, s // tq),
        in_specs=[pl.BlockSpec((None, tq, LANES), lambda bi, p, i: (bi, i, col0 + p)),
                  pl.BlockSpec((None, s, LANES), lambda bi, p, i: (bi, 0, col0 + pairs + p)),
                  pl.BlockSpec((None, s, LANES), lambda bi, p, i: (bi, 0, col0 + 2 * pairs + p)),
                  pl.BlockSpec((tq, tq), lambda bi, p, i: (0, 0))],
        out_specs=pl.BlockSpec((None, tq, LANES), lambda bi, p, i: (bi, i, p)),
        out_shape=jax.ShapeDtypeStruct((b, s, pairs * LANES), F32),
        scratch_shapes=[pltpu.VMEM((2, tq, LANES), F32), pltpu.VMEM((2, tq, 1), F32)],
        compiler_params=_cparams("parallel", "parallel", "arbitrary"),
        name="attention",
    )(proj3, proj3, proj3, later)


def _out_ln_kernel(ya_ref, yd_ref, x_ref, w_ref, g_ref, b_ref, o_ref, *, alpha, wa):
    m = jnp.dot(ya_ref[...].astype(BF16), w_ref[0:wa, :], preferred_element_type=F32)
    m = m + jnp.dot(yd_ref[...].astype(BF16), w_ref[wa:, :], preferred_element_type=F32)
    o_ref[...] = _layer_norm(alpha * x_ref[...] + m, g_ref[...], b_ref[...])


def out_ln(y_abc, y_d, x2d, w_out_bf16, g, b, alpha, tm):
    t, d = x2d.shape
    wa, wd = y_abc.shape[1], y_d.shape[1]
    return pl.pallas_call(
        functools.partial(_out_ln_kernel, alpha=alpha, wa=wa),
        grid=(t // tm,),
        in_specs=[pl.BlockSpec((tm, wa), lambda i: (i, 0)),
                  pl.BlockSpec((tm, wd), lambda i: (i, 0)),
                  pl.BlockSpec((tm, d), lambda i: (i, 0)),
                  pl.BlockSpec((wa + wd, d), lambda i: (0, 0)),
                  pl.BlockSpec((1, d), lambda i: (0, 0)),
                  pl.BlockSpec((1, d), lambda i: (0, 0))],
        out_specs=pl.BlockSpec((tm, d), lambda i: (i, 0)),
        out_shape=jax.ShapeDtypeStruct((t, d), F32),
        compiler_params=_cparams("parallel"),
        name="out_ln",
    )(y_abc, y_d, x2d, w_out_bf16, g.reshape(1, d), b.reshape(1, d))


NEVER = 1e9


def _top_k_rows(vals, order, payload, k):
    out_v, out_p = [], []
    for _ in range(k):
        best = jnp.max(vals, axis=0, keepdims=True)
        first = jnp.min(jnp.where(vals == best, order, NEVER), axis=0, keepdims=True)
        hit = order == first
        out_v.append(best)
        if payload is None:
            out_p.append(first)
        else:
            out_p.append(jnp.max(jnp.where(hit, payload, -1.0), axis=0, keepdims=True))
        vals = jnp.where(hit, -jnp.inf, vals)
    return out_v, out_p


def _stack(rows):
    return jnp.concatenate(rows, axis=0)


def _merge_sort_network(n):
    pairs = []

    def merge(lo, size, r):
        step = 2 * r
        if step < size:
            merge(lo, size, step)
            merge(lo + r, size, step)
            pairs.extend((i, i + r) for i in range(lo + r, lo + size - r, step))
        else:
            pairs.append((lo, lo + r))

    def sort(lo, size):
        if size > 1:
            sort(lo, size // 2)
            sort(lo + size // 2, size // 2)
            merge(lo, size, 1)

    sort(0, n)
    return pairs


def _top_k_keys(scores, k):
    n, m = scores.shape
    sub = 8
    slabs = n // sub
    base = lax.broadcasted_iota(I32, (sub, m), 0).astype(F32)
    vals = [scores[v * sub:(v + 1) * sub, :] for v in range(slabs)]
    ids = [base + float(v * sub) for v in range(slabs)]
    for i, j in _merge_sort_network(slabs):
        better = (vals[j] > vals[i]) | ((vals[j] == vals[i]) & (ids[j] < ids[i]))
        vals[i], vals[j] = jnp.where(better, vals[j], vals[i]), jnp.where(better, vals[i], vals[j])
        ids[i], ids[j] = jnp.where(better, ids[j], ids[i]), jnp.where(better, ids[i], ids[j])
    out_v, out_i = [], []
    for step in range(k):
        best = jnp.max(vals[0], axis=0, keepdims=True)
        first = jnp.min(jnp.where(vals[0] == best, ids[0], NEVER), axis=0, keepdims=True)
        out_v.append(best)
        out_i.append(first)
        hit = ids[0] == first
        for depth in range(k - 1 - step):
            vals[depth] = jnp.where(hit, vals[depth + 1], vals[depth])
            ids[depth] = jnp.where(hit, ids[depth + 1], ids[depth])
    return out_v, out_i


def _pair_candidates(s0, i0, s1, i1, tm):
    k = PEER_TOPK
    sub = 8
    b_all = lax.broadcasted_iota(I32, (k, tm), 0).astype(F32)
    b_low = lax.broadcasted_iota(I32, (sub, tm), 0).astype(F32)
    s1_low, i1_low = _stack(s1[:sub]), _stack(i1[:sub])
    s0_low, i0_low = _stack(s0[:sub]), _stack(i0[:sub])
    vals, order, ident = [], [], []

    def add(v, o, e, keep=None):
        if keep is not None:
            v = jnp.where(keep, v, -jnp.inf)
            o = jnp.where(keep, o, NEVER)
        vals.append(v)
        order.append(o)
        ident.append(e)

    add(s0[0] + _stack(s1), b_all, i0[0] * PEER_NKEYS + _stack(i1))
    full_rows = 4
    for a in range(1, full_rows + 1):
        add(s0[a] + s1_low, a * k + b_low, i0[a] * PEER_NKEYS + i1_low)
    add(_stack(s0[sub:]) + s1[0], (b_low + sub) * k, _stack(i0[sub:]) * PEER_NKEYS + i1[0])
    rest = b_low > full_rows
    for b in range(2):
        add(s0_low + s1[b], b_low * k + b, i0_low * PEER_NKEYS + i1[b], rest)
    return _stack(vals), _stack(order), _stack(ident)


def _route_kernel(x_ref, wq_ref, keys_ref, idx_ref, gate_ref, *, tm):
    half = PEER_NKEYS
    qt = _dot_nt(wq_ref[...], x_ref[...].astype(BF16)).astype(BF16)
    idx_rows, gate_rows = [], []
    for h in range(PEER_HEADS):
        tops = []
        for p in range(2):
            hp = 2 * h + p
            scores = jnp.dot(keys_ref[hp], qt[hp * half:(hp + 1) * half, :], preferred_element_type=F32)
            tops.append(_top_k_keys(scores, PEER_TOPK))
        (s0, i0), (s1, i1) = tops
        cand_s, cand_order, cand_e = _pair_candidates(s0, i0, s1, i1, tm)
        best_s, best_e = _top_k_rows(cand_s, cand_order, cand_e, PEER_TOPK)
        e = jnp.exp(_stack(best_s) - best_s[0])
        gate_rows.append(e / jnp.sum(e, axis=0, keepdims=True))
        idx_rows.extend(best_e)
    idx_ref[...] = _stack(idx_rows).astype(I32).T
    gate_ref[...] = _stack(gate_rows).T


def route(x2d, wq_t_bf16, keys_bf16, tm):
    t, d = x2d.shape
    nq = wq_t_bf16.shape[0]
    sel = PEER_HEADS * PEER_TOPK
    return pl.pallas_call(
        functools.partial(_route_kernel, tm=tm),
        grid=(t // tm,),
        in_specs=[pl.BlockSpec((tm, d), lambda i: (i, 0)),
                  pl.BlockSpec((nq, d), lambda i: (0, 0)),
                  pl.BlockSpec(keys_bf16.shape, lambda i: (0, 0, 0))],
        out_specs=[pl.BlockSpec((tm, sel), lambda i: (i, 0)),
                   pl.BlockSpec((tm, sel), lambda i: (i, 0))],
        out_shape=[jax.ShapeDtypeStruct((t, sel), I32), jax.ShapeDtypeStruct((t, sel), F32)],
        compiler_params=_cparams("parallel"),
        name="route",
    )(x2d, wq_t_bf16, keys_bf16)


PEER_PIECES = 4
MXU_ROWS = 8


def _peer_kernel(idx_now, idx_next, x_ref, gate_ref, g_ref, b_ref, tab_ref, o_ref, rows, sems,
                 *, c, sel, d, alpha):
    i = pl.program_id(0)
    n = pl.num_programs(0)
    piece = d // PEER_PIECES
    per = sel // (2 * PEER_PIECES)

    def start_rows(idx_ref, src_row, half, tk, first):
        for j in range(first, first + per):
            pltpu.make_async_copy(tab_ref.at[idx_ref[src_row, j]], rows.at[half, tk, pl.ds(j, 1), :],
                                  sems.at[half, tk]).start(priority=j % 2)

    def wait_token(half, tk):
        pltpu.make_async_copy(rows.at[1 - half, tk], rows.at[half, tk], sems.at[half, tk]).wait()

    @pl.when(i == 0)
    def _():
        for tk in range(c):
            for first in range(0, sel, per):
                start_rows(idx_now, tk, 0, tk, first)

    def score_piece(tok, xb, q):
        half, tk = divmod(tok, c)
        cols = slice(q * piece, (q + 1) * piece)
        u = lax.bitcast_convert_type(rows[half, tk, :, cols] << 16, F32).astype(BF16)
        return _dot_nt(xb[:, cols], u)

    def mix_piece(tok, act, q):
        half, tk = divmod(tok, c)
        cols = slice(q * piece, (q + 1) * piece)
        v = lax.bitcast_convert_type(rows[half, tk, :, cols] & jnp.uint32(0xFFFF0000), F32).astype(BF16)
        return jnp.dot(act, v, preferred_element_type=F32)

    groups = 2 * PEER_PIECES
    scored = None
    mixed = None
    for k in range(2 * c + 2):
        tok = k if k < 2 * c else None
        if tok is not None:
            half, tk = divmod(tok, c)
            ahead_ref, ahead_row = (idx_now, tok + c) if half == 0 else (idx_next, tk)
            wait_token(half, tk)
            x = jnp.broadcast_to(x_ref[tok:tok + 1, :], (MXU_ROWS, d))
            xb = x.astype(BF16)
        y_done = None
        if mixed is not None:
            m_tok, m_x, m_f = mixed
            y_done = (m_tok, _layer_norm(alpha * m_x + m_f, g_ref[...], b_ref[...]))
        act = None
        if scored is not None:
            s_tok, s_x, s_score = scored
            act = (jax.nn.gelu(s_score) * gate_ref[s_tok:s_tok + 1, :]).astype(BF16)
        score = jnp.zeros((MXU_ROWS, sel), F32)
        parts = []
        issued = 0
        for batch in range(2):
            for q in range(batch * PEER_PIECES // 2, (batch + 1) * PEER_PIECES // 2):
                if act is not None:
                    parts.append(mix_piece(s_tok, act, q))
                if tok is not None:
                    score = score + score_piece(tok, xb, q)
            if tok is not None:
                for _ in range(groups // 4 if batch == 0 else groups - groups // 4):
                    start_rows(ahead_ref, ahead_row, 1 - half, tk, issued * per)
                    issued += 1
        if y_done is not None:
            o_ref[y_done[0]:y_done[0] + 1, :] = y_done[1][0:1, :]
        mixed = (s_tok, s_x, jnp.concatenate(parts, axis=1)) if act is not None else None
        scored = (tok, x, score) if tok is not None else None

    @pl.when(i == n - 1)
    def _():
        for tk in range(c):
            wait_token(0, tk)


def peer(x2d, idx, gates, table, g, b, alpha, c):
    t, d = x2d.shape
    sel = idx.shape[1]
    n = t // (2 * c)
    return pl.pallas_call(
        functools.partial(_peer_kernel, c=c, sel=sel, d=d, alpha=alpha),
        grid=(n,),
        in_specs=[pl.BlockSpec((2 * c, sel), lambda i: (i, 0), memory_space=pltpu.SMEM),
                  pl.BlockSpec((2 * c, sel), lambda i: (jnp.minimum(i + 1, n - 1), 0), memory_space=pltpu.SMEM),
                  pl.BlockSpec((2 * c, d), lambda i: (i, 0)),
                  pl.BlockSpec((2 * c, sel), lambda i: (i, 0)),
                  pl.BlockSpec((1, d), lambda i: (0, 0)),
                  pl.BlockSpec((1, d), lambda i: (0, 0)),
                  pl.BlockSpec(memory_space=pl.ANY)],
        out_specs=pl.BlockSpec((2 * c, d), lambda i: (i, 0)),
        out_shape=jax.ShapeDtypeStruct((t, d), F32),
        scratch_shapes=[pltpu.VMEM((2, c, sel, d), jnp.uint32), pltpu.SemaphoreType.DMA((2, c))],
        compiler_params=_cparams("arbitrary"),
        name="peer",
    )(idx, idx, x2d, gates, g.reshape(1, d), b.reshape(1, d), table)


SC_LANES = 16
SC_WORKERS = 32


def _sc_peer_body(x_hbm, idx_hbm, gate_hbm, tab_hbm, f_hbm, x_v, idx_v, gate_v, out_v, rows_v,
                  row_sems, in_sems, out_sems, *, per_worker, d, sel):
    heads = sel // PEER_TOPK
    chunks = d // SC_LANES
    wid = lax.axis_index("s") * 2 + lax.axis_index("c")
    first = wid * per_worker
    last = first + per_worker - 1

    def gather(p, h):
        return pltpu.make_async_copy(tab_hbm.at[idx_v.at[p, pl.ds(h * PEER_TOPK, PEER_TOPK)]],
                                     rows_v.at[h % 2], row_sems.at[h % 2])

    def inputs(g, p):
        return (pltpu.make_async_copy(idx_hbm.at[pl.ds(g * sel, sel)], idx_v.at[p], in_sems.at[p, 0]),
                pltpu.make_async_copy(gate_hbm.at[pl.ds(g * sel, sel)], gate_v.at[p], in_sems.at[p, 1]),
                pltpu.make_async_copy(x_hbm.at[pl.ds(g * d, d)], x_v.at[p], in_sems.at[p, 2]))

    def result(g, p):
        return pltpu.make_async_copy(out_v.at[p], f_hbm.at[pl.ds(g * d, d)], out_sems.at[p])

    def token(g, p, have_older):
        following = jnp.minimum(g + 1, last)
        for copy in inputs(following, 1 - p):
            copy.start()

        @pl.when(have_older)
        def _():
            result(g, p).wait()

        def clear(ci, c2):
            out_v[p, pl.ds(ci * SC_LANES, SC_LANES)] = jnp.zeros((SC_LANES,), F32)
            return c2
        lax.fori_loop(0, chunks, clear, 0)

        for h in range(heads):
            slot = h % 2
            if h + 1 < heads:
                gather(p, h + 1).start()
            else:
                for copy in inputs(following, 1 - p):
                    copy.wait()
                gather(1 - p, 0).start()
            gather(p, h).wait()

            def dots(ci, accs):
                xc = x_v[p, pl.ds(ci * SC_LANES, SC_LANES)]
                out = []
                for r in range(PEER_TOPK):
                    w = rows_v[slot, r, pl.ds(ci * SC_LANES, SC_LANES)]
                    out.append(accs[r] + lax.bitcast_convert_type(w << 16, F32) * xc)
                return tuple(out)
            accs = lax.fori_loop(0, chunks, dots, tuple(jnp.zeros((SC_LANES,), F32) for _ in range(PEER_TOPK)))
            lane = lax.iota(I32, SC_LANES)
            score = jnp.zeros((SC_LANES,), F32)
            for r in range(PEER_TOPK):
                score = jnp.where(lane == r, jnp.sum(accs[r]), score)
            inner = 0.7978845608028654 * (score + 0.044715 * score * score * score)
            tanh = 1.0 - 2.0 / (jnp.exp(2.0 * inner) + 1.0)
            act = 0.5 * score * (1.0 + tanh) * gate_v[p, pl.ds(h * PEER_TOPK, PEER_TOPK)]
            weights = [jnp.sum(jnp.where(lane == r, act, 0.0)) for r in range(PEER_TOPK)]

            def mix(ci, c2):
                part = [out_v[p, pl.ds(ci * SC_LANES, SC_LANES)]] + [jnp.zeros((SC_LANES,), F32)] * 3
                for r in range(PEER_TOPK):
                    w = rows_v[slot, r, pl.ds(ci * SC_LANES, SC_LANES)]
                    part[r % 4] = part[r % 4] + weights[r] * lax.bitcast_convert_type(
                        w & jnp.uint32(0xFFFF0000), F32)
                out_v[p, pl.ds(ci * SC_LANES, SC_LANES)] = (part[0] + part[1]) + (part[2] + part[3])
                return c2
            lax.fori_loop(0, chunks, mix, 0)
        result(g, p).start()

    for copy in inputs(first, 0):
        copy.start()
    for copy in inputs(first, 0):
        copy.wait()
    gather(0, 0).start()

    def pair(k, carry):
        token(first + 2 * k, 0, k > 0)
        token(first + 2 * k + 1, 1, k > 0)
        return carry
    lax.fori_loop(0, per_worker // 2, pair, 0)

    gather(0, 0).wait()
    result(last, 0).wait()
    result(last, 1).wait()


def sc_peer(x_flat, idx_flat, gate_flat, table2d):
    from jax.experimental.pallas import tpu_sc as plsc
    d = table2d.shape[1]
    tokens = x_flat.shape[0] // d
    sel = idx_flat.shape[0] // tokens
    per_worker = tokens // SC_WORKERS
    mesh = plsc.VectorSubcoreMesh(core_axis_name="c", subcore_axis_name="s")
    return pl.kernel(
        functools.partial(_sc_peer_body, per_worker=per_worker, d=d, sel=sel),
        out_type=jax.ShapeDtypeStruct((tokens * d,), F32),
        mesh=mesh,
        scratch_types=[pltpu.VMEM((2, d), F32), pltpu.VMEM((2, sel), I32), pltpu.VMEM((2, sel), F32),
                       pltpu.VMEM((2, d), F32), pltpu.VMEM((2, PEER_TOPK, d), jnp.uint32),
                       pltpu.SemaphoreType.DMA((2,)), pltpu.SemaphoreType.DMA((2, 3)),
                       pltpu.SemaphoreType.DMA((2,))],
        compiler_params=pltpu.CompilerParams(needs_layout_passes=False),
        name="sc_peer",
    )(x_flat, idx_flat, gate_flat, table2d)


def _block_diag(pool_w):
    g, c, _ = pool_w.shape
    out = jnp.zeros((g * c, g * c), pool_w.dtype)
    for k in range(g):
        out = out.at[k * c:(k + 1) * c, k * c:(k + 1) * c].set(pool_w[k])
    return out


def _pack_rows(u, v):
    bits = lambda a: lax.bitcast_convert_type(a.astype(BF16), jnp.uint16).astype(jnp.uint32)
    return bits(u) | (bits(v) << 16)


def _residual_ln_kernel(x_ref, f_ref, g_ref, b_ref, o_ref, *, alpha):
    o_ref[...] = _layer_norm(alpha * x_ref[...] + f_ref[...], g_ref[...], b_ref[...])


def residual_ln(x2d, f2d, g, b, alpha, tm):
    t, d = x2d.shape
    return pl.pallas_call(
        functools.partial(_residual_ln_kernel, alpha=alpha),
        grid=(t // tm,),
        in_specs=[pl.BlockSpec((tm, d), lambda i: (i, 0)),
                  pl.BlockSpec((tm, d), lambda i: (i, 0)),
                  pl.BlockSpec((1, d), lambda i: (0, 0)),
                  pl.BlockSpec((1, d), lambda i: (0, 0))],
        out_specs=pl.BlockSpec((tm, d), lambda i: (i, 0)),
        out_shape=jax.ShapeDtypeStruct((t, d), F32),
        compiler_params=_cparams("parallel"),
        name="residual_ln",
    )(x2d, f2d, g.reshape(1, d), b.reshape(1, d))


def _sc_batches(b):
    return b // 2


def kernel(x, w_in, conv_a_w, conv_a_b, norm_a_g, norm_a_b, pool_w, pool_scale, conv_c_w, w_out,
           ln1_g, ln1_b, peer_wq, peer_keys, peer_u, peer_v, ln2_g, ln2_b):
    depth = w_in.shape[0]
    b, s, d = x.shape
    alpha = float((2 * depth) ** 0.25)
    w = conv_a_w.shape[2]
    tq = min(256, s)
    ts = min(512, s)
    tm = min(512, s)
    row = lax.broadcasted_iota(I32, (tq, tq), 0)
    col = lax.broadcasted_iota(I32, (tq, tq), 1)
    later = (row > col).astype(BF16)
    b_tc = b - _sc_batches(b)
    for l in range(depth):
        w_in_l, w_out_l = w_in[l].astype(BF16), w_out[l].astype(BF16)
        pool_bd = _block_diag(pool_w[l]).astype(BF16)
        keys = peer_keys[l].reshape(2 * PEER_HEADS, PEER_NKEYS, -1).astype(BF16)
        wq_t = peer_wq[l].T.astype(BF16)
        words = _pack_rows(peer_u[l], peer_v[l])

        def front(xp):
            tp = xp.shape[0] * s
            xpf = xp.reshape(tp, d)
            proj3 = in_proj(xpf, w_in_l, tm).reshape(xp.shape[0], s, -1)
            y_abc = local_mix(proj3, conv_a_w[l], conv_a_b[l], norm_a_g[l], norm_a_b[l],
                              pool_bd, pool_scale[l], conv_c_w[l], ts)
            y_d = attention(proj3, later, tq, (6 * w) // LANES)
            x1 = out_ln(y_abc.reshape(tp, 3 * w), y_d.reshape(tp, -1), xpf, w_out_l,
                        ln1_g[l], ln1_b[l], alpha, tm)
            idx, gates = route(x1, wq_t, keys, min(256, s))
            return x1, idx, gates

        outs = []
        if b_tc < b:
            x1_sc, idx_sc, gates_sc = front(x[b_tc:])
            f_sc = sc_peer(x1_sc.reshape(-1), idx_sc.reshape(-1), gates_sc.reshape(-1), words)
        x1_tc, idx_tc, gates_tc = front(x[:b_tc])
        outs.append(peer(x1_tc, idx_tc, gates_tc, words[:, None, :], ln2_g[l], ln2_b[l], alpha, 16))
        if b_tc < b:
            outs.append(residual_ln(x1_sc, f_sc.reshape(-1, d), ln2_g[l], ln2_b[l], alpha, tm))
        x = jnp.concatenate(outs, axis=0).reshape(b, s, d)
    return x
```

```python
import functools

import jax
import jax.numpy as jnp
from jax import lax
from jax.experimental import pallas as pl
from jax.experimental.pallas import tpu as pltpu

F32 = jnp.float32
BF16 = jnp.bfloat16
I32 = jnp.int32

LN_EPS = 1e-5
LANES = 128
CONF_WIDTH = 31
POOL_WINDOWS = (2, 4, 8, 16)
SCONV_WIDTH = 3
HEAD_DIM = 64
PEER_HEADS = 8
PEER_NKEYS = 128
PEER_TOPK = 16
HALO = 32
ATTN_UNDERFLOW = -100.0
VMEM_LIMIT = 56 * 1024 * 1024


def _cparams(*sem):
    return pltpu.CompilerParams(dimension_semantics=sem, vmem_limit_bytes=VMEM_LIMIT)


def _layer_norm(h, g, b):
    mu = jnp.mean(h, axis=-1, keepdims=True)
    hc = h - mu
    var = jnp.mean(hc * hc, axis=-1, keepdims=True)
    return hc * lax.rsqrt(var + LN_EPS) * g + b


def _dot_nt(a, b):
    return lax.dot_general(a, b, (((1,), (1,)), ((), ())), preferred_element_type=F32)


def _in_proj_kernel(x_ref, w_ref, o_ref):
    o_ref[...] = jnp.dot(x_ref[...].astype(BF16), w_ref[...], preferred_element_type=F32)


def in_proj(x2d, w_bf16, tm):
    t, d = x2d.shape
    n = w_bf16.shape[1]
    return pl.pallas_call(
        _in_proj_kernel,
        grid=(t // tm,),
        in_specs=[pl.BlockSpec((tm, d), lambda i: (i, 0)),
                  pl.BlockSpec((d, n), lambda i: (0, 0))],
        out_specs=pl.BlockSpec((tm, n), lambda i: (i, 0)),
        out_shape=jax.ShapeDtypeStruct((t, n), F32),
        compiler_params=_cparams("parallel"),
        name="in_proj",
    )(x2d, w_bf16)


def _local_mix_kernel(p_ref, caw_ref, cab_ref, nag_ref, nab_ref, pw_ref, ps_ref, ccw_ref,
                      o_ref, hist, *, ts, w):
    s = pl.program_id(1)

    @pl.when(s == 0)
    def _():
        hist[0:HALO, :] = jnp.zeros((HALO, 3 * w), F32)

    a_val = p_ref[:, 0:w]
    a_gate = p_ref[:, w:2 * w]
    c_h = p_ref[:, 3 * w:4 * w]
    c_gate_b = p_ref[:, 4 * w:5 * w]
    c_gate_c = p_ref[:, 5 * w:6 * w]
    hist[HALO:HALO + ts, 0:w] = a_val * jax.nn.sigmoid(a_gate)
    hist[HALO:HALO + ts, w:2 * w] = p_ref[:, 2 * w:3 * w]
    hist[HALO:HALO + ts, 2 * w:3 * w] = c_gate_c * c_h

    def past(col, back):
        return hist[HALO - back:HALO - back + ts, col * w:(col + 1) * w]

    acc = jnp.zeros((ts, w), F32)
    for k in range(CONF_WIDTH):
        acc = acc + caw_ref[k:k + 1, :] * past(0, CONF_WIDTH - 1 - k)
    h = _layer_norm(acc + cab_ref[...], nag_ref[...], nab_ref[...])
    o_ref[:, 0:w] = h * jax.nn.sigmoid(h)

    t_pos = s * ts + lax.broadcasted_iota(I32, (ts, 1), 0)
    chan = lax.broadcasted_iota(I32, (ts, w), 1)
    group = w // len(POOL_WINDOWS)
    x0 = past(1, 0)
    run = x0
    pooled = jnp.zeros((ts, w), F32)
    done = 1
    for g, win in enumerate(POOL_WINDOWS):
        for back in range(done, win):
            run = run + past(1, back)
        done = win
        inv = 1.0 / jnp.minimum(t_pos + 1, win).astype(F32)
        in_group = (chan >= g * group) & (chan < (g + 1) * group)
        pooled = jnp.where(in_group, run * inv, pooled)
    pooled = pooled - x0
    mixed = jnp.dot(pooled.astype(BF16), pw_ref[...], preferred_element_type=F32)
    o_ref[:, w:2 * w] = mixed * ps_ref[...]

    conv = jnp.zeros((ts, w), F32)
    for k in range(SCONV_WIDTH):
        conv = conv + ccw_ref[k:k + 1, :] * past(2, SCONV_WIDTH - 1 - k)
    o_ref[:, 2 * w:3 * w] = c_gate_b * conv

    hist[0:HALO, :] = hist[ts:ts + HALO, :]


def local_mix(proj3, conv_a_w, conv_a_b, norm_a_g, norm_a_b, pool_bd, pool_scale, conv_c_w, ts):
    b, s, _ = proj3.shape
    w = conv_a_w.shape[1]
    row = lambda a: a.reshape(1, w)
    full = lambda shape: pl.BlockSpec(shape, lambda bi, si: (0,) * len(shape))
    return pl.pallas_call(
        functools.partial(_local_mix_kernel, ts=ts, w=w),
        grid=(b, s // ts),
        in_specs=[pl.BlockSpec((None, ts, 6 * w), lambda bi, si: (bi, si, 0)),
                  full((CONF_WIDTH, w)), full((1, w)), full((1, w)), full((1, w)),
                  full((w, w)), full((1, w)), full((SCONV_WIDTH, w))],
        out_specs=pl.BlockSpec((None, ts, 3 * w), lambda bi, si: (bi, si, 0)),
        out_shape=jax.ShapeDtypeStruct((b, s, 3 * w), F32),
        scratch_shapes=[pltpu.VMEM((HALO + ts, 3 * w), F32)],
        compiler_params=_cparams("arbitrary", "arbitrary"),
        name="local_mix",
    )(proj3, conv_a_w, row(conv_a_b), row(norm_a_g), row(norm_a_b), pool_bd, row(pool_scale), conv_c_w)


def _attention_kernel(q_ref, k_ref, v_ref, m_ref, o_ref, acc_ref, car_ref, *, tq, scale):
    i = pl.program_id(2)
    q = q_ref[...]
    lane = lax.broadcasted_iota(I32, (tq, LANES), 1)
    q_heads = (jnp.where(lane < HEAD_DIM, q, 0.0).astype(BF16),
               jnp.where(lane >= HEAD_DIM, q, 0.0).astype(BF16))
    acc_ref[...] = jnp.zeros(acc_ref.shape, F32)
    car_ref[...] = jnp.zeros(car_ref.shape, F32)
    row = lax.broadcasted_iota(I32, (tq, tq), 0)
    col = lax.broadcasted_iota(I32, (tq, tq), 1)
    causal = col < row
    later = m_ref[...]

    def prepare(j, diagonal):
        start = pl.multiple_of(j * tq, tq)
        kb = k_ref[pl.ds(start, tq), :].astype(BF16)
        vb = v_ref[pl.ds(start, tq), :].astype(BF16)
        heads = []
        for h in range(2):
            z = _dot_nt(q_heads[h], kb) * scale
            softplus = jnp.maximum(z, 0.0) + jnp.log1p(jnp.exp(-jnp.abs(z)))
            log_stay = -softplus
            log_att = z - softplus
            if diagonal:
                log_stay = jnp.where(causal, log_stay, 0.0)
            hi = log_stay.astype(BF16)
            lo = (log_stay - hi.astype(F32)).astype(BF16)
            inside = (jnp.dot(hi, later, preferred_element_type=F32)
                      + jnp.dot(lo, later, preferred_element_type=F32))
            heads.append((log_att + inside, jnp.sum(log_stay, axis=1, keepdims=True)))
        return heads, vb

    def apply(prepared, diagonal):
        heads, vb = prepared
        for h, (log_w, stay_sum) in enumerate(heads):
            carried = car_ref[h]
            wgt = jnp.exp(log_w + carried)
            if diagonal:
                wgt = jnp.where(causal, wgt, 0.0)
            acc_ref[h] += jnp.dot(wgt.astype(BF16), vb, preferred_element_type=F32)
            car_ref[h] = carried + stay_sum

    @pl.when(i == 0)
    def _():
        apply(prepare(i, True), True)

    @pl.when(i > 0)
    def _():
        diagonal, neighbour = prepare(i, True), prepare(i - 1, False)
        apply(diagonal, True)
        apply(neighbour, False)

    def cond(c):
        j, worst = c
        return (j >= 0) & (worst > ATTN_UNDERFLOW)

    def body(c):
        j, _ = c
        apply(prepare(j, False), False)
        return j - 1, jnp.max(car_ref[...])

    lax.while_loop(cond, body, (i - 2, jnp.max(car_ref[...])))
    o_ref[...] = jnp.where(lane < HEAD_DIM, acc_ref[0], acc_ref[1])


def attention(proj3, later, tq, col0):
    b, s, _ = proj3.shape
    pairs = 2
    scale = HEAD_DIM ** -0.5
    return pl.pallas_call(
        functools.partial(_attention_kernel, tq=tq, scale=scale),
        grid=(b, pairs, s // tq),
        in_specs=[pl.BlockSpec((None, tq, LANES), lambda bi, p, i: (bi, i, col0 + p)),
                  pl.BlockSpec((None, s, LANES), lambda bi, p, i: (bi, 0, col0 + pairs + p)),
                  pl.BlockSpec((None, s, LANES), lambda bi, p, i: (bi, 0, col0 + 2 * pairs + p)),
                  pl.BlockSpec((tq, tq), lambda bi, p, i: (0, 0))],
        out_specs=pl.BlockSpec((None, tq, LANES), lambda bi, p, i: (bi, i, p)),
        out_shape=jax.ShapeDtypeStruct((b, s, pairs * LANES), F32),
        scratch_shapes=[pltpu.VMEM((2, tq, LANES), F32), pltpu.VMEM((2, tq, 1), F32)],
        compiler_params=_cparams("parallel", "parallel", "arbitrary"),
        name="attention",
    )(proj3, proj3, proj3, later)


def _out_ln_kernel(ya_ref, yd_ref, x_ref, w_ref, g_ref, b_ref, o_ref, *, alpha, wa):
    m = jnp.dot(ya_ref[...].astype(BF16), w_ref[0:wa, :], preferred_element_type=F32)
    m = m + jnp.dot(yd_ref[...].astype(BF16), w_ref[wa:, :], preferred_element_type=F32)
    o_ref[...] = _layer_norm(alpha * x_ref[...] + m, g_ref[...], b_ref[...])


def out_ln(y_abc, y_d, x2d, w_out_bf16, g, b, alpha, tm):
    t, d = x2d.shape
    wa, wd = y_abc.shape[1], y_d.shape[1]
    return pl.pallas_call(
        functools.partial(_out_ln_kernel, alpha=alpha, wa=wa),
        grid=(t // tm,),
        in_specs=[pl.BlockSpec((tm, wa), lambda i: (i, 0)),
                  pl.BlockSpec((tm, wd), lambda i: (i, 0)),
                  pl.BlockSpec((tm, d), lambda i: (i, 0)),
                  pl.BlockSpec((wa + wd, d), lambda i: (0, 0)),
                  pl.BlockSpec((1, d), lambda i: (0, 0)),
                  pl.BlockSpec((1, d), lambda i: (0, 0))],
        out_specs=pl.BlockSpec((tm, d), lambda i: (i, 0)),
        out_shape=jax.ShapeDtypeStruct((t, d), F32),
        compiler_params=_cparams("parallel"),
        name="out_ln",
    )(y_abc, y_d, x2d, w_out_bf16, g.reshape(1, d), b.reshape(1, d))


NEVER = 1e9


def _stack(rows):
    return jnp.concatenate(rows, axis=0)


def _merge_sort_network(n):
    pairs = []

    def merge(lo, size, r):
        step = 2 * r
        if step < size:
            merge(lo, size, step)
            merge(lo + r, size, step)
            pairs.extend((i, i + r) for i in range(lo + r, lo + size - r, step))
        else:
            pairs.append((lo, lo + r))

    def sort(lo, size):
        if size > 1:
            sort(lo, size // 2)
            sort(lo + size // 2, size // 2)
            merge(lo, size, 1)

    sort(0, n)
    return pairs


def _top_k_keys(scores, k):
    n, m = scores.shape
    sub = 8
    slabs = n // sub
    base = lax.broadcasted_iota(I32, (sub, m), 0).astype(F32)
    vals = [scores[v * sub:(v + 1) * sub, :] for v in range(slabs)]
    ids = [base + float(v * sub) for v in range(slabs)]
    for i, j in _merge_sort_network(slabs):
        better = (vals[j] > vals[i]) | ((vals[j] == vals[i]) & (ids[j] < ids[i]))
        vals[i], vals[j] = jnp.where(better, vals[j], vals[i]), jnp.where(better, vals[i], vals[j])
        ids[i], ids[j] = jnp.where(better, ids[j], ids[i]), jnp.where(better, ids[i], ids[j])
    out_v, out_i = [], []
    for step in range(k):
        best = jnp.max(vals[0], axis=0, keepdims=True)
        first = jnp.min(jnp.where(vals[0] == best, ids[0], NEVER), axis=0, keepdims=True)
        out_v.append(best)
        out_i.append(first)
        hit = ids[0] == first
        for depth in range(k - 1 - step):
            vals[depth] = jnp.where(hit, vals[depth + 1], vals[depth])
            ids[depth] = jnp.where(hit, ids[depth + 1], ids[depth])
    return out_v, out_i


def _top_k_pairs(s0, i0, s1, i1, tm):
    k = PEER_TOPK
    sub = 8
    a_low = lax.broadcasted_iota(I32, (sub, tm), 0).astype(F32)
    s0_low, i0_low = _stack(s0[:sub]), _stack(i0[:sub])
    vals, order, ident = [], [], []
    for b in range(k):
        reach = a_low <= float(k // (b + 1) - 1)
        vals.append(jnp.where(reach, s0_low + s1[b], -jnp.inf))
        order.append(jnp.where(reach, a_low * k + b, NEVER))
        ident.append(i0_low * PEER_NKEYS + i1[b])
    top_v = _stack(s0[sub:]) + s1[0]
    top_o = (a_low + sub) * k
    top_e = _stack(i0[sub:]) * PEER_NKEYS + i1[0]
    out_s, out_e = [], []
    for step in range(k):
        best = jnp.max(jnp.maximum(vals[0], top_v), axis=0, keepdims=True)
        first = jnp.min(jnp.minimum(jnp.where(vals[0] == best, order[0], NEVER),
                                    jnp.where(top_v == best, top_o, NEVER)), axis=0, keepdims=True)
        hit, top_hit = order[0] == first, top_o == first
        out_s.append(best)
        out_e.append(jnp.max(jnp.maximum(jnp.where(hit, ident[0], -1.0), jnp.where(top_hit, top_e, -1.0)),
                             axis=0, keepdims=True))
        top_v = jnp.where(top_hit, -jnp.inf, top_v)
        top_o = jnp.where(top_hit, NEVER, top_o)
        for depth in range(k - 1 - step):
            vals[depth] = jnp.where(hit, vals[depth + 1], vals[depth])
            order[depth] = jnp.where(hit, order[depth + 1], order[depth])
            ident[depth] = jnp.where(hit, ident[depth + 1], ident[depth])
    return out_s, out_e


def _route_kernel(x_ref, wq_ref, keys_ref, idx_ref, gate_ref, *, tm):
    half = PEER_NKEYS
    qt = _dot_nt(wq_ref[...], x_ref[...].astype(BF16)).astype(BF16)
    idx_rows, gate_rows = [], []
    for h in range(PEER_HEADS):
        tops = []
        for p in range(2):
            hp = 2 * h + p
            scores = jnp.dot(keys_ref[hp], qt[hp * half:(hp + 1) * half, :], preferred_element_type=F32)
            tops.append(_top_k_keys(scores, PEER_TOPK))
        (s0, i0), (s1, i1) = tops
        best_s, best_e = _top_k_pairs(s0, i0, s1, i1, tm)
        e = jnp.exp(_stack(best_s) - best_s[0])
        gate_rows.append(e / jnp.sum(e, axis=0, keepdims=True))
        idx_rows.extend(best_e)
    idx_ref[...] = _stack(idx_rows).astype(I32).T
    gate_ref[...] = _stack(gate_rows).T


def route(x2d, wq_t_bf16, keys_bf16, tm):
    t, d = x2d.shape
    nq = wq_t_bf16.shape[0]
    sel = PEER_HEADS * PEER_TOPK
    return pl.pallas_call(
        functools.partial(_route_kernel, tm=tm),
        grid=(t // tm,),
        in_specs=[pl.BlockSpec((tm, d), lambda i: (i, 0)),
                  pl.BlockSpec((nq, d), lambda i: (0, 0)),
                  pl.BlockSpec(keys_bf16.shape, lambda i: (0, 0, 0))],
        out_specs=[pl.BlockSpec((tm, sel), lambda i: (i, 0)),
                   pl.BlockSpec((tm, sel), lambda i: (i, 0))],
        out_shape=[jax.ShapeDtypeStruct((t, sel), I32), jax.ShapeDtypeStruct((t, sel), F32)],
        compiler_params=_cparams("parallel"),
        name="route",
    )(x2d, wq_t_bf16, keys_bf16)


PEER_PIECES = 4
MXU_ROWS = 8


def _peer_kernel(idx_now, idx_next, x_ref, gate_ref, g_ref, b_ref, tab_ref, o_ref, rows, sems,
                 *, c, sel, d, alpha):
    i = pl.program_id(0)
    n = pl.num_programs(0)
    piece = d // PEER_PIECES
    per = sel // (2 * PEER_PIECES)

    def start_rows(idx_ref, src_row, half, tk, first):
        for j in range(first, first + per):
            pltpu.make_async_copy(tab_ref.at[idx_ref[src_row, j]], rows.at[half, tk, pl.ds(j, 1), :],
                                  sems.at[half, tk]).start(priority=j % 2)

    def wait_token(half, tk):
        pltpu.make_async_copy(rows.at[1 - half, tk], rows.at[half, tk], sems.at[half, tk]).wait()

    @pl.when(i == 0)
    def _():
        for tk in range(c):
            for first in range(0, sel, per):
                start_rows(idx_now, tk, 0, tk, first)

    def score_piece(tok, xb, q):
        half, tk = divmod(tok, c)
        cols = slice(q * piece, (q + 1) * piece)
        u = lax.bitcast_convert_type(rows[half, tk, :, cols] << 16, F32).astype(BF16)
        return _dot_nt(xb[:, cols], u)

    def mix_piece(tok, act, q):
        half, tk = divmod(tok, c)
        cols = slice(q * piece, (q + 1) * piece)
        v = lax.bitcast_convert_type(rows[half, tk, :, cols] & jnp.uint32(0xFFFF0000), F32).astype(BF16)
        return jnp.dot(act, v, preferred_element_type=F32)

    groups = 2 * PEER_PIECES
    scored = None
    mixed = None
    for k in range(2 * c + 2):
        tok = k if k < 2 * c else None
        if tok is not None:
            half, tk = divmod(tok, c)
            ahead_ref, ahead_row = (idx_now, tok + c) if half == 0 else (idx_next, tk)
            wait_token(half, tk)
            x = jnp.broadcast_to(x_ref[tok:tok + 1, :], (MXU_ROWS, d))
            xb = x.astype(BF16)
        y_done = None
        if mixed is not None:
            m_tok, m_x, m_f = mixed
            y_done = (m_tok, _layer_norm(alpha * m_x + m_f, g_ref[...], b_ref[...]))
        act = None
        if scored is not None:
            s_tok, s_x, s_score = scored
            act = (jax.nn.gelu(s_score) * gate_ref[s_tok:s_tok + 1, :]).astype(BF16)
        score = jnp.zeros((MXU_ROWS, sel), F32)
        parts = []
        issued = 0
        for batch in range(2):
            for q in range(batch * PEER_PIECES // 2, (batch + 1) * PEER_PIECES // 2):
                if act is not None:
                    parts.append(mix_piece(s_tok, act, q))
                if tok is not None:
                    score = score + score_piece(tok, xb, q)
            if tok is not None:
                for _ in range(groups // 4 if batch == 0 else groups - groups // 4):
                    start_rows(ahead_ref, ahead_row, 1 - half, tk, issued * per)
                    issued += 1
        if y_done is not None:
            o_ref[y_done[0]:y_done[0] + 1, :] = y_done[1][0:1, :]
        mixed = (s_tok, s_x, jnp.concatenate(parts, axis=1)) if act is not None else None
        scored = (tok, x, score) if tok is not None else None

    @pl.when(i == n - 1)
    def _():
        for tk in range(c):
            wait_token(0, tk)


def peer(x2d, idx, gates, table, g, b, alpha, c):
    t, d = x2d.shape
    sel = idx.shape[1]
    n = t // (2 * c)
    return pl.pallas_call(
        functools.partial(_peer_kernel, c=c, sel=sel, d=d, alpha=alpha),
        grid=(n,),
        in_specs=[pl.BlockSpec((2 * c, sel), lambda i: (i, 0), memory_space=pltpu.SMEM),
                  pl.BlockSpec((2 * c, sel), lambda i: (jnp.minimum(i + 1, n - 1), 0), memory_space=pltpu.SMEM),
                  pl.BlockSpec((2 * c, d), lambda i: (i, 0)),
                  pl.BlockSpec((2 * c, sel), lambda i: (i, 0)),
                  pl.BlockSpec((1, d), lambda i: (0, 0)),
                  pl.BlockSpec((1, d), lambda i: (0, 0)),
                  pl.BlockSpec(memory_space=pl.ANY)],
        out_specs=pl.BlockSpec((2 * c, d), lambda i: (i, 0)),
        out_shape=jax.ShapeDtypeStruct((t, d), F32),
        scratch_shapes=[pltpu.VMEM((2, c, sel, d), jnp.uint32), pltpu.SemaphoreType.DMA((2, c))],
        compiler_params=_cparams("arbitrary"),
        name="peer",
    )(idx, idx, x2d, gates, g.reshape(1, d), b.reshape(1, d), table)


SC_LANES = 16
SC_WORKERS = 32


def _sc_peer_body(x_hbm, idx_hbm, gate_hbm, tab_hbm, f_hbm, x_v, idx_v, gate_v, out_v, rows_v,
                  row_sems, in_sems, out_sems, *, per_worker, d, sel):
    heads = sel // PEER_TOPK
    chunks = d // SC_LANES
    wid = lax.axis_index("s") * 2 + lax.axis_index("c")
    first = wid * per_worker
    last = first + per_worker - 1

    def gather(p, h):
        return pltpu.make_async_copy(tab_hbm.at[idx_v.at[p, pl.ds(h * PEER_TOPK, PEER_TOPK)]],
                                     rows_v.at[h % 2], row_sems.at[h % 2])

    def inputs(g, p):
        return (pltpu.make_async_copy(idx_hbm.at[pl.ds(g * sel, sel)], idx_v.at[p], in_sems.at[p, 0]),
                pltpu.make_async_copy(gate_hbm.at[pl.ds(g * sel, sel)], gate_v.at[p], in_sems.at[p, 1]),
                pltpu.make_async_copy(x_hbm.at[pl.ds(g * d, d)], x_v.at[p], in_sems.at[p, 2]))

    def result(g, p):
        return pltpu.make_async_copy(out_v.at[p], f_hbm.at[pl.ds(g * d, d)], out_sems.at[p])

    def token(g, p, have_older):
        following = jnp.minimum(g + 1, last)
        for copy in inputs(following, 1 - p):
            copy.start()

        @pl.when(have_older)
        def _():
            result(g, p).wait()

        def clear(ci, c2):
            out_v[p, pl.ds(ci * SC_LANES, SC_LANES)] = jnp.zeros((SC_LANES,), F32)
            return c2
        lax.fori_loop(0, chunks, clear, 0)

        for h in range(heads):
            slot = h % 2
            if h + 1 < heads:
                gather(p, h + 1).start()
            else:
                for copy in inputs(following, 1 - p):
                    copy.wait()
                gather(1 - p, 0).start()
            gather(p, h).wait()

            def dots(ci, accs):
                xc = x_v[p, pl.ds(ci * SC_LANES, SC_LANES)]
                out = []
                for r in range(PEER_TOPK):
                    w = rows_v[slot, r, pl.ds(ci * SC_LANES, SC_LANES)]
                    out.append(accs[r] + lax.bitcast_convert_type(w << 16, F32) * xc)
                return tuple(out)
            accs = lax.fori_loop(0, chunks, dots, tuple(jnp.zeros((SC_LANES,), F32) for _ in range(PEER_TOPK)))
            lane = lax.iota(I32, SC_LANES)
            score = jnp.zeros((SC_LANES,), F32)
            for r in range(PEER_TOPK):
                score = jnp.where(lane == r, jnp.sum(accs[r]), score)
            inner = 0.7978845608028654 * (score + 0.044715 * score * score * score)
            tanh = 1.0 - 2.0 / (jnp.exp(2.0 * inner) + 1.0)
            act = 0.5 * score * (1.0 + tanh) * gate_v[p, pl.ds(h * PEER_TOPK, PEER_TOPK)]
            weights = [jnp.sum(jnp.where(lane == r, act, 0.0)) for r in range(PEER_TOPK)]

            def mix(ci, c2):
                part = [out_v[p, pl.ds(ci * SC_LANES, SC_LANES)]] + [jnp.zeros((SC_LANES,), F32)] * 3
                for r in range(PEER_TOPK):
                    w = rows_v[slot, r, pl.ds(ci * SC_LANES, SC_LANES)]
                    part[r % 4] = part[r % 4] + weights[r] * lax.bitcast_convert_type(
                        w & jnp.uint32(0xFFFF0000), F32)
                out_v[p, pl.ds(ci * SC_LANES, SC_LANES)] = (part[0] + part[1]) + (part[2] + part[3])
                return c2
            lax.fori_loop(0, chunks, mix, 0)
        result(g, p).start()

    for copy in inputs(first, 0):
        copy.start()
    for copy in inputs(first, 0):
        copy.wait()
    gather(0, 0).start()

    def pair(k, carry):
        token(first + 2 * k, 0, k > 0)
        token(first + 2 * k + 1, 1, k > 0)
        return carry
    lax.fori_loop(0, per_worker // 2, pair, 0)

    gather(0, 0).wait()
    result(last, 0).wait()
    result(last, 1).wait()


def sc_peer(x_flat, idx_flat, gate_flat, table2d):
    from jax.experimental.pallas import tpu_sc as plsc
    d = table2d.shape[1]
    tokens = x_flat.shape[0] // d
    sel = idx_flat.shape[0] // tokens
    per_worker = tokens // SC_WORKERS
    mesh = plsc.VectorSubcoreMesh(core_axis_name="c", subcore_axis_name="s")
    return pl.kernel(
        functools.partial(_sc_peer_body, per_worker=per_worker, d=d, sel=sel),
        out_type=jax.ShapeDtypeStruct((tokens * d,), F32),
        mesh=mesh,
        scratch_types=[pltpu.VMEM((2, d), F32), pltpu.VMEM((2, sel), I32), pltpu.VMEM((2, sel), F32),
                       pltpu.VMEM((2, d), F32), pltpu.VMEM((2, PEER_TOPK, d), jnp.uint32),
                       pltpu.SemaphoreType.DMA((2,)), pltpu.SemaphoreType.DMA((2, 3)),
                       pltpu.SemaphoreType.DMA((2,))],
        compiler_params=pltpu.CompilerParams(needs_layout_passes=False),
        name="sc_peer",
    )(x_flat, idx_flat, gate_flat, table2d)


def _block_diag(pool_w):
    g, c, _ = pool_w.shape
    out = jnp.zeros((g * c, g * c), pool_w.dtype)
    for k in range(g):
        out = out.at[k * c:(k + 1) * c, k * c:(k + 1) * c].set(pool_w[k])
    return out


def _pack_rows(u, v):
    bits = lambda a: lax.bitcast_convert_type(a.astype(BF16), jnp.uint16).astype(jnp.uint32)
    return bits(u) | (bits(v) << 16)


def _residual_ln_kernel(x_ref, f_ref, g_ref, b_ref, o_ref, *, alpha):
    o_ref[...] = _layer_norm(alpha * x_ref[...] + f_ref[...], g_ref[...], b_ref[...])


def residual_ln(x2d, f2d, g, b, alpha, tm):
    t, d = x2d.shape
    return pl.pallas_call(
        functools.partial(_residual_ln_kernel, alpha=alpha),
        grid=(t // tm,),
        in_specs=[pl.BlockSpec((tm, d), lambda i: (i, 0)),
                  pl.BlockSpec((tm, d), lambda i: (i, 0)),
                  pl.BlockSpec((1, d), lambda i: (0, 0)),
                  pl.BlockSpec((1, d), lambda i: (0, 0))],
        out_specs=pl.BlockSpec((tm, d), lambda i: (i, 0)),
        out_shape=jax.ShapeDtypeStruct((t, d), F32),
        compiler_params=_cparams("parallel"),
        name="residual_ln",
    )(x2d, f2d, g.reshape(1, d), b.reshape(1, d))


def _sc_batches(b):
    return b // 2


def kernel(x, w_in, conv_a_w, conv_a_b, norm_a_g, norm_a_b, pool_w, pool_scale, conv_c_w, w_out,
           ln1_g, ln1_b, peer_wq, peer_keys, peer_u, peer_v, ln2_g, ln2_b):
    depth = w_in.shape[0]
    b, s, d = x.shape
    alpha = float((2 * depth) ** 0.25)
    w = conv_a_w.shape[2]
    tq = min(256, s)
    ts = min(512, s)
    tm = min(512, s)
    row = lax.broadcasted_iota(I32, (tq, tq), 0)
    col = lax.broadcasted_iota(I32, (tq, tq), 1)
    later = (row > col).astype(BF16)
    b_tc = b - _sc_batches(b)
    for l in range(depth):
        w_in_l, w_out_l = w_in[l].astype(BF16), w_out[l].astype(BF16)
        pool_bd = _block_diag(pool_w[l]).astype(BF16)
        keys = peer_keys[l].reshape(2 * PEER_HEADS, PEER_NKEYS, -1).astype(BF16)
        wq_t = peer_wq[l].T.astype(BF16)
        words = _pack_rows(peer_u[l], peer_v[l])

        def front(xp):
            tp = xp.shape[0] * s
            xpf = xp.reshape(tp, d)
            proj3 = in_proj(xpf, w_in_l, tm).reshape(xp.shape[0], s, -1)
            y_abc = local_mix(proj3, conv_a_w[l], conv_a_b[l], norm_a_g[l], norm_a_b[l],
                              pool_bd, pool_scale[l], conv_c_w[l], ts)
            y_d = attention(proj3, later, tq, (6 * w) // LANES)
            x1 = out_ln(y_abc.reshape(tp, 3 * w), y_d.reshape(tp, -1), xpf, w_out_l,
                        ln1_g[l], ln1_b[l], alpha, tm)
            idx, gates = route(x1, wq_t, keys, min(256, s))
            return x1, idx, gates

        outs = []
        if b_tc < b:
            x1_sc, idx_sc, gates_sc = front(x[b_tc:])
            f_sc = sc_peer(x1_sc.reshape(-1), idx_sc.reshape(-1), gates_sc.reshape(-1), words)
        x1_tc, idx_tc, gates_tc = front(x[:b_tc])
        outs.append(peer(x1_tc, idx_tc, gates_tc, words[:, None, :], ln2_g[l], ln2_b[l], alpha, 16))
        if b_tc < b:
            outs.append(residual_ln(x1_sc, f_sc.reshape(-1, d), ln2_g[l], ln2_b[l], alpha, tm))
        x = jnp.concatenate(outs, axis=0).reshape(b, s, d)
    return x
```

```python
import functools

import jax
import jax.numpy as jnp
from jax import lax
from jax.experimental import pallas as pl
from jax.experimental.pallas import tpu as pltpu

F32 = jnp.float32
BF16 = jnp.bfloat16
I32 = jnp.int32

LN_EPS = 1e-5
LANES = 128
CONF_WIDTH = 31
POOL_WINDOWS = (2, 4, 8, 16)
SCONV_WIDTH = 3
HEAD_DIM = 64
PEER_HEADS = 8
PEER_NKEYS = 128
PEER_TOPK = 16
HALO = 32
ATTN_UNDERFLOW = -100.0
VMEM_LIMIT = 56 * 1024 * 1024


def _cparams(*sem):
    return pltpu.CompilerParams(dimension_semantics=sem, vmem_limit_bytes=VMEM_LIMIT)


def _layer_norm(h, g, b):
    mu = jnp.mean(h, axis=-1, keepdims=True)
    hc = h - mu
    var = jnp.mean(hc * hc, axis=-1, keepdims=True)
    return hc * lax.rsqrt(var + LN_EPS) * g + b


def _dot_nt(a, b):
    return lax.dot_general(a, b, (((1,), (1,)), ((), ())), preferred_element_type=F32)


def _in_proj_kernel(x_ref, w_ref, o_ref):
    o_ref[...] = jnp.dot(x_ref[...].astype(BF16), w_ref[...], preferred_element_type=F32)


def in_proj(x2d, w_bf16, tm):
    t, d = x2d.shape
    n = w_bf16.shape[1]
    return pl.pallas_call(
        _in_proj_kernel,
        grid=(t // tm,),
        in_specs=[pl.BlockSpec((tm, d), lambda i: (i, 0)),
                  pl.BlockSpec((d, n), lambda i: (0, 0))],
        out_specs=pl.BlockSpec((tm, n), lambda i: (i, 0)),
        out_shape=jax.ShapeDtypeStruct((t, n), F32),
        compiler_params=_cparams("parallel"),
        name="in_proj",
    )(x2d, w_bf16)


def _local_mix_kernel(p_ref, caw_ref, cab_ref, nag_ref, nab_ref, pw_ref, ps_ref, ccw_ref,
                      o_ref, hist, *, ts, w):
    s = pl.program_id(1)

    @pl.when(s == 0)
    def _():
        hist[0:HALO, :] = jnp.zeros((HALO, 3 * w), F32)

    a_val = p_ref[:, 0:w]
    a_gate = p_ref[:, w:2 * w]
    c_h = p_ref[:, 3 * w:4 * w]
    c_gate_b = p_ref[:, 4 * w:5 * w]
    c_gate_c = p_ref[:, 5 * w:6 * w]
    hist[HALO:HALO + ts, 0:w] = a_val * jax.nn.sigmoid(a_gate)
    hist[HALO:HALO + ts, w:2 * w] = p_ref[:, 2 * w:3 * w]
    hist[HALO:HALO + ts, 2 * w:3 * w] = c_gate_c * c_h

    def past(col, back):
        return hist[HALO - back:HALO - back + ts, col * w:(col + 1) * w]

    acc = jnp.zeros((ts, w), F32)
    for k in range(CONF_WIDTH):
        acc = acc + caw_ref[k:k + 1, :] * past(0, CONF_WIDTH - 1 - k)
    h = _layer_norm(acc + cab_ref[...], nag_ref[...], nab_ref[...])
    o_ref[:, 0:w] = h * jax.nn.sigmoid(h)

    t_pos = s * ts + lax.broadcasted_iota(I32, (ts, 1), 0)
    chan = lax.broadcasted_iota(I32, (ts, w), 1)
    group = w // len(POOL_WINDOWS)
    x0 = past(1, 0)
    run = x0
    pooled = jnp.zeros((ts, w), F32)
    done = 1
    for g, win in enumerate(POOL_WINDOWS):
        for back in range(done, win):
            run = run + past(1, back)
        done = win
        inv = 1.0 / jnp.minimum(t_pos + 1, win).astype(F32)
        in_group = (chan >= g * group) & (chan < (g + 1) * group)
        pooled = jnp.where(in_group, run * inv, pooled)
    pooled = pooled - x0
    mixed = jnp.dot(pooled.astype(BF16), pw_ref[...], preferred_element_type=F32)
    o_ref[:, w:2 * w] = mixed * ps_ref[...]

    conv = jnp.zeros((ts, w), F32)
    for k in range(SCONV_WIDTH):
        conv = conv + ccw_ref[k:k + 1, :] * past(2, SCONV_WIDTH - 1 - k)
    o_ref[:, 2 * w:3 * w] = c_gate_b * conv

    hist[0:HALO, :] = hist[ts:ts + HALO, :]


def local_mix(proj3, conv_a_w, conv_a_b, norm_a_g, norm_a_b, pool_bd, pool_scale, conv_c_w, ts):
    b, s, _ = proj3.shape
    w = conv_a_w.shape[1]
    row = lambda a: a.reshape(1, w)
    full = lambda shape: pl.BlockSpec(shape, lambda bi, si: (0,) * len(shape))
    return pl.pallas_call(
        functools.partial(_local_mix_kernel, ts=ts, w=w),
        grid=(b, s // ts),
        in_specs=[pl.BlockSpec((None, ts, 6 * w), lambda bi, si: (bi, si, 0)),
                  full((CONF_WIDTH, w)), full((1, w)), full((1, w)), full((1, w)),
                  full((w, w)), full((1, w)), full((SCONV_WIDTH, w))],
        out_specs=pl.BlockSpec((None, ts, 3 * w), lambda bi, si: (bi, si, 0)),
        out_shape=jax.ShapeDtypeStruct((b, s, 3 * w), F32),
        scratch_shapes=[pltpu.VMEM((HALO + ts, 3 * w), F32)],
        compiler_params=_cparams("arbitrary", "arbitrary"),
        name="local_mix",
    )(proj3, conv_a_w, row(conv_a_b), row(norm_a_g), row(norm_a_b), pool_bd, row(pool_scale), conv_c_w)


def _attention_kernel(q_ref, k_ref, v_ref, m_ref, o_ref, acc_ref, car_ref, *, tq, scale):
    i = pl.program_id(2)
    q = q_ref[...]
    lane = lax.broadcasted_iota(I32, (tq, LANES), 1)
    q_heads = (jnp.where(lane < HEAD_DIM, q, 0.0).astype(BF16),
               jnp.where(lane >= HEAD_DIM, q, 0.0).astype(BF16))
    acc_ref[...] = jnp.zeros(acc_ref.shape, F32)
    car_ref[...] = jnp.zeros(car_ref.shape, F32)
    row = lax.broadcasted_iota(I32, (tq, tq), 0)
    col = lax.broadcasted_iota(I32, (tq, tq), 1)
    causal = col < row
    later = m_ref[...]

    def prepare(j, diagonal):
        start = pl.multiple_of(j * tq, tq)
        kb = k_ref[pl.ds(start, tq), :].astype(BF16)
        vb = v_ref[pl.ds(start, tq), :].astype(BF16)
        heads = []
        for h in range(2):
            z = _dot_nt(q_heads[h], kb) * scale
            softplus = jnp.maximum(z, 0.0) + jnp.log1p(jnp.exp(-jnp.abs(z)))
            log_stay = -softplus
            log_att = z - softplus
            if diagonal:
                log_stay = jnp.where(causal, log_stay, 0.0)
            hi = log_stay.astype(BF16)
            lo = (log_stay - hi.astype(F32)).astype(BF16)
            inside = (jnp.dot(hi, later, preferred_element_type=F32)
                      + jnp.dot(lo, later, preferred_element_type=F32))
            heads.append((log_att + inside, jnp.sum(log_stay, axis=1, keepdims=True)))
        return heads, vb

    def apply(prepared, diagonal):
        heads, vb = prepared
        for h, (log_w, stay_sum) in enumerate(heads):
            carried = car_ref[h]
            wgt = jnp.exp(log_w + carried)
            if diagonal:
                wgt = jnp.where(causal, wgt, 0.0)
            acc_ref[h] += jnp.dot(wgt.astype(BF16), vb, preferred_element_type=F32)
            car_ref[h] = carried + stay_sum

    @pl.when(i == 0)
    def _():
        apply(prepare(i, True), True)

    @pl.when(i > 0)
    def _():
        diagonal, neighbour = prepare(i, True), prepare(i - 1, False)
        apply(diagonal, True)
        apply(neighbour, False)

    def cond(c):
        j, worst = c
        return (j >= 0) & (worst > ATTN_UNDERFLOW)

    def body(c):
        j, _ = c
        apply(prepare(j, False), False)
        return j - 1, jnp.max(car_ref[...])

    lax.while_loop(cond, body, (i - 2, jnp.max(car_ref[...])))
    o_ref[...] = jnp.where(lane < HEAD_DIM, acc_ref[0], acc_ref[1])


def attention(proj3, later, tq, col0):
    b, s, _ = proj3.shape
    pairs = 2
    scale = HEAD_DIM ** -0.5
    return pl.pallas_call(
        functools.partial(_attention_kernel, tq=tq, scale=scale),
        grid=(b, pairs, s // tq),
        in_specs=[pl.BlockSpec((None, tq, LANES), lambda bi, p, i: (bi, i, col0 + p)),
                  pl.BlockSpec((None, s, LANES), lambda bi, p, i: (bi, 0, col0 + pairs + p)),
                  pl.BlockSpec((None, s, LANES), lambda bi, p, i: (bi, 0, col0 + 2 * pairs + p)),
                  pl.BlockSpec((tq, tq), lambda bi, p, i: (0, 0))],
        out_specs=pl.BlockSpec((None, tq, LANES), lambda bi, p, i: (bi, i, p)),
        out_shape=jax.ShapeDtypeStruct((b, s, pairs * LANES), F32),
        scratch_shapes=[pltpu.VMEM((2, tq, LANES), F32), pltpu.VMEM((2, tq, 1), F32)],
        compiler_params=_cparams("parallel", "parallel", "arbitrary"),
        name="attention",
    )(proj3, proj3, proj3, later)


def _out_ln_kernel(ya_ref, yd_ref, x_ref, w_ref, g_ref, b_ref, o_ref, *, alpha, wa):
    m = jnp.dot(ya_ref[...].astype(BF16), w_ref[0:wa, :], preferred_element_type=F32)
    m = m + jnp.dot(yd_ref[...].astype(BF16), w_ref[wa:, :], preferred_element_type=F32)
    o_ref[...] = _layer_norm(alpha * x_ref[...] + m, g_ref[...], b_ref[...])


def out_ln(y_abc, y_d, x2d, w_out_bf16, g, b, alpha, tm):
    t, d = x2d.shape
    wa, wd = y_abc.shape[1], y_d.shape[1]
    return pl.pallas_call(
        functools.partial(_out_ln_kernel, alpha=alpha, wa=wa),
        grid=(t // tm,),
        in_specs=[pl.BlockSpec((tm, wa), lambda i: (i, 0)),
                  pl.BlockSpec((tm, wd), lambda i: (i, 0)),
                  pl.BlockSpec((tm, d), lambda i: (i, 0)),
                  pl.BlockSpec((wa + wd, d), lambda i: (0, 0)),
                  pl.BlockSpec((1, d), lambda i: (0, 0)),
                  pl.BlockSpec((1, d), lambda i: (0, 0))],
        out_specs=pl.BlockSpec((tm, d), lambda i: (i, 0)),
        out_shape=jax.ShapeDtypeStruct((t, d), F32),
        compiler_params=_cparams("parallel"),
        name="out_ln",
    )(y_abc, y_d, x2d, w_out_bf16, g.reshape(1, d), b.reshape(1, d))


NEVER = 1e9


def _stack(rows):
    return jnp.concatenate(rows, axis=0)


def _merge_sort_network(n):
    pairs = []

    def merge(lo, size, r):
        step = 2 * r
        if step < size:
            merge(lo, size, step)
            merge(lo + r, size, step)
            pairs.extend((i, i + r) for i in range(lo + r, lo + size - r, step))
        else:
            pairs.append((lo, lo + r))

    def sort(lo, size):
        if size > 1:
            sort(lo, size // 2)
            sort(lo + size // 2, size // 2)
            merge(lo, size, 1)

    sort(0, n)
    return pairs


def _top_k_keys(scores, k):
    n, m = scores.shape
    sub = 8
    slabs = n // sub
    base = lax.broadcasted_iota(I32, (sub, m), 0).astype(F32)
    vals = [scores[v * sub:(v + 1) * sub, :] for v in range(slabs)]
    ids = [base + float(v * sub) for v in range(slabs)]
    for i, j in _merge_sort_network(slabs):
        better = (vals[j] > vals[i]) | ((vals[j] == vals[i]) & (ids[j] < ids[i]))
        vals[i], vals[j] = jnp.where(better, vals[j], vals[i]), jnp.where(better, vals[i], vals[j])
        ids[i], ids[j] = jnp.where(better, ids[j], ids[i]), jnp.where(better, ids[i], ids[j])
    out_v, out_i = [], []
    for step in range(k):
        best = jnp.max(vals[0], axis=0, keepdims=True)
        first = jnp.min(jnp.where(vals[0] == best, ids[0], NEVER), axis=0, keepdims=True)
        out_v.append(best)
        out_i.append(first)
        hit = ids[0] == first
        for depth in range(k - 1 - step):
            vals[depth] = jnp.where(hit, vals[depth + 1], vals[depth])
            ids[depth] = jnp.where(hit, ids[depth + 1], ids[depth])
    return out_v, out_i


def _top_k_pairs(s0, i0, s1, i1, tm):
    k = PEER_TOPK
    sub = 8
    a_low = lax.broadcasted_iota(I32, (sub, tm), 0).astype(F32)
    s0_low, i0_low = _stack(s0[:sub]), _stack(i0[:sub])
    vals, order, ident = [], [], []
    for b in range(k):
        reach = a_low <= float(k // (b + 1) - 1)
        vals.append(jnp.where(reach, s0_low + s1[b], -jnp.inf))
        order.append(jnp.where(reach, a_low * k + b, NEVER))
        ident.append(i0_low * PEER_NKEYS + i1[b])
    top_v = _stack(s0[sub:]) + s1[0]
    top_o = (a_low + sub) * k
    top_e = _stack(i0[sub:]) * PEER_NKEYS + i1[0]
    out_s, out_e = [], []
    for step in range(k):
        best = jnp.max(jnp.maximum(vals[0], top_v), axis=0, keepdims=True)
        first = jnp.min(jnp.minimum(jnp.where(vals[0] == best, order[0], NEVER),
                                    jnp.where(top_v == best, top_o, NEVER)), axis=0, keepdims=True)
        hit, top_hit = order[0] == first, top_o == first
        out_s.append(best)
        out_e.append(jnp.max(jnp.maximum(jnp.where(hit, ident[0], -1.0), jnp.where(top_hit, top_e, -1.0)),
                             axis=0, keepdims=True))
        top_v = jnp.where(top_hit, -jnp.inf, top_v)
        top_o = jnp.where(top_hit, NEVER, top_o)
        for depth in range(k - 1 - step):
            vals[depth] = jnp.where(hit, vals[depth + 1], vals[depth])
            order[depth] = jnp.where(hit, order[depth + 1], order[depth])
            ident[depth] = jnp.where(hit, ident[depth + 1], ident[depth])
    return out_s, out_e


def _route_kernel(x_ref, wq_ref, keys_ref, idx_ref, gate_ref, *, tm):
    half = PEER_NKEYS
    qt = _dot_nt(wq_ref[...], x_ref[...].astype(BF16)).astype(BF16)
    idx_rows, gate_rows = [], []
    for h in range(PEER_HEADS):
        tops = []
        for p in range(2):
            hp = 2 * h + p
            scores = jnp.dot(keys_ref[hp], qt[hp * half:(hp + 1) * half, :], preferred_element_type=F32)
            tops.append(_top_k_keys(scores, PEER_TOPK))
        (s0, i0), (s1, i1) = tops
        best_s, best_e = _top_k_pairs(s0, i0, s1, i1, tm)
        e = jnp.exp(_stack(best_s) - best_s[0])
        gate_rows.append(e / jnp.sum(e, axis=0, keepdims=True))
        idx_rows.extend(best_e)
    idx_ref[...] = _stack(idx_rows).astype(I32).T
    gate_ref[...] = _stack(gate_rows).T


def route(x2d, wq_t_bf16, keys_bf16, tm):
    t, d = x2d.shape
    nq = wq_t_bf16.shape[0]
    sel = PEER_HEADS * PEER_TOPK
    return pl.pallas_call(
        functools.partial(_route_kernel, tm=tm),
        grid=(t // tm,),
        in_specs=[pl.BlockSpec((tm, d), lambda i: (i, 0)),
                  pl.BlockSpec((nq, d), lambda i: (0, 0)),
                  pl.BlockSpec(keys_bf16.shape, lambda i: (0, 0, 0))],
        out_specs=[pl.BlockSpec((tm, sel), lambda i: (i, 0)),
                   pl.BlockSpec((tm, sel), lambda i: (i, 0))],
        out_shape=[jax.ShapeDtypeStruct((t, sel), I32), jax.ShapeDtypeStruct((t, sel), F32)],
        compiler_params=_cparams("parallel"),
        name="route",
    )(x2d, wq_t_bf16, keys_bf16)


PEER_PIECES = 4
MXU_ROWS = 8


def _peer_kernel(idx_now, idx_next, x_ref, gate_ref, g_ref, b_ref, tab_ref, o_ref, rows, sems,
                 *, c, sel, d, alpha):
    i = pl.program_id(0)
    n = pl.num_programs(0)
    piece = d // PEER_PIECES
    per = sel // (2 * PEER_PIECES)

    def start_rows(idx_ref, src_row, half, tk, first):
        for j in range(first, first + per):
            pltpu.make_async_copy(tab_ref.at[idx_ref[src_row, j]], rows.at[half, tk, pl.ds(j, 1), :],
                                  sems.at[half, tk]).start(priority=j % 2)

    def wait_token(half, tk):
        pltpu.make_async_copy(rows.at[1 - half, tk], rows.at[half, tk], sems.at[half, tk]).wait()

    @pl.when(i == 0)
    def _():
        for tk in range(c):
            for first in range(0, sel, per):
                start_rows(idx_now, tk, 0, tk, first)

    def score_piece(tok, xb, q):
        half, tk = divmod(tok, c)
        cols = slice(q * piece, (q + 1) * piece)
        u = lax.bitcast_convert_type(rows[half, tk, :, cols] << 16, F32).astype(BF16)
        return _dot_nt(xb[:, cols], u)

    def mix_piece(tok, act, q):
        half, tk = divmod(tok, c)
        cols = slice(q * piece, (q + 1) * piece)
        v = lax.bitcast_convert_type(rows[half, tk, :, cols] & jnp.uint32(0xFFFF0000), F32).astype(BF16)
        return jnp.dot(act, v, preferred_element_type=F32)

    groups = 2 * PEER_PIECES
    scored = None
    mixed = None
    for k in range(2 * c + 2):
        tok = k if k < 2 * c else None
        if tok is not None:
            half, tk = divmod(tok, c)
            ahead_ref, ahead_row = (idx_now, tok + c) if half == 0 else (idx_next, tk)
            wait_token(half, tk)
            x = jnp.broadcast_to(x_ref[tok:tok + 1, :], (MXU_ROWS, d))
            xb = x.astype(BF16)
        y_done = None
        if mixed is not None:
            m_tok, m_x, m_f = mixed
            y_done = (m_tok, _layer_norm(alpha * m_x + m_f, g_ref[...], b_ref[...]))
        act = None
        if scored is not None:
            s_tok, s_x, s_score = scored
            act = (jax.nn.gelu(s_score) * gate_ref[s_tok:s_tok + 1, :]).astype(BF16)
        score = jnp.zeros((MXU_ROWS, sel), F32)
        parts = []
        issued = 0
        for batch in range(2):
            for q in range(batch * PEER_PIECES // 2, (batch + 1) * PEER_PIECES // 2):
                if act is not None:
                    parts.append(mix_piece(s_tok, act, q))
                if tok is not None:
                    score = score + score_piece(tok, xb, q)
            if tok is not None:
                for _ in range(groups // 4 if batch == 0 else groups - groups // 4):
                    start_rows(ahead_ref, ahead_row, 1 - half, tk, issued * per)
                    issued += 1
        if y_done is not None:
            o_ref[y_done[0]:y_done[0] + 1, :] = y_done[1][0:1, :]
        mixed = (s_tok, s_x, jnp.concatenate(parts, axis=1)) if act is not None else None
        scored = (tok, x, score) if tok is not None else None

    @pl.when(i == n - 1)
    def _():
        for tk in range(c):
            wait_token(0, tk)


def peer(x2d, idx, gates, table, g, b, alpha, c):
    t, d = x2d.shape
    sel = idx.shape[1]
    n = t // (2 * c)
    return pl.pallas_call(
        functools.partial(_peer_kernel, c=c, sel=sel, d=d, alpha=alpha),
        grid=(n,),
        in_specs=[pl.BlockSpec((2 * c, sel), lambda i: (i, 0), memory_space=pltpu.SMEM),
                  pl.BlockSpec((2 * c, sel), lambda i: (jnp.minimum(i + 1, n - 1), 0), memory_space=pltpu.SMEM),
                  pl.BlockSpec((2 * c, d), lambda i: (i, 0)),
                  pl.BlockSpec((2 * c, sel), lambda i: (i, 0)),
                  pl.BlockSpec((1, d), lambda i: (0, 0)),
                  pl.BlockSpec((1, d), lambda i: (0, 0)),
                  pl.BlockSpec(memory_space=pl.ANY)],
        out_specs=pl.BlockSpec((2 * c, d), lambda i: (i, 0)),
        out_shape=jax.ShapeDtypeStruct((t, d), F32),
        scratch_shapes=[pltpu.VMEM((2, c, sel, d), jnp.uint32), pltpu.SemaphoreType.DMA((2, c))],
        compiler_params=_cparams("arbitrary"),
        name="peer",
    )(idx, idx, x2d, gates, g.reshape(1, d), b.reshape(1, d), table)


SC_LANES = 16
SC_WORKERS = 32


def _sc_peer_body(x_hbm, idx_hbm, gate_hbm, tab_hbm, f_hbm, x_v, idx_v, gate_v, out_v, rows_v,
                  row_sems, in_sems, out_sems, *, per_worker, d, sel):
    heads = sel // PEER_TOPK
    chunks = d // SC_LANES
    wid = lax.axis_index("s") * 2 + lax.axis_index("c")
    first = wid * per_worker
    last = first + per_worker - 1

    def gather(p, h):
        return pltpu.make_async_copy(tab_hbm.at[idx_v.at[p, pl.ds(h * PEER_TOPK, PEER_TOPK)]],
                                     rows_v.at[h % 2], row_sems.at[h % 2])

    def inputs(g, p):
        return (pltpu.make_async_copy(idx_hbm.at[pl.ds(g * sel, sel)], idx_v.at[p], in_sems.at[p, 0]),
                pltpu.make_async_copy(gate_hbm.at[pl.ds(g * sel, sel)], gate_v.at[p], in_sems.at[p, 1]),
                pltpu.make_async_copy(x_hbm.at[pl.ds(g * d, d)], x_v.at[p], in_sems.at[p, 2]))

    def result(g, p):
        return pltpu.make_async_copy(out_v.at[p], f_hbm.at[pl.ds(g * d, d)], out_sems.at[p])

    def token(g, p, have_older):
        following = jnp.minimum(g + 1, last)
        for copy in inputs(following, 1 - p):
            copy.start()

        @pl.when(have_older)
        def _():
            result(g, p).wait()

        def clear(ci, c2):
            out_v[p, pl.ds(ci * SC_LANES, SC_LANES)] = jnp.zeros((SC_LANES,), F32)
            return c2
        lax.fori_loop(0, chunks, clear, 0)

        for h in range(heads):
            slot = h % 2
            if h + 1 < heads:
                gather(p, h + 1).start()
            else:
                for copy in inputs(following, 1 - p):
                    copy.wait()
                gather(1 - p, 0).start()
            gather(p, h).wait()

            def dots(ci, accs):
                xc = x_v[p, pl.ds(ci * SC_LANES, SC_LANES)]
                out = []
                for r in range(PEER_TOPK):
                    w = rows_v[slot, r, pl.ds(ci * SC_LANES, SC_LANES)]
                    out.append(accs[r] + lax.bitcast_convert_type(w << 16, F32) * xc)
                return tuple(out)
            accs = lax.fori_loop(0, chunks, dots, tuple(jnp.zeros((SC_LANES,), F32) for _ in range(PEER_TOPK)))
            lane = lax.iota(I32, SC_LANES)
            score = jnp.zeros((SC_LANES,), F32)
            for r in range(PEER_TOPK):
                score = jnp.where(lane == r, jnp.sum(accs[r]), score)
            inner = 0.7978845608028654 * (score + 0.044715 * score * score * score)
            tanh = 1.0 - 2.0 / (jnp.exp(2.0 * inner) + 1.0)
            act = 0.5 * score * (1.0 + tanh) * gate_v[p, pl.ds(h * PEER_TOPK, PEER_TOPK)]
            weights = [jnp.sum(jnp.where(lane == r, act, 0.0)) for r in range(PEER_TOPK)]

            def mix(ci, c2):
                part = [out_v[p, pl.ds(ci * SC_LANES, SC_LANES)]] + [jnp.zeros((SC_LANES,), F32)] * 3
                for r in range(PEER_TOPK):
                    w = rows_v[slot, r, pl.ds(ci * SC_LANES, SC_LANES)]
                    part[r % 4] = part[r % 4] + weights[r] * lax.bitcast_convert_type(
                        w & jnp.uint32(0xFFFF0000), F32)
                out_v[p, pl.ds(ci * SC_LANES, SC_LANES)] = (part[0] + part[1]) + (part[2] + part[3])
                return c2
            lax.fori_loop(0, chunks, mix, 0)
        result(g, p).start()

    for copy in inputs(first, 0):
        copy.start()
    for copy in inputs(first, 0):
        copy.wait()
    gather(0, 0).start()

    def pair(k, carry):
        token(first + 2 * k, 0, k > 0)
        token(first + 2 * k + 1, 1, k > 0)
        return carry
    lax.fori_loop(0, per_worker // 2, pair, 0)

    gather(0, 0).wait()
    result(last, 0).wait()
    result(last, 1).wait()


def sc_peer(x_flat, idx_flat, gate_flat, table2d):
    from jax.experimental.pallas import tpu_sc as plsc
    d = table2d.shape[1]
    tokens = x_flat.shape[0] // d
    sel = idx_flat.shape[0] // tokens
    per_worker = tokens // SC_WORKERS
    mesh = plsc.VectorSubcoreMesh(core_axis_name="c", subcore_axis_name="s")
    return pl.kernel(
        functools.partial(_sc_peer_body, per_worker=per_worker, d=d, sel=sel),
        out_type=jax.ShapeDtypeStruct((tokens * d,), F32),
        mesh=mesh,
        scratch_types=[pltpu.VMEM((2, d), F32), pltpu.VMEM((2, sel), I32), pltpu.VMEM((2, sel), F32),
                       pltpu.VMEM((2, d), F32), pltpu.VMEM((2, PEER_TOPK, d), jnp.uint32),
                       pltpu.SemaphoreType.DMA((2,)), pltpu.SemaphoreType.DMA((2, 3)),
                       pltpu.SemaphoreType.DMA((2,))],
        compiler_params=pltpu.CompilerParams(needs_layout_passes=False),
        name="sc_peer",
    )(x_flat, idx_flat, gate_flat, table2d)


def _block_diag(pool_w):
    g, c, _ = pool_w.shape
    out = jnp.zeros((g * c, g * c), pool_w.dtype)
    for k in range(g):
        out = out.at[k * c:(k + 1) * c, k * c:(k + 1) * c].set(pool_w[k])
    return out


def _pack_rows(u, v):
    bits = lambda a: lax.bitcast_convert_type(a.astype(BF16), jnp.uint16).astype(jnp.uint32)
    return bits(u) | (bits(v) << 16)


def _residual_ln_kernel(x_ref, f_ref, g_ref, b_ref, o_ref, *, alpha):
    o_ref[...] = _layer_norm(alpha * x_ref[...] + f_ref[...], g_ref[...], b_ref[...])


def residual_ln(x2d, f2d, g, b, alpha, tm):
    t, d = x2d.shape
    return pl.pallas_call(
        functools.partial(_residual_ln_kernel, alpha=alpha),
        grid=(t // tm,),
        in_specs=[pl.BlockSpec((tm, d), lambda i: (i, 0)),
                  pl.BlockSpec((tm, d), lambda i: (i, 0)),
                  pl.BlockSpec((1, d), lambda i: (0, 0)),
                  pl.BlockSpec((1, d), lambda i: (0, 0))],
        out_specs=pl.BlockSpec((tm, d), lambda i: (i, 0)),
        out_shape=jax.ShapeDtypeStruct((t, d), F32),
        compiler_params=_cparams("parallel"),
        name="residual_ln",
    )(x2d, f2d, g.reshape(1, d), b.reshape(1, d))


def _sc_batches(b):
    return b // 2


def kernel(x, w_in, conv_a_w, conv_a_b, norm_a_g, norm_a_b, pool_w, pool_scale, conv_c_w, w_out,
           ln1_g, ln1_b, peer_wq, peer_keys, peer_u, peer_v, ln2_g, ln2_b):
    depth = w_in.shape[0]
    b, s, d = x.shape
    alpha = float((2 * depth) ** 0.25)
    w = conv_a_w.shape[2]
    tm = min(512, s)
    ts = min(512, s)
    tq = min(256, s)
    t_route = min(256, s)
    peer_half = 16
    row = lax.broadcasted_iota(I32, (tq, tq), 0)
    col = lax.broadcasted_iota(I32, (tq, tq), 1)
    later = (row > col).astype(BF16)
    b_tc = b - _sc_batches(b)
    for l in range(depth):
        w_in_l, w_out_l = w_in[l].astype(BF16), w_out[l].astype(BF16)
        pool_bd = _block_diag(pool_w[l]).astype(BF16)
        keys = peer_keys[l].reshape(2 * PEER_HEADS, PEER_NKEYS, -1).astype(BF16)
        wq_t = peer_wq[l].T.astype(BF16)
        words = _pack_rows(peer_u[l], peer_v[l])

        def front(xp):
            tp = xp.shape[0] * s
            xpf = xp.reshape(tp, d)
            proj3 = in_proj(xpf, w_in_l, tm).reshape(xp.shape[0], s, -1)
            y_abc = local_mix(proj3, conv_a_w[l], conv_a_b[l], norm_a_g[l], norm_a_b[l],
                              pool_bd, pool_scale[l], conv_c_w[l], ts)
            y_d = attention(proj3, later, tq, (6 * w) // LANES)
            x1 = out_ln(y_abc.reshape(tp, 3 * w), y_d.reshape(tp, -1), xpf, w_out_l,
                        ln1_g[l], ln1_b[l], alpha, tm)
            idx, gates = route(x1, wq_t, keys, t_route)
            return x1, idx, gates

        outs = []
        if b_tc < b:
            x1_sc, idx_sc, gates_sc = front(x[b_tc:])
            f_sc = sc_peer(x1_sc.reshape(-1), idx_sc.reshape(-1), gates_sc.reshape(-1), words)
        x1_tc, idx_tc, gates_tc = front(x[:b_tc])
        outs.append(peer(x1_tc, idx_tc, gates_tc, words[:, None, :], ln2_g[l], ln2_b[l], alpha, peer_half))
        if b_tc < b:
            outs.append(residual_ln(x1_sc, f_sc.reshape(-1, d), ln2_g[l], ln2_b[l], alpha, tm))
        x = jnp.concatenate(outs, axis=0).reshape(b, s, d)
    return x
```

```python
import functools

import jax
import jax.numpy as jnp
from jax import lax
from jax.experimental import pallas as pl
from jax.experimental.pallas import tpu as pltpu

F32 = jnp.float32
BF16 = jnp.bfloat16
I32 = jnp.int32

LN_EPS = 1e-5
LANES = 128
CONF_WIDTH = 31
POOL_WINDOWS = (2, 4, 8, 16)
SCONV_WIDTH = 3
HEAD_DIM = 64
PEER_HEADS = 8
PEER_NKEYS = 128
PEER_TOPK = 16
HALO = 32
ATTN_UNDERFLOW = -100.0
VMEM_LIMIT = 56 * 1024 * 1024


def _cparams(*sem):
    return pltpu.CompilerParams(dimension_semantics=sem, vmem_limit_bytes=VMEM_LIMIT)


def _layer_norm(h, g, b):
    mu = jnp.mean(h, axis=-1, keepdims=True)
    hc = h - mu
    var = jnp.mean(hc * hc, axis=-1, keepdims=True)
    return hc * lax.rsqrt(var + LN_EPS) * g + b


def _dot_nt(a, b):
    return lax.dot_general(a, b, (((1,), (1,)), ((), ())), preferred_element_type=F32)


def _in_proj_kernel(x_ref, w_ref, o_ref):
    o_ref[...] = jnp.dot(x_ref[...].astype(BF16), w_ref[...], preferred_element_type=F32)


def in_proj(x2d, w_bf16, tm):
    t, d = x2d.shape
    n = w_bf16.shape[1]
    return pl.pallas_call(
        _in_proj_kernel,
        grid=(t // tm,),
        in_specs=[pl.BlockSpec((tm, d), lambda i: (i, 0)),
                  pl.BlockSpec((d, n), lambda i: (0, 0))],
        out_specs=pl.BlockSpec((tm, n), lambda i: (i, 0)),
        out_shape=jax.ShapeDtypeStruct((t, n), F32),
        compiler_params=_cparams("parallel"),
        name="in_proj",
    )(x2d, w_bf16)


def _local_mix_kernel(p_ref, caw_ref, cab_ref, nag_ref, nab_ref, pw_ref, ps_ref, ccw_ref,
                      o_ref, hist, *, ts, w):
    s = pl.program_id(1)

    @pl.when(s == 0)
    def _():
        hist[0:HALO, :] = jnp.zeros((HALO, 3 * w), F32)

    a_val = p_ref[:, 0:w]
    a_gate = p_ref[:, w:2 * w]
    c_h = p_ref[:, 3 * w:4 * w]
    c_gate_b = p_ref[:, 4 * w:5 * w]
    c_gate_c = p_ref[:, 5 * w:6 * w]
    hist[HALO:HALO + ts, 0:w] = a_val * jax.nn.sigmoid(a_gate)
    hist[HALO:HALO + ts, w:2 * w] = p_ref[:, 2 * w:3 * w]
    hist[HALO:HALO + ts, 2 * w:3 * w] = c_gate_c * c_h

    def past(col, back):
        return hist[HALO - back:HALO - back + ts, col * w:(col + 1) * w]

    acc = jnp.zeros((ts, w), F32)
    for k in range(CONF_WIDTH):
        acc = acc + caw_ref[k:k + 1, :] * past(0, CONF_WIDTH - 1 - k)
    h = _layer_norm(acc + cab_ref[...], nag_ref[...], nab_ref[...])
    o_ref[:, 0:w] = h * jax.nn.sigmoid(h)

    t_pos = s * ts + lax.broadcasted_iota(I32, (ts, 1), 0)
    chan = lax.broadcasted_iota(I32, (ts, w), 1)
    group = w // len(POOL_WINDOWS)
    x0 = past(1, 0)
    run = x0
    pooled = jnp.zeros((ts, w), F32)
    done = 1
    for g, win in enumerate(POOL_WINDOWS):
        for back in range(done, win):
            run = run + past(1, back)
        done = win
        inv = 1.0 / jnp.minimum(t_pos + 1, win).astype(F32)
        in_group = (chan >= g * group) & (chan < (g + 1) * group)
        pooled = jnp.where(in_group, run * inv, pooled)
    pooled = pooled - x0
    mixed = jnp.dot(pooled.astype(BF16), pw_ref[...], preferred_element_type=F32)
    o_ref[:, w:2 * w] = mixed * ps_ref[...]

    conv = jnp.zeros((ts, w), F32)
    for k in range(SCONV_WIDTH):
        conv = conv + ccw_ref[k:k + 1, :] * past(2, SCONV_WIDTH - 1 - k)
    o_ref[:, 2 * w:3 * w] = c_gate_b * conv

    hist[0:HALO, :] = hist[ts:ts + HALO, :]


def local_mix(proj3, conv_a_w, conv_a_b, norm_a_g, norm_a_b, pool_bd, pool_scale, conv_c_w, ts):
    b, s, _ = proj3.shape
    w = conv_a_w.shape[1]
    row = lambda a: a.reshape(1, w)
    full = lambda shape: pl.BlockSpec(shape, lambda bi, si: (0,) * len(shape))
    return pl.pallas_call(
        functools.partial(_local_mix_kernel, ts=ts, w=w),
        grid=(b, s // ts),
        in_specs=[pl.BlockSpec((None, ts, 6 * w), lambda bi, si: (bi, si, 0)),
                  full((CONF_WIDTH, w)), full((1, w)), full((1, w)), full((1, w)),
                  full((w, w)), full((1, w)), full((SCONV_WIDTH, w))],
        out_specs=pl.BlockSpec((None, ts, 3 * w), lambda bi, si: (bi, si, 0)),
        out_shape=jax.ShapeDtypeStruct((b, s, 3 * w), F32),
        scratch_shapes=[pltpu.VMEM((HALO + ts, 3 * w), F32)],
        compiler_params=_cparams("arbitrary", "arbitrary"),
        name="local_mix",
    )(proj3, conv_a_w, row(conv_a_b), row(norm_a_g), row(norm_a_b), pool_bd, row(pool_scale), conv_c_w)


def _attention_kernel(q_ref, k_ref, v_ref, m_ref, o_ref, acc_ref, car_ref, *, tq, scale):
    i = pl.program_id(2)
    q = q_ref[...]
    lane = lax.broadcasted_iota(I32, (tq, LANES), 1)
    q_heads = (jnp.where(lane < HEAD_DIM, q, 0.0).astype(BF16),
               jnp.where(lane >= HEAD_DIM, q, 0.0).astype(BF16))
    acc_ref[...] = jnp.zeros(acc_ref.shape, F32)
    car_ref[...] = jnp.zeros(car_ref.shape, F32)
    row = lax.broadcasted_iota(I32, (tq, tq), 0)
    col = lax.broadcasted_iota(I32, (tq, tq), 1)
    causal = col < row
    later = m_ref[...]

    def prepare(j, diagonal):
        start = pl.multiple_of(j * tq, tq)
        kb = k_ref[pl.ds(start, tq), :].astype(BF16)
        vb = v_ref[pl.ds(start, tq), :].astype(BF16)
        heads = []
        for h in range(2):
            z = _dot_nt(q_heads[h], kb) * scale
            softplus = jnp.maximum(z, 0.0) + jnp.log1p(jnp.exp(-jnp.abs(z)))
            log_stay = -softplus
            log_att = z - softplus
            if diagonal:
                log_stay = jnp.where(causal, log_stay, 0.0)
            hi = log_stay.astype(BF16)
            lo = (log_stay - hi.astype(F32)).astype(BF16)
            inside = (jnp.dot(hi, later, preferred_element_type=F32)
                      + jnp.dot(lo, later, preferred_element_type=F32))
            heads.append((log_att + inside, jnp.sum(log_stay, axis=1, keepdims=True)))
        return heads, vb

    def apply(prepared, diagonal):
        heads, vb = prepared
        for h, (log_w, stay_sum) in enumerate(heads):
            carried = car_ref[h]
            wgt = jnp.exp(log_w + carried)
            if diagonal:
                wgt = jnp.where(causal, wgt, 0.0)
            acc_ref[h] += jnp.dot(wgt.astype(BF16), vb, preferred_element_type=F32)
            car_ref[h] = carried + stay_sum

    @pl.when(i == 0)
    def _():
        apply(prepare(i, True), True)

    @pl.when(i > 0)
    def _():
        diagonal, neighbour = prepare(i, True), prepare(i - 1, False)
        apply(diagonal, True)
        apply(neighbour, False)

    def cond(c):
        j, worst = c
        return (j >= 0) & (worst > ATTN_UNDERFLOW)

    def body(c):
        j, _ = c
        apply(prepare(j, False), False)
        return j - 1, jnp.max(car_ref[...])

    lax.while_loop(cond, body, (i - 2, jnp.max(car_ref[...])))
    o_ref[...] = jnp.where(lane < HEAD_DIM, acc_ref[0], acc_ref[1])


def attention(proj3, later, tq, col0):
    b, s, _ = proj3.shape
    pairs = 2
    scale = HEAD_DIM ** -0.5
    return pl.pallas_call(
        functools.partial(_attention_kernel, tq=tq, scale=scale),
        grid=(b, pairs, s // tq),
        in_specs=[pl.BlockSpec((None, tq, LANES), lambda bi, p, i: (bi, i, col0 + p)),
                  pl.BlockSpec((None, s, LANES), lambda bi, p, i: (bi, 0, col0 + pairs + p)),
                  pl.BlockSpec((None, s, LANES), lambda bi, p, i: (bi, 0, col0 + 2 * pairs + p)),
                  pl.BlockSpec((tq, tq), lambda bi, p, i: (0, 0))],
        out_specs=pl.BlockSpec((None, tq, LANES), lambda bi, p, i: (bi, i, p)),
        out_shape=jax.ShapeDtypeStruct((b, s, pairs * LANES), F32),
        scratch_shapes=[pltpu.VMEM((2, tq, LANES), F32), pltpu.VMEM((2, tq, 1), F32)],
        compiler_params=_cparams("parallel", "parallel", "arbitrary"),
        name="attention",
    )(proj3, proj3, proj3, later)


def _out_ln_kernel(ya_ref, yd_ref, x_ref, w_ref, g_ref, b_ref, o_ref, *, alpha, wa):
    m = jnp.dot(ya_ref[...].astype(BF16), w_ref[0:wa, :], preferred_element_type=F32)
    m = m + jnp.dot(yd_ref[...].astype(BF16), w_ref[wa:, :], preferred_element_type=F32)
    o_ref[...] = _layer_norm(alpha * x_ref[...] + m, g_ref[...], b_ref[...])


def out_ln(y_abc, y_d, x2d, w_out_bf16, g, b, alpha, tm):
    t, d = x2d.shape
    wa, wd = y_abc.shape[1], y_d.shape[1]
    return pl.pallas_call(
        functools.partial(_out_ln_kernel, alpha=alpha, wa=wa),
        grid=(t // tm,),
        in_specs=[pl.BlockSpec((tm, wa), lambda i: (i, 0)),
                  pl.BlockSpec((tm, wd), lambda i: (i, 0)),
                  pl.BlockSpec((tm, d), lambda i: (i, 0)),
                  pl.BlockSpec((wa + wd, d), lambda i: (0, 0)),
                  pl.BlockSpec((1, d), lambda i: (0, 0)),
                  pl.BlockSpec((1, d), lambda i: (0, 0))],
        out_specs=pl.BlockSpec((tm, d), lambda i: (i, 0)),
        out_shape=jax.ShapeDtypeStruct((t, d), F32),
        compiler_params=_cparams("parallel"),
        name="out_ln",
    )(y_abc, y_d, x2d, w_out_bf16, g.reshape(1, d), b.reshape(1, d))


NEVER = 1e9


def _stack(rows):
    return jnp.concatenate(rows, axis=0)


def _merge_sort_network(n):
    pairs = []

    def merge(lo, size, r):
        step = 2 * r
        if step < size:
            merge(lo, size, step)
            merge(lo + r, size, step)
            pairs.extend((i, i + r) for i in range(lo + r, lo + size - r, step))
        else:
            pairs.append((lo, lo + r))

    def sort(lo, size):
        if size > 1:
            sort(lo, size // 2)
            sort(lo + size // 2, size // 2)
            merge(lo, size, 1)

    sort(0, n)
    return pairs


def _top_k_keys(scores, k):
    n, m = scores.shape
    sub = 8
    slabs = n // sub
    base = lax.broadcasted_iota(I32, (sub, m), 0).astype(F32)
    vals = [scores[v * sub:(v + 1) * sub, :] for v in range(slabs)]
    ids = [base + float(v * sub) for v in range(slabs)]
    for i, j in _merge_sort_network(slabs):
        better = (vals[j] > vals[i]) | ((vals[j] == vals[i]) & (ids[j] < ids[i]))
        vals[i], vals[j] = jnp.where(better, vals[j], vals[i]), jnp.where(better, vals[i], vals[j])
        ids[i], ids[j] = jnp.where(better, ids[j], ids[i]), jnp.where(better, ids[i], ids[j])
    out_v, out_i = [], []
    for step in range(k):
        best = jnp.max(vals[0], axis=0, keepdims=True)
        first = jnp.min(jnp.where(vals[0] == best, ids[0], NEVER), axis=0, keepdims=True)
        out_v.append(best)
        out_i.append(first)
        hit = ids[0] == first
        for depth in range(k - 1 - step):
            vals[depth] = jnp.where(hit, vals[depth + 1], vals[depth])
            ids[depth] = jnp.where(hit, ids[depth + 1], ids[depth])
    return out_v, out_i


def _top_k_pairs(s0, i0, s1, i1, tm):
    k = PEER_TOPK
    sub = 8
    a_low = lax.broadcasted_iota(I32, (sub, tm), 0).astype(F32)
    s0_low, i0_low = _stack(s0[:sub]), _stack(i0[:sub])
    vals, order, ident = [], [], []
    for b in range(k):
        reach = a_low <= float(k // (b + 1) - 1)
        vals.append(jnp.where(reach, s0_low + s1[b], -jnp.inf))
        order.append(jnp.where(reach, a_low * k + b, NEVER))
        ident.append(i0_low * PEER_NKEYS + i1[b])
    top_v = _stack(s0[sub:]) + s1[0]
    top_o = (a_low + sub) * k
    top_e = _stack(i0[sub:]) * PEER_NKEYS + i1[0]
    out_s, out_e = [], []
    for step in range(k):
        best = jnp.max(jnp.maximum(vals[0], top_v), axis=0, keepdims=True)
        first = jnp.min(jnp.minimum(jnp.where(vals[0] == best, order[0], NEVER),
                                    jnp.where(top_v == best, top_o, NEVER)), axis=0, keepdims=True)
        hit, top_hit = order[0] == first, top_o == first
        out_s.append(best)
        out_e.append(jnp.max(jnp.maximum(jnp.where(hit, ident[0], -1.0), jnp.where(top_hit, top_e, -1.0)),
                             axis=0, keepdims=True))
        top_v = jnp.where(top_hit, -jnp.inf, top_v)
        top_o = jnp.where(top_hit, NEVER, top_o)
        for depth in range(k - 1 - step):
            vals[depth] = jnp.where(hit, vals[depth + 1], vals[depth])
            order[depth] = jnp.where(hit, order[depth + 1], order[depth])
            ident[depth] = jnp.where(hit, ident[depth + 1], ident[depth])
    return out_s, out_e


def _route_kernel(x_ref, wq_ref, keys_ref, idx_ref, gate_ref, *, tm):
    half = PEER_NKEYS
    qt = _dot_nt(wq_ref[...], x_ref[...].astype(BF16)).astype(BF16)
    idx_rows, gate_rows = [], []
    for h in range(PEER_HEADS):
        tops = []
        for p in range(2):
            hp = 2 * h + p
            scores = jnp.dot(keys_ref[hp], qt[hp * half:(hp + 1) * half, :], preferred_element_type=F32)
            tops.append(_top_k_keys(scores, PEER_TOPK))
        (s0, i0), (s1, i1) = tops
        best_s, best_e = _top_k_pairs(s0, i0, s1, i1, tm)
        e = jnp.exp(_stack(best_s) - best_s[0])
        gate_rows.append(e / jnp.sum(e, axis=0, keepdims=True))
        idx_rows.extend(best_e)
    idx_ref[...] = _stack(idx_rows).astype(I32).T
    gate_ref[...] = _stack(gate_rows).T


def route(x2d, wq_t_bf16, keys_bf16, tm):
    t, d = x2d.shape
    nq = wq_t_bf16.shape[0]
    sel = PEER_HEADS * PEER_TOPK
    return pl.pallas_call(
        functools.partial(_route_kernel, tm=tm),
        grid=(t // tm,),
        in_specs=[pl.BlockSpec((tm, d), lambda i: (i, 0)),
                  pl.BlockSpec((nq, d), lambda i: (0, 0)),
                  pl.BlockSpec(keys_bf16.shape, lambda i: (0, 0, 0))],
        out_specs=[pl.BlockSpec((tm, sel), lambda i: (i, 0)),
                   pl.BlockSpec((tm, sel), lambda i: (i, 0))],
        out_shape=[jax.ShapeDtypeStruct((t, sel), I32), jax.ShapeDtypeStruct((t, sel), F32)],
        compiler_params=_cparams("parallel"),
        name="route",
    )(x2d, wq_t_bf16, keys_bf16)


PEER_PIECES = 4
MXU_ROWS = 8


def _peer_kernel(idx_now, idx_next, x_ref, gate_ref, g_ref, b_ref, tab_ref, o_ref, rows, sems,
                 *, c, sel, d, alpha):
    i = pl.program_id(0)
    n = pl.num_programs(0)
    piece = d // PEER_PIECES
    per = sel // (2 * PEER_PIECES)

    def start_rows(idx_ref, src_row, half, tk, first):
        for j in range(first, first + per):
            pltpu.make_async_copy(tab_ref.at[idx_ref[src_row, j]], rows.at[half, tk, pl.ds(j, 1), :],
                                  sems.at[half, tk]).start(priority=j % 2)

    def wait_token(half, tk):
        pltpu.make_async_copy(rows.at[1 - half, tk], rows.at[half, tk], sems.at[half, tk]).wait()

    @pl.when(i == 0)
    def _():
        for tk in range(c):
            for first in range(0, sel, per):
                start_rows(idx_now, tk, 0, tk, first)

    def score_piece(tok, xb, q):
        half, tk = divmod(tok, c)
        cols = slice(q * piece, (q + 1) * piece)
        u = lax.bitcast_convert_type(rows[half, tk, :, cols] << 16, F32).astype(BF16)
        return _dot_nt(xb[:, cols], u)

    def mix_piece(tok, act, q):
        half, tk = divmod(tok, c)
        cols = slice(q * piece, (q + 1) * piece)
        v = lax.bitcast_convert_type(rows[half, tk, :, cols] & jnp.uint32(0xFFFF0000), F32).astype(BF16)
        return jnp.dot(act, v, preferred_element_type=F32)

    groups = 2 * PEER_PIECES
    scored = None
    mixed = None
    for k in range(2 * c + 2):
        tok = k if k < 2 * c else None
        if tok is not None:
            half, tk = divmod(tok, c)
            ahead_ref, ahead_row = (idx_now, tok + c) if half == 0 else (idx_next, tk)
            wait_token(half, tk)
            x = jnp.broadcast_to(x_ref[tok:tok + 1, :], (MXU_ROWS, d))
            xb = x.astype(BF16)
        y_done = None
        if mixed is not None:
            m_tok, m_x, m_f = mixed
            y_done = (m_tok, _layer_norm(alpha * m_x + m_f, g_ref[...], b_ref[...]))
        act = None
        if scored is not None:
            s_tok, s_x, s_score = scored
            act = (jax.nn.gelu(s_score) * gate_ref[s_tok:s_tok + 1, :]).astype(BF16)
        score = jnp.zeros((MXU_ROWS, sel), F32)
        parts = []
        issued = 0
        for batch in range(2):
            for q in range(batch * PEER_PIECES // 2, (batch + 1) * PEER_PIECES // 2):
                if act is not None:
                    parts.append(mix_piece(s_tok, act, q))
                if tok is not None:
                    score = score + score_piece(tok, xb, q)
            if tok is not None:
                for _ in range(groups // 4 if batch == 0 else groups - groups // 4):
                    start_rows(ahead_ref, ahead_row, 1 - half, tk, issued * per)
                    issued += 1
        if y_done is not None:
            o_ref[y_done[0]:y_done[0] + 1, :] = y_done[1][0:1, :]
        mixed = (s_tok, s_x, jnp.concatenate(parts, axis=1)) if act is not None else None
        scored = (tok, x, score) if tok is not None else None

    @pl.when(i == n - 1)
    def _():
        for tk in range(c):
            wait_token(0, tk)


def peer(x2d, idx, gates, table, g, b, alpha, c, out_rows=None):
    t, d = x2d.shape
    sel = idx.shape[1]
    n = t // (2 * c)
    return pl.pallas_call(
        functools.partial(_peer_kernel, c=c, sel=sel, d=d, alpha=alpha),
        grid=(n,),
        in_specs=[pl.BlockSpec((2 * c, sel), lambda i: (i, 0), memory_space=pltpu.SMEM),
                  pl.BlockSpec((2 * c, sel), lambda i: (jnp.minimum(i + 1, n - 1), 0), memory_space=pltpu.SMEM),
                  pl.BlockSpec((2 * c, d), lambda i: (i, 0)),
                  pl.BlockSpec((2 * c, sel), lambda i: (i, 0)),
                  pl.BlockSpec((1, d), lambda i: (0, 0)),
                  pl.BlockSpec((1, d), lambda i: (0, 0)),
                  pl.BlockSpec(memory_space=pl.ANY)],
        out_specs=pl.BlockSpec((2 * c, d), lambda i: (i, 0)),
        out_shape=jax.ShapeDtypeStruct((out_rows or t, d), F32),
        scratch_shapes=[pltpu.VMEM((2, c, sel, d), jnp.uint32), pltpu.SemaphoreType.DMA((2, c))],
        compiler_params=_cparams("arbitrary"),
        name="peer",
    )(idx, idx, x2d, gates, g.reshape(1, d), b.reshape(1, d), table)


SC_LANES = 16
SC_WORKERS = 32


def _sc_peer_body(x_hbm, idx_hbm, gate_hbm, tab_hbm, f_hbm, x_v, idx_v, gate_v, out_v, rows_v,
                  row_sems, in_sems, out_sems, *, per_worker, d, sel):
    heads = sel // PEER_TOPK
    chunks = d // SC_LANES
    wid = lax.axis_index("s") * 2 + lax.axis_index("c")
    first = wid * per_worker
    last = first + per_worker - 1

    def gather(p, h):
        return pltpu.make_async_copy(tab_hbm.at[idx_v.at[p, pl.ds(h * PEER_TOPK, PEER_TOPK)]],
                                     rows_v.at[h % 2], row_sems.at[h % 2])

    def inputs(g, p):
        return (pltpu.make_async_copy(idx_hbm.at[pl.ds(g * sel, sel)], idx_v.at[p], in_sems.at[p, 0]),
                pltpu.make_async_copy(gate_hbm.at[pl.ds(g * sel, sel)], gate_v.at[p], in_sems.at[p, 1]),
                pltpu.make_async_copy(x_hbm.at[pl.ds(g * d, d)], x_v.at[p], in_sems.at[p, 2]))

    def result(g, p):
        return pltpu.make_async_copy(out_v.at[p], f_hbm.at[pl.ds(g * d, d)], out_sems.at[p])

    def token(g, p, have_older):
        following = jnp.minimum(g + 1, last)
        for copy in inputs(following, 1 - p):
            copy.start()

        @pl.when(have_older)
        def _():
            result(g, p).wait()

        def clear(ci, c2):
            out_v[p, pl.ds(ci * SC_LANES, SC_LANES)] = jnp.zeros((SC_LANES,), F32)
            return c2
        lax.fori_loop(0, chunks, clear, 0)

        for h in range(heads):
            slot = h % 2
            if h + 1 < heads:
                gather(p, h + 1).start()
            else:
                for copy in inputs(following, 1 - p):
                    copy.wait()
                gather(1 - p, 0).start()
            gather(p, h).wait()

            def dots(ci, accs):
                xc = x_v[p, pl.ds(ci * SC_LANES, SC_LANES)]
                out = []
                for r in range(PEER_TOPK):
                    w = rows_v[slot, r, pl.ds(ci * SC_LANES, SC_LANES)]
                    out.append(accs[r] + lax.bitcast_convert_type(w << 16, F32) * xc)
                return tuple(out)
            accs = lax.fori_loop(0, chunks, dots, tuple(jnp.zeros((SC_LANES,), F32) for _ in range(PEER_TOPK)))
            lane = lax.iota(I32, SC_LANES)
            score = jnp.zeros((SC_LANES,), F32)
            for r in range(PEER_TOPK):
                score = jnp.where(lane == r, jnp.sum(accs[r]), score)
            inner = 0.7978845608028654 * (score + 0.044715 * score * score * score)
            tanh = 1.0 - 2.0 / (jnp.exp(2.0 * inner) + 1.0)
            act = 0.5 * score * (1.0 + tanh) * gate_v[p, pl.ds(h * PEER_TOPK, PEER_TOPK)]
            weights = [jnp.sum(jnp.where(lane == r, act, 0.0)) for r in range(PEER_TOPK)]

            def mix(ci, c2):
                part = [out_v[p, pl.ds(ci * SC_LANES, SC_LANES)]] + [jnp.zeros((SC_LANES,), F32)] * 3
                for r in range(PEER_TOPK):
                    w = rows_v[slot, r, pl.ds(ci * SC_LANES, SC_LANES)]
                    part[r % 4] = part[r % 4] + weights[r] * lax.bitcast_convert_type(
                        w & jnp.uint32(0xFFFF0000), F32)
                out_v[p, pl.ds(ci * SC_LANES, SC_LANES)] = (part[0] + part[1]) + (part[2] + part[3])
                return c2
            lax.fori_loop(0, chunks, mix, 0)
        result(g, p).start()

    for copy in inputs(first, 0):
        copy.start()
    for copy in inputs(first, 0):
        copy.wait()
    gather(0, 0).start()

    def pair(k, carry):
        token(first + 2 * k, 0, k > 0)
        token(first + 2 * k + 1, 1, k > 0)
        return carry
    lax.fori_loop(0, per_worker // 2, pair, 0)

    gather(0, 0).wait()
    result(last, 0).wait()
    result(last, 1).wait()


def sc_peer(x_flat, idx_flat, gate_flat, table2d):
    from jax.experimental.pallas import tpu_sc as plsc
    d = table2d.shape[1]
    tokens = x_flat.shape[0] // d
    sel = idx_flat.shape[0] // tokens
    per_worker = tokens // SC_WORKERS
    mesh = plsc.VectorSubcoreMesh(core_axis_name="c", subcore_axis_name="s")
    return pl.kernel(
        functools.partial(_sc_peer_body, per_worker=per_worker, d=d, sel=sel),
        out_type=jax.ShapeDtypeStruct((tokens * d,), F32),
        mesh=mesh,
        scratch_types=[pltpu.VMEM((2, d), F32), pltpu.VMEM((2, sel), I32), pltpu.VMEM((2, sel), F32),
                       pltpu.VMEM((2, d), F32), pltpu.VMEM((2, PEER_TOPK, d), jnp.uint32),
                       pltpu.SemaphoreType.DMA((2,)), pltpu.SemaphoreType.DMA((2, 3)),
                       pltpu.SemaphoreType.DMA((2,))],
        compiler_params=pltpu.CompilerParams(needs_layout_passes=False),
        name="sc_peer",
    )(x_flat, idx_flat, gate_flat, table2d)


def _block_diag(pool_w):
    g, c, _ = pool_w.shape
    out = jnp.zeros((g * c, g * c), pool_w.dtype)
    for k in range(g):
        out = out.at[k * c:(k + 1) * c, k * c:(k + 1) * c].set(pool_w[k])
    return out


def _pack_rows(u, v):
    bits = lambda a: lax.bitcast_convert_type(a.astype(BF16), jnp.uint16).astype(jnp.uint32)
    return bits(u) | (bits(v) << 16)


def _residual_ln_kernel(x_ref, f_ref, g_ref, b_ref, base_ref, o_ref, *, alpha):
    del base_ref
    o_ref[...] = _layer_norm(alpha * x_ref[...] + f_ref[...], g_ref[...], b_ref[...])


def residual_ln(x2d, f2d, g, b, alpha, tm, base):
    t, d = x2d.shape
    first_block = (base.shape[0] - t) // tm
    return pl.pallas_call(
        functools.partial(_residual_ln_kernel, alpha=alpha),
        grid=(t // tm,),
        in_specs=[pl.BlockSpec((tm, d), lambda i: (i, 0)),
                  pl.BlockSpec((tm, d), lambda i: (i, 0)),
                  pl.BlockSpec((1, d), lambda i: (0, 0)),
                  pl.BlockSpec((1, d), lambda i: (0, 0)),
                  pl.BlockSpec(memory_space=pl.ANY)],
        out_specs=pl.BlockSpec((tm, d), lambda i: (i + first_block, 0)),
        out_shape=jax.ShapeDtypeStruct(base.shape, F32),
        input_output_aliases={4: 0},
        compiler_params=_cparams("parallel"),
        name="residual_ln",
    )(x2d, f2d, g.reshape(1, d), b.reshape(1, d), base)


def _sc_batches(b):
    return b // 2


def kernel(x, w_in, conv_a_w, conv_a_b, norm_a_g, norm_a_b, pool_w, pool_scale, conv_c_w, w_out,
           ln1_g, ln1_b, peer_wq, peer_keys, peer_u, peer_v, ln2_g, ln2_b):
    depth = w_in.shape[0]
    b, s, d = x.shape
    alpha = float((2 * depth) ** 0.25)
    w = conv_a_w.shape[2]
    tm = min(512, s)
    ts = min(512, s)
    tq = min(256, s)
    t_route = min(256, s)
    peer_half = 16
    row = lax.broadcasted_iota(I32, (tq, tq), 0)
    col = lax.broadcasted_iota(I32, (tq, tq), 1)
    later = (row > col).astype(BF16)
    b_tc = b - _sc_batches(b)
    for l in range(depth):
        w_in_l, w_out_l = w_in[l].astype(BF16), w_out[l].astype(BF16)
        pool_bd = _block_diag(pool_w[l]).astype(BF16)
        keys = peer_keys[l].reshape(2 * PEER_HEADS, PEER_NKEYS, -1).astype(BF16)
        wq_t = peer_wq[l].T.astype(BF16)
        words = _pack_rows(peer_u[l], peer_v[l])

        def front(xp):
            tp = xp.shape[0] * s
            xpf = xp.reshape(tp, d)
            proj3 = in_proj(xpf, w_in_l, tm).reshape(xp.shape[0], s, -1)
            y_abc = local_mix(proj3, conv_a_w[l], conv_a_b[l], norm_a_g[l], norm_a_b[l],
                              pool_bd, pool_scale[l], conv_c_w[l], ts)
            y_d = attention(proj3, later, tq, (6 * w) // LANES)
            x1 = out_ln(y_abc.reshape(tp, 3 * w), y_d.reshape(tp, -1), xpf, w_out_l,
                        ln1_g[l], ln1_b[l], alpha, tm)
            idx, gates = route(x1, wq_t, keys, t_route)
            return x1, idx, gates

        if b_tc < b:
            x1_sc, idx_sc, gates_sc = front(x[b_tc:])
            f_sc = sc_peer(x1_sc.reshape(-1), idx_sc.reshape(-1), gates_sc.reshape(-1), words)
        x1_tc, idx_tc, gates_tc = front(x[:b_tc])
        out = peer(x1_tc, idx_tc, gates_tc, words[:, None, :], ln2_g[l], ln2_b[l], alpha, peer_half,
                   out_rows=b * s)
        if b_tc < b:
            out = residual_ln(x1_sc, f_sc.reshape(-1, d), ln2_g[l], ln2_b[l], alpha, tm, out)
        x = out.reshape(b, s, d)
    return x
```
